```python
import jax, jax.numpy as jnp
from jax import lax
import numpy as np

D_MODEL = 4096
BATCH = 2
SEQ = 8192
DEPTH = 2

BRANCH_WIDTH = D_MODEL // 4
N_BRANCHES = 3
GMLP_CHUNK = 128
GMLP_GROUP_DIM = 128
GMLP_GROUPS = BRANCH_WIDTH // GMLP_GROUP_DIM
MOBA_HEAD_DIM = 128
MOBA_HEADS = BRANCH_WIDTH // MOBA_HEAD_DIM
MOBA_BLOCK = 256
MOBA_TOPK = 3
MOBA_Q_CHUNK = 32
SWA_HEAD_DIM = 64
SWA_Q_HEADS = BRANCH_WIDTH // SWA_HEAD_DIM
SWA_KV_HEADS = SWA_Q_HEADS // 8
SWA_WINDOW = 128
SWA_KV_WIDTH = SWA_KV_HEADS * SWA_HEAD_DIM
IN_COLS = 6 * BRANCH_WIDTH + 2 * SWA_KV_WIDTH
D_FF = 2 * D_MODEL
N_ALIBI_HEADS = SWA_Q_HEADS + MOBA_HEADS
RMS_EPS = 1e-6
LN_EPS = 1e-5

kernel_name = "hybrid_gmlp_moba_swa_macaron"


def rms_norm(x, g):
    xf = x.astype(jnp.float32)
    y = xf * lax.rsqrt(jnp.mean(xf * xf, axis=-1, keepdims=True) + RMS_EPS)
    return (y * g.astype(jnp.float32)).astype(x.dtype)


def swiglu(h, w_up, w_down):
    gate, up = jnp.split(h @ w_up, 2, axis=-1)
    return (jax.nn.silu(gate) * up) @ w_down


def alibi_slopes():
    i = jnp.arange(1, N_ALIBI_HEADS + 1, dtype=jnp.float32)
    s = jnp.exp2(-8.0 * i / N_ALIBI_HEADS)
    return s[:SWA_Q_HEADS], s[SWA_Q_HEADS:]


def chunked_gmlp(z, ln_g, ln_b, w_s, b_s):
    b, s, _ = z.shape
    z = jax.nn.gelu(z, approximate=False)
    u, v = jnp.split(z, 2, axis=-1)
    vf = v.astype(jnp.float32)
    mu = jnp.mean(vf, axis=-1, keepdims=True)
    var = jnp.mean(jnp.square(vf - mu), axis=-1, keepdims=True)
    vn = ((vf - mu) * lax.rsqrt(var + LN_EPS) * ln_g.astype(jnp.float32) + ln_b.astype(jnp.float32)).astype(v.dtype)
    nc = s // GMLP_CHUNK
    vc = vn.reshape(b, nc, GMLP_CHUNK, GMLP_GROUPS, GMLP_GROUP_DIM)
    causal = jnp.tril(jnp.ones((GMLP_CHUNK, GMLP_CHUNK), dtype=bool))
    w = jnp.where(causal[None], w_s, jnp.zeros_like(w_s))
    mixed = jnp.einsum('gts,bcsgd->bctgd', w, vc) + b_s.T[None, None, :, :, None]
    return u * mixed.reshape(b, s, BRANCH_WIDTH)


def moba_attention(q, k, v, slopes):
    b, s, h, dh = q.shape
    nb = s // MOBA_BLOCK
    topk = min(MOBA_TOPK, nb)
    scale = dh ** -0.5
    k_blk = k.reshape(b, nb, MOBA_BLOCK, h, dh)
    v_blk = v.reshape(b, nb, MOBA_BLOCK, h, dh)
    k_mean = jnp.mean(k_blk.astype(jnp.float32), axis=2)
    gate = jnp.einsum('bthd,bnhd->bhtn', q.astype(jnp.float32), k_mean)
    cur_blk = jnp.arange(s) // MOBA_BLOCK
    past = jnp.arange(nb)[None, :] < cur_blk[:, None]
    gate = jnp.where(past[None, None], gate, -jnp.inf)
    _, sel = lax.top_k(gate, topk)
    sel_valid = jnp.arange(topk)[None, :] < cur_blk[:, None]

    k_bt = k_blk.transpose(0, 3, 1, 2, 4)
    v_bt = v_blk.transpose(0, 3, 1, 2, 4)
    nq = s // MOBA_Q_CHUNK
    q_c = q.reshape(b, nq, MOBA_Q_CHUNK, h, dh).transpose(1, 0, 3, 2, 4)
    sel_c = sel.reshape(b, h, nq, MOBA_Q_CHUNK, topk).transpose(2, 0, 1, 3, 4)
    valid_c = sel_valid.reshape(nq, MOBA_Q_CHUNK, topk)
    bi = jnp.arange(b)[:, None, None, None]
    hi = jnp.arange(h)[None, :, None, None]
    blk_off = jnp.arange(MOBA_BLOCK)
    n_sel = topk * MOBA_BLOCK

    def one_chunk(args):
        qc, selc, validc, ci = args
        t = ci * MOBA_Q_CHUNK + jnp.arange(MOBA_Q_CHUNK)
        own = (ci * MOBA_Q_CHUNK) // MOBA_BLOCK
        k_sel = k_bt[bi, hi, selc]
        v_sel = v_bt[bi, hi, selc]
        s_sel = jnp.einsum('bhtd,bhtjsd->bhtjs', qc, k_sel).astype(jnp.float32) * scale
        dist_sel = (t[None, None, :, None, None] - (selc[..., None] * MOBA_BLOCK + blk_off)).astype(jnp.float32)
        s_sel = s_sel - slopes[None, :, None, None, None] * dist_sel
        s_sel = jnp.where(validc[None, None, :, :, None], s_sel, -jnp.inf)
        k_own = lax.dynamic_index_in_dim(k_bt, own, axis=2, keepdims=False)
        v_own = lax.dynamic_index_in_dim(v_bt, own, axis=2, keepdims=False)
        s_own = jnp.einsum('bhtd,bhsd->bhts', qc, k_own).astype(jnp.float32) * scale
        dist_own = t[:, None] - (own * MOBA_BLOCK + blk_off)[None, :]
        s_own = jnp.where((dist_own >= 0)[None, None],
                          s_own - slopes[None, :, None, None] * dist_own.astype(jnp.float32)[None, None],
                          -jnp.inf)
        logits = jnp.concatenate([s_sel.reshape(b, h, MOBA_Q_CHUNK, n_sel), s_own], axis=-1)
        p = jax.nn.softmax(logits, axis=-1).astype(v.dtype)
        p_sel = p[..., :n_sel].reshape(b, h, MOBA_Q_CHUNK, topk, MOBA_BLOCK)
        p_own = p[..., n_sel:]
        return (jnp.einsum('bhtjs,bhtjsd->bhtd', p_sel, v_sel)
                + jnp.einsum('bhts,bhsd->bhtd', p_own, v_own))

    out = lax.map(one_chunk, (q_c, sel_c, valid_c, jnp.arange(nq)))
    return out.transpose(1, 0, 3, 2, 4).reshape(b, s, h * dh)


def swa_sink_attention(q, k, v, sinks, slopes):
    b, s, hq, dh = q.shape
    hkv = k.shape[2]
    g = hq // hkv
    w = SWA_WINDOW
    nblk = s // w
    qb = q.reshape(b, nblk, w, hkv, g, dh)

    def band(x):
        xp = jnp.pad(x, ((0, 0), (w, 0), (0, 0), (0, 0)))
        prev = xp[:, :s].reshape(b, nblk, w, hkv, dh)
        cur = x.reshape(b, nblk, w, hkv, dh)
        return jnp.concatenate([prev, cur], axis=2)

    kb, vb = band(k), band(v)
    scores = jnp.einsum('bntkgd,bnskd->bnkgts', qb, kb).astype(jnp.float32) * (dh ** -0.5)
    t_loc = jnp.arange(w)
    s_loc = jnp.arange(2 * w)
    dist = t_loc[:, None] + w - s_loc[None, :]
    in_window = (dist >= 0) & (dist < w)
    key_pos = jnp.arange(nblk)[:, None] * w - w + s_loc[None, :]
    valid = in_window[None] & (key_pos >= 0)[:, None, :]
    slopes_kg = slopes.reshape(hkv, g)
    scores = scores - slopes_kg[None, None, :, :, None, None] * dist.astype(jnp.float32)
    scores = jnp.where(valid[None, :, None, None], scores, -jnp.inf)
    sink = jnp.broadcast_to(sinks.astype(jnp.float32).reshape(hkv, g)[None, None, :, :, None, None],
                            scores.shape[:-1] + (1,))
    p = jax.nn.softmax(jnp.concatenate([scores, sink], axis=-1), axis=-1)[..., :2 * w].astype(v.dtype)
    o = jnp.einsum('bnkgts,bnskd->bntkgd', p, vb)
    return o.reshape(b, s, hq * dh)


def mixing_sublayer(h, w_in, gmlp_ln_g, gmlp_ln_b, gmlp_w_s, gmlp_b_s, swa_sinks,
                    w_gate, b_gate, w_branch, w_out):
    b, s, _ = h.shape
    proj = h @ w_in
    bw = BRANCH_WIDTH
    z_a, qkv_b, q_c, k_c, v_c = jnp.split(
        proj, [2 * bw, 5 * bw, 6 * bw, 6 * bw + SWA_KV_WIDTH], axis=-1)
    swa_slopes, moba_slopes = alibi_slopes()
    y_a = chunked_gmlp(z_a, gmlp_ln_g, gmlp_ln_b, gmlp_w_s, gmlp_b_s)
    qkv_b = qkv_b.reshape(b, s, 3, MOBA_HEADS, MOBA_HEAD_DIM)
    y_b = moba_attention(qkv_b[:, :, 0], qkv_b[:, :, 1], qkv_b[:, :, 2], moba_slopes)
    y_c = swa_sink_attention(q_c.reshape(b, s, SWA_Q_HEADS, SWA_HEAD_DIM),
                             k_c.reshape(b, s, SWA_KV_HEADS, SWA_HEAD_DIM),
                             v_c.reshape(b, s, SWA_KV_HEADS, SWA_HEAD_DIM),
                             swa_sinks, swa_slopes)
    merged = None
    for n, y in enumerate((y_a, y_b, y_c)):
        gate = jax.nn.sigmoid(h @ w_gate[:, n * D_MODEL:(n + 1) * D_MODEL]
                              + b_gate[n * D_MODEL:(n + 1) * D_MODEL])
        term = gate * (y @ w_branch[n])
        merged = term if merged is None else merged + term
    return merged @ w_out


def setup_inputs(seed: int = 0) -> dict:
    key = jax.random.key(seed)
    ks = jax.random.split(key, 24)
    f32 = jnp.float32

    def nrm(k, shape, fan_in):
        return jax.random.normal(k, shape, f32) * (fan_in ** -0.5)

    def gain(k, shape):
        return 1.0 + 0.05 * jax.random.normal(k, shape, f32)

    L, D = DEPTH, D_MODEL
    return {
        "x": jax.random.normal(ks[0], (BATCH, SEQ, D), f32),
        "ffn1_pre_g": gain(ks[1], (L, D)),
        "ffn1_w_up": nrm(ks[2], (L, D, 2 * D_FF), D),
        "ffn1_w_down": nrm(ks[3], (L, D_FF, D), D_FF),
        "ffn1_post_g": gain(ks[4], (L, D)),
        "mix_pre_g": gain(ks[5], (L, D)),
        "w_in": nrm(ks[6], (L, D, IN_COLS), D),
        "gmlp_ln_g": gain(ks[7], (L, BRANCH_WIDTH)),
        "gmlp_ln_b": 0.02 * jax.random.normal(ks[8], (L, BRANCH_WIDTH), f32),
        "gmlp_w_s": nrm(ks[9], (L, GMLP_GROUPS, GMLP_CHUNK, GMLP_CHUNK), GMLP_CHUNK),
        "gmlp_b_s": 1.0 + 0.1 * jax.random.normal(ks[10], (L, GMLP_GROUPS, GMLP_CHUNK), f32),
        "swa_sinks": 0.5 * jax.random.normal(ks[11], (L, SWA_Q_HEADS), f32),
        "w_gate": nrm(ks[12], (L, D, N_BRANCHES * D), D),
        "b_gate": 0.01 * jax.random.normal(ks[13], (L, N_BRANCHES * D), f32),
        "w_branch": nrm(ks[14], (L, N_BRANCHES, BRANCH_WIDTH, D), BRANCH_WIDTH),
        "w_out": nrm(ks[15], (L, D, D), D),
        "mix_post_g": gain(ks[16], (L, D)),
        "ffn2_pre_g": gain(ks[17], (L, D)),
        "ffn2_w_up": nrm(ks[18], (L, D, 2 * D_FF), D),
        "ffn2_w_down": nrm(ks[19], (L, D_FF, D), D_FF),
        "ffn2_post_g": gain(ks[20], (L, D)),
    }


def reference(x, ffn1_pre_g, ffn1_w_up, ffn1_w_down, ffn1_post_g, mix_pre_g, w_in,
              gmlp_ln_g, gmlp_ln_b, gmlp_w_s, gmlp_b_s, swa_sinks, w_gate, b_gate,
              w_branch, w_out, mix_post_g, ffn2_pre_g, ffn2_w_up, ffn2_w_down, ffn2_post_g):
    for i in range(DEPTH):
        x = x + 0.5 * rms_norm(swiglu(rms_norm(x, ffn1_pre_g[i]), ffn1_w_up[i], ffn1_w_down[i]),
                               ffn1_post_g[i])
        h = rms_norm(x, mix_pre_g[i])
        m = mixing_sublayer(h, w_in[i], gmlp_ln_g[i], gmlp_ln_b[i], gmlp_w_s[i], gmlp_b_s[i],
                            swa_sinks[i], w_gate[i], b_gate[i], w_branch[i], w_out[i])
        x = x + rms_norm(m, mix_post_g[i])
        x = x + 0.5 * rms_norm(swiglu(rms_norm(x, ffn2_pre_g[i]), ffn2_w_up[i], ffn2_w_down[i]),
                               ffn2_post_g[i])
    return x
```

```python
import functools
import math

import jax
import jax.numpy as jnp
from jax import lax
from jax.experimental import pallas as pl
from jax.experimental.pallas import tpu as pltpu

D_MODEL = 4096
DEPTH = 2
BRANCH_WIDTH = D_MODEL // 4
N_BRANCHES = 3
GMLP_CHUNK = 128
GMLP_GROUP_DIM = 128
GMLP_GROUPS = BRANCH_WIDTH // GMLP_GROUP_DIM
MOBA_HEAD_DIM = 128
MOBA_HEADS = BRANCH_WIDTH // MOBA_HEAD_DIM
MOBA_BLOCK = 256
MOBA_TOPK = 3
SWA_HEAD_DIM = 64
SWA_Q_HEADS = BRANCH_WIDTH // SWA_HEAD_DIM
SWA_KV_HEADS = SWA_Q_HEADS // 8
SWA_WINDOW = 128
SWA_KV_WIDTH = SWA_KV_HEADS * SWA_HEAD_DIM
IN_COLS = 6 * BRANCH_WIDTH + 2 * SWA_KV_WIDTH
D_FF = 2 * D_MODEL
N_ALIBI_HEADS = SWA_Q_HEADS + MOBA_HEADS
RMS_EPS = 1e-6
LN_EPS = 1e-5

LANES = 128
VMEM_LIMIT = 56 * 1024 * 1024
MASKED = -1e30

F32 = jnp.float32
BF16 = jnp.bfloat16

COL_GMLP = 0
COL_MOBA_Q = 2 * BRANCH_WIDTH
COL_MOBA_K = 3 * BRANCH_WIDTH
COL_MOBA_V = 4 * BRANCH_WIDTH
COL_SWA_Q = 5 * BRANCH_WIDTH
COL_SWA_K = 6 * BRANCH_WIDTH
COL_SWA_V = 6 * BRANCH_WIDTH + SWA_KV_WIDTH


def _params(*semantics):
    return pltpu.CompilerParams(dimension_semantics=semantics, vmem_limit_bytes=VMEM_LIMIT)


def _rms_scale(x):
    return lax.rsqrt(jnp.mean(x * x, axis=-1, keepdims=True) + RMS_EPS)


def _dot_nt(a, b):
    return lax.dot_general(a, b, (((1,), (1,)), ((), ())), preferred_element_type=F32)


def _prenorm_kernel(x_ref, g_ref, h_ref):
    x = x_ref[...]
    h_ref[...] = (x * _rms_scale(x) * g_ref[...]).astype(h_ref.dtype)


def prenorm(x, g, *, tm=256):
    t, d = x.shape
    return pl.pallas_call(
        _prenorm_kernel,
        grid=(t // tm,),
        in_specs=[pl.BlockSpec((tm, d), lambda i: (i, 0)),
                  pl.BlockSpec((1, d), lambda i: (0, 0))],
        out_specs=pl.BlockSpec((tm, d), lambda i: (i, 0)),
        out_shape=jax.ShapeDtypeStruct((t, d), BF16),
        compiler_params=_params("parallel"),
        name="prenorm",
    )(x, g.reshape(1, d))


def _ffn_up_kernel(h_ref, wg_ref, wu_ref, o_ref):
    h = h_ref[...]
    g = jnp.dot(h, wg_ref[...], preferred_element_type=F32)
    u = jnp.dot(h, wu_ref[...], preferred_element_type=F32)
    o_ref[...] = (g * jax.nn.sigmoid(g) * u).astype(o_ref.dtype)


def ffn_up(h, w_up, *, tm=512, tn=512):
    t, d = h.shape
    f = w_up.shape[1] // 2
    nj = f // tn
    return pl.pallas_call(
        _ffn_up_kernel,
        grid=(t // tm, nj),
        in_specs=[pl.BlockSpec((tm, d), lambda i, j: (i, 0)),
                  pl.BlockSpec((d, tn), lambda i, j: (0, j)),
                  pl.BlockSpec((d, tn), lambda i, j: (0, j + nj))],
        out_specs=pl.BlockSpec((tm, tn), lambda i, j: (i, j)),
        out_shape=jax.ShapeDtypeStruct((t, f), BF16),
        compiler_params=_params("parallel", "arbitrary"),
        name="ffn_up",
    )(h, w_up, w_up)


def _mm_norm_res_kernel(a_ref, w_ref, x_ref, pg_ref, ng_ref, xo_ref, *rest, res_scale, nk, rows, emit_h):
    if emit_h:
        ho_ref, acc_ref = rest
    else:
        (acc_ref,) = rest
    k = pl.program_id(1)
    part = jnp.dot(a_ref[...], w_ref[...], preferred_element_type=F32)

    @pl.when(k == 0)
    def _():
        acc_ref[...] = part

    @pl.when(k > 0)
    def _():
        acc_ref[...] += part

    @pl.when(k == nk - 1)
    def _():
        tm = acc_ref.shape[0]

        def chunk(c, carry):
            r = pl.ds(pl.multiple_of(c * rows, rows), rows)
            y = acc_ref[r, :]
            xn = x_ref[r, :] + res_scale * (y * _rms_scale(y) * pg_ref[...])
            xo_ref[r, :] = xn
            if emit_h:
                ho_ref[r, :] = (xn * _rms_scale(xn) * ng_ref[...]).astype(ho_ref.dtype)
            return carry

        lax.fori_loop(0, tm // rows, chunk, 0)


def mm_norm_res(a, w, x, post_g, next_g, *, res_scale, emit_h=True, tm=512, tk=256, rows=64):
    t, kdim = a.shape
    d = w.shape[1]
    nk = kdim // tk
    once = pl.Buffered(1)
    out_shape = [jax.ShapeDtypeStruct((t, d), F32)]
    out_specs = [pl.BlockSpec((tm, d), lambda i, k: (i, 0))]
    if emit_h:
        out_shape.append(jax.ShapeDtypeStruct((t, d), BF16))
        out_specs.append(pl.BlockSpec((tm, d), lambda i, k: (i, 0)))
    res = pl.pallas_call(
        functools.partial(_mm_norm_res_kernel, res_scale=res_scale, nk=nk, rows=rows, emit_h=emit_h),
        grid=(t // tm, nk),
        in_specs=[pl.BlockSpec((tm, tk), lambda i, k: (i, k)),
                  pl.BlockSpec((tk, d), lambda i, k: (k, 0)),
                  pl.BlockSpec((tm, d), lambda i, k: (i, 0), pipeline_mode=once),
                  pl.BlockSpec((1, d), lambda i, k: (0, 0)),
                  pl.BlockSpec((1, d), lambda i, k: (0, 0))],
        out_specs=out_specs,
        out_shape=out_shape,
        scratch_shapes=[pltpu.VMEM((tm, d), F32)],
        compiler_params=_params("parallel", "arbitrary"),
        name="mm_norm_res",
    )(a, w, x, post_g.reshape(1, d), next_g.reshape(1, d))
    return (res[0], res[1]) if emit_h else (res[0], None)


def _matmul_kernel(a_ref, w_ref, o_ref):
    o_ref[...] = jnp.dot(a_ref[...], w_ref[...], preferred_element_type=F32).astype(o_ref.dtype)


def matmul(a, w, *, tm=512, tn=1280):
    t, kdim = a.shape
    n = w.shape[1]
    return pl.pallas_call(
        _matmul_kernel,
        grid=(t // tm, n // tn),
        in_specs=[pl.BlockSpec((tm, kdim), lambda i, j: (i, 0)),
                  pl.BlockSpec((kdim, tn), lambda i, j: (0, j))],
        out_specs=pl.BlockSpec((tm, tn), lambda i, j: (i, j)),
        out_shape=jax.ShapeDtypeStruct((t, n), BF16),
        compiler_params=_params("parallel", "arbitrary"),
        name="in_proj",
    )(a, w)


def _gelu_exact(x):
    return 0.5 * x * (1.0 + lax.erf(x * (1.0 / math.sqrt(2.0))))


def _gmlp_kernel(z_ref, lng_ref, lnb_ref, ws_ref, bs_ref, o_ref, *, chunks):
    bw = BRANCH_WIDTH
    row = lax.broadcasted_iota(jnp.int32, (GMLP_CHUNK, GMLP_CHUNK), 0)
    col = lax.broadcasted_iota(jnp.int32, (GMLP_CHUNK, GMLP_CHUNK), 1)
    causal = row >= col
    for c in range(chunks):
        r = slice(c * GMLP_CHUNK, (c + 1) * GMLP_CHUNK)
        u = _gelu_exact(z_ref[r, :bw].astype(F32))
        v = _gelu_exact(z_ref[r, bw:].astype(F32))
        mu = jnp.mean(v, axis=-1, keepdims=True)
        vc = v - mu
        var = jnp.mean(vc * vc, axis=-1, keepdims=True)
        vn = (vc * lax.rsqrt(var + LN_EPS) * lng_ref[...] + lnb_ref[...]).astype(BF16)
        for g in range(GMLP_GROUPS):
            gs = slice(g * GMLP_GROUP_DIM, (g + 1) * GMLP_GROUP_DIM)
            w = jnp.where(causal, ws_ref[g], 0.0).astype(BF16)
            mixed = jnp.dot(w, vn[:, gs], preferred_element_type=F32) + bs_ref[:, g:g + 1]
            o_ref[r, gs] = (u[:, gs] * mixed).astype(o_ref.dtype)


def gmlp(proj, ln_g, ln_b, w_s, b_s, *, chunks=4):
    t = proj.shape[0]
    bw = BRANCH_WIDTH
    tm = chunks * GMLP_CHUNK
    return pl.pallas_call(
        functools.partial(_gmlp_kernel, chunks=chunks),
        grid=(t // tm,),
        in_specs=[pl.BlockSpec((tm, 2 * bw), lambda i: (i, 0)),
                  pl.BlockSpec((1, bw), lambda i: (0, 0)),
                  pl.BlockSpec((1, bw), lambda i: (0, 0)),
                  pl.BlockSpec((GMLP_GROUPS, GMLP_CHUNK, GMLP_CHUNK), lambda i: (0, 0, 0)),
                  pl.BlockSpec((GMLP_CHUNK, GMLP_GROUPS), lambda i: (0, 0))],
        out_specs=pl.BlockSpec((tm, bw), lambda i: (i, 0)),
        out_shape=jax.ShapeDtypeStruct((t, bw), BF16),
        compiler_params=_params("parallel"),
        name="gmlp",
    )(proj, ln_g.reshape(1, bw), ln_b.reshape(1, bw), w_s, b_s.T)


def _moba_kernel(slopes_ref, q_ref, k_ref, v_ref, o_ref, kmh_ref, kml_ref, *, nb):
    blk = MOBA_BLOCK
    h = pl.program_id(1)
    qi = pl.program_id(2)
    slope = slopes_ref[h]
    scale = MOBA_HEAD_DIM ** -0.5

    @pl.when(qi == 0)
    def _():
        kmh_ref[...] = jnp.zeros_like(kmh_ref)
        kml_ref[...] = jnp.zeros_like(kml_ref)
        for n in range(nb):
            km = jnp.mean(k_ref[n * blk:(n + 1) * blk, :].astype(F32), axis=0, keepdims=True)
            hi = km.astype(BF16)
            kmh_ref[n:n + 1, :] = hi
            kml_ref[n:n + 1, :] = (km - hi.astype(F32)).astype(BF16)

    q = q_ref[...]
    gate = _dot_nt(q, kmh_ref[...]) + _dot_nt(q, kml_ref[...])
    lane = lax.broadcasted_iota(jnp.int32, (blk, LANES), 1)
    gate = jnp.where(lane < qi, gate, -jnp.inf)
    sel_bias = jnp.full((blk, LANES), MASKED, F32)
    for _ in range(MOBA_TOPK):
        best = jnp.max(gate, axis=-1, keepdims=True)
        first = jnp.min(jnp.where(gate == best, lane, LANES), axis=-1, keepdims=True)
        pick = (lane == first) & (best > -jnp.inf)
        sel_bias = jnp.where(pick, 0.0, sel_bias)
        gate = jnp.where(pick, -jnp.inf, gate)

    r_idx = lax.broadcasted_iota(jnp.int32, (blk, blk), 0)
    c_idx = lax.broadcasted_iota(jnp.int32, (blk, blk), 1)
    rel = (r_idx - c_idx).astype(F32)

    row0 = pl.multiple_of(qi * blk, blk)
    k_own = k_ref[pl.ds(row0, blk), :]
    v_own = v_ref[pl.ds(row0, blk), :]
    s = _dot_nt(q, k_own) * scale
    s = jnp.where(r_idx >= c_idx, s - slope * rel, MASKED)
    m0 = jnp.max(s, axis=-1, keepdims=True)
    p = jnp.exp(s - m0)
    l0 = jnp.sum(p, axis=-1, keepdims=True)
    acc0 = jnp.dot(p.astype(BF16), v_own, preferred_element_type=F32)

    def past_block(j, carry):
        m, l, acc = carry
        rj = pl.multiple_of(j * blk, blk)
        kj = k_ref[pl.ds(rj, blk), :]
        vj = v_ref[pl.ds(rj, blk), :]
        bias_j = jnp.sum(jnp.where(lane == j, sel_bias, 0.0), axis=-1, keepdims=True)
        dist = rel + ((qi - j) * blk).astype(F32)
        s = _dot_nt(q, kj) * scale - slope * dist + bias_j
        m_new = jnp.maximum(m, jnp.max(s, axis=-1, keepdims=True))
        alpha = jnp.exp(m - m_new)
        p = jnp.exp(s - m_new)
        l = alpha * l + jnp.sum(p, axis=-1, keepdims=True)
        acc = alpha * acc + jnp.dot(p.astype(BF16), vj, preferred_element_type=F32)
        return m_new, l, acc

    _, l, acc = lax.fori_loop(0, qi, past_block, (m0, l0, acc0))
    o_ref[...] = (acc / l).astype(o_ref.dtype)


def moba(proj, slopes, *, batch, seq):
    blk = MOBA_BLOCK
    nb = seq // blk
    dh = MOBA_HEAD_DIM
    cq, ck, cv = COL_MOBA_Q // dh, COL_MOBA_K // dh, COL_MOBA_V // dh
    return pl.pallas_call(
        functools.partial(_moba_kernel, nb=nb),
        grid_spec=pltpu.PrefetchScalarGridSpec(
            num_scalar_prefetch=1,
            grid=(batch, MOBA_HEADS, nb),
            in_specs=[pl.BlockSpec((blk, dh), lambda b, h, i, s: (b * nb + i, cq + h)),
                      pl.BlockSpec((seq, dh), lambda b, h, i, s: (b, ck + h)),
                      pl.BlockSpec((seq, dh), lambda b, h, i, s: (b, cv + h))],
            out_specs=pl.BlockSpec((blk, dh), lambda b, h, i, s: (b * nb + i, h)),
            scratch_shapes=[pltpu.VMEM((LANES, dh), BF16), pltpu.VMEM((LANES, dh), BF16)],
        ),
        out_shape=jax.ShapeDtypeStruct((batch * seq, BRANCH_WIDTH), BF16),
        compiler_params=_params("parallel", "parallel", "arbitrary"),
        name="moba",
    )(slopes, proj, proj, proj)


def _swa_kernel(slopes_ref, sinks_ref, q_ref, kp_ref, kc_ref, vp_ref, vc_ref, o_ref):
    w = SWA_WINDOW
    dh = SWA_HEAD_DIM
    n = pl.program_id(1)
    scale = dh ** -0.5
    lane = lax.broadcasted_iota(jnp.int32, (w, LANES), 1)
    low = lane < dh
    t_idx = lax.broadcasted_iota(jnp.int32, (w, w), 0)
    s_idx = lax.broadcasted_iota(jnp.int32, (w, w), 1)
    dist_cur = (t_idx - s_idx).astype(F32)
    dist_prev = dist_cur + float(w)
    ok_cur = s_idx <= t_idx
    ok_prev = (s_idx > t_idx) & (n > 0)

    def halves(ref):
        x = ref[...].astype(F32)
        xr = pltpu.roll(x, dh, axis=1)
        z = jnp.zeros_like(x)
        lo = (jnp.where(low, x, z).astype(BF16), jnp.where(low, z, xr).astype(BF16))
        hi = (jnp.where(low, xr, z).astype(BF16), jnp.where(low, z, x).astype(BF16))
        return lo, hi

    kp, kc, vp, vc = halves(kp_ref), halves(kc_ref), halves(vp_ref), halves(vc_ref)
    pairs = SWA_Q_HEADS // 2
    group = SWA_Q_HEADS // SWA_KV_HEADS
    for pr in range(pairs):
        q = q_ref[:, pr * LANES:(pr + 1) * LANES]
        kh = (2 * pr) // group
        out = None
        for half in range(2):
            hq = 2 * pr + half
            slope = slopes_ref[hq]
            sink = sinks_ref[hq]
            s_p = _dot_nt(q, kp[kh][half]) * scale
            s_c = _dot_nt(q, kc[kh][half]) * scale
            s_p = jnp.where(ok_prev, s_p - slope * dist_prev, MASKED)
            s_c = jnp.where(ok_cur, s_c - slope * dist_cur, MASKED)
            m = jnp.maximum(jnp.max(s_p, axis=-1, keepdims=True), jnp.max(s_c, axis=-1, keepdims=True))
            m = jnp.maximum(m, sink)
            e_p = jnp.exp(s_p - m)
            e_c = jnp.exp(s_c - m)
            denom = (jnp.sum(e_p, axis=-1, keepdims=True) + jnp.sum(e_c, axis=-1, keepdims=True)
                     + jnp.exp(sink - m))
            inv = 1.0 / denom
            o = (jnp.dot((e_p * inv).astype(BF16), vp[kh][half], preferred_element_type=F32)
                 + jnp.dot((e_c * inv).astype(BF16), vc[kh][half], preferred_element_type=F32))
            out = o if out is None else out + o
        o_ref[:, pr * LANES:(pr + 1) * LANES] = out.astype(o_ref.dtype)


def swa(proj, slopes, sinks, *, batch, seq):
    w = SWA_WINDOW
    nblk = seq // w
    bw = BRANCH_WIDTH
    cq = COL_SWA_Q // bw
    ck = COL_SWA_K // SWA_KV_WIDTH
    cv = COL_SWA_V // SWA_KV_WIDTH
    prev = lambda b, n, *_: (b * nblk + jnp.maximum(n - 1, 0))
    cur = lambda b, n, *_: (b * nblk + n)
    return pl.pallas_call(
        _swa_kernel,
        grid_spec=pltpu.PrefetchScalarGridSpec(
            num_scalar_prefetch=2,
            grid=(batch, nblk),
            in_specs=[pl.BlockSpec((w, bw), lambda b, n, *_: (cur(b, n), cq)),
                      pl.BlockSpec((w, SWA_KV_WIDTH), lambda b, n, *_: (prev(b, n), ck)),
                      pl.BlockSpec((w, SWA_KV_WIDTH), lambda b, n, *_: (cur(b, n), ck)),
                      pl.BlockSpec((w, SWA_KV_WIDTH), lambda b, n, *_: (prev(b, n), cv)),
                      pl.BlockSpec((w, SWA_KV_WIDTH), lambda b, n, *_: (cur(b, n), cv))],
            out_specs=pl.BlockSpec((w, bw), lambda b, n, *_: (cur(b, n), 0)),
        ),
        out_shape=jax.ShapeDtypeStruct((batch * seq, bw), BF16),
        compiler_params=_params("parallel", "arbitrary"),
        name="swa",
    )(slopes, sinks, proj, proj, proj, proj, proj)


def _merge_kernel(h_ref, wg_ref0, wg_ref1, wg_ref2, bg_ref, ya_ref, yb_ref, yc_ref, wb_ref, o_ref):
    h = h_ref[...]
    merged = None
    for n, (wg_ref, y_ref) in enumerate(((wg_ref0, ya_ref), (wg_ref1, yb_ref), (wg_ref2, yc_ref))):
        logits = jnp.dot(h, wg_ref[...], preferred_element_type=F32) + bg_ref[n]
        branch = jnp.dot(y_ref[...], wb_ref[n], preferred_element_type=F32)
        term = jax.nn.sigmoid(logits) * branch
        merged = term if merged is None else merged + term
    o_ref[...] = merged.astype(o_ref.dtype)


def merge(h, w_gate, b_gate, y_a, y_b, y_c, w_branch, *, tm=512, tn=256):
    t, d = h.shape
    bw = BRANCH_WIDTH
    nj = d // tn
    gate_spec = lambda n: pl.BlockSpec((d, tn), lambda i, j: (0, n * nj + j))
    y_spec = pl.BlockSpec((tm, bw), lambda i, j: (i, 0))
    return pl.pallas_call(
        _merge_kernel,
        grid=(t // tm, nj),
        in_specs=[pl.BlockSpec((tm, d), lambda i, j: (i, 0)),
                  gate_spec(0), gate_spec(1), gate_spec(2),
                  pl.BlockSpec((N_BRANCHES, 1, tn), lambda i, j: (0, 0, j)),
                  y_spec, y_spec, y_spec,
                  pl.BlockSpec((N_BRANCHES, bw, tn), lambda i, j: (0, 0, j))],
        out_specs=pl.BlockSpec((tm, tn), lambda i, j: (i, j)),
        out_shape=jax.ShapeDtypeStruct((t, d), BF16),
        compiler_params=_params("parallel", "arbitrary"),
        name="merge",
    )(h, w_gate, w_gate, w_gate, b_gate.reshape(N_BRANCHES, 1, d), y_a, y_b, y_c, w_branch)


def _alibi_slopes():
    i = jnp.arange(1, N_ALIBI_HEADS + 1, dtype=F32)
    s = jnp.exp2(-8.0 * i / N_ALIBI_HEADS)
    return s[:SWA_Q_HEADS], s[SWA_Q_HEADS:]


def kernel(x, ffn1_pre_g, ffn1_w_up, ffn1_w_down, ffn1_post_g, mix_pre_g, w_in, gmlp_ln_g, gmlp_ln_b,
           gmlp_w_s, gmlp_b_s, swa_sinks, w_gate, b_gate, w_branch, w_out, mix_post_g, ffn2_pre_g,
           ffn2_w_up, ffn2_w_down, ffn2_post_g):
    batch, seq, d = x.shape
    depth = ffn1_pre_g.shape[0]
    swa_slopes, moba_slopes = _alibi_slopes()
    xf = x.reshape(batch * seq, d)
    h = prenorm(xf, ffn1_pre_g[0])
    for i in range(depth):
        act = ffn_up(h, ffn1_w_up[i].astype(BF16))
        xf, h = mm_norm_res(act, ffn1_w_down[i].astype(BF16), xf, ffn1_post_g[i], mix_pre_g[i], res_scale=0.5)

        proj = matmul(h, w_in[i].astype(BF16))
        y_a = gmlp(proj, gmlp_ln_g[i], gmlp_ln_b[i], gmlp_w_s[i], gmlp_b_s[i])
        y_b = moba(proj, moba_slopes, batch=batch, seq=seq)
        y_c = swa(proj, swa_slopes, swa_sinks[i], batch=batch, seq=seq)
        merged = merge(h, w_gate[i].astype(BF16), b_gate[i], y_a, y_b, y_c, w_branch[i].astype(BF16))
        xf, h = mm_norm_res(merged, w_out[i].astype(BF16), xf, mix_post_g[i], ffn2_pre_g[i], res_scale=1.0)

        act = ffn_up(h, ffn2_w_up[i].astype(BF16))
        last = i == depth - 1
        next_g = ffn2_post_g[i] if last else ffn1_pre_g[i + 1]
        xf, h = mm_norm_res(act, ffn2_w_down[i].astype(BF16), xf, ffn2_post_g[i], next_g,
                            res_scale=0.5, emit_h=not last)
    return xf.reshape(batch, seq, d)
```

```python
import functools
import math

import jax
import jax.numpy as jnp
from jax import lax
from jax.experimental import pallas as pl
from jax.experimental.pallas import tpu as pltpu

D_MODEL = 4096
DEPTH = 2
BRANCH_WIDTH = D_MODEL // 4
N_BRANCHES = 3
GMLP_CHUNK = 128
GMLP_GROUP_DIM = 128
GMLP_GROUPS = BRANCH_WIDTH // GMLP_GROUP_DIM
MOBA_HEAD_DIM = 128
MOBA_HEADS = BRANCH_WIDTH // MOBA_HEAD_DIM
MOBA_BLOCK = 256
MOBA_TOPK = 3
SWA_HEAD_DIM = 64
SWA_Q_HEADS = BRANCH_WIDTH // SWA_HEAD_DIM
SWA_KV_HEADS = SWA_Q_HEADS // 8
SWA_WINDOW = 128
SWA_KV_WIDTH = SWA_KV_HEADS * SWA_HEAD_DIM
IN_COLS = 6 * BRANCH_WIDTH + 2 * SWA_KV_WIDTH
D_FF = 2 * D_MODEL
N_ALIBI_HEADS = SWA_Q_HEADS + MOBA_HEADS
RMS_EPS = 1e-6
LN_EPS = 1e-5

LANES = 128
VMEM_LIMIT = 56 * 1024 * 1024
MASKED = -1e30

F32 = jnp.float32
BF16 = jnp.bfloat16

COL_GMLP = 0
COL_MOBA_Q = 2 * BRANCH_WIDTH
COL_MOBA_K = 3 * BRANCH_WIDTH
COL_MOBA_V = 4 * BRANCH_WIDTH
COL_SWA_Q = 5 * BRANCH_WIDTH
COL_SWA_K = 6 * BRANCH_WIDTH
COL_SWA_V = 6 * BRANCH_WIDTH + SWA_KV_WIDTH


def _params(*semantics):
    return pltpu.CompilerParams(dimension_semantics=semantics, vmem_limit_bytes=VMEM_LIMIT)


def _rms_scale(x):
    return lax.rsqrt(jnp.mean(x * x, axis=-1, keepdims=True) + RMS_EPS)


def _dot_nt(a, b):
    return lax.dot_general(a, b, (((1,), (1,)), ((), ())), preferred_element_type=F32)


def _prenorm_kernel(x_ref, g_ref, h_ref):
    x = x_ref[...]
    h_ref[...] = (x * _rms_scale(x) * g_ref[...]).astype(h_ref.dtype)


def prenorm(x, g, *, tm=256):
    t, d = x.shape
    return pl.pallas_call(
        _prenorm_kernel,
        grid=(t // tm,),
        in_specs=[pl.BlockSpec((tm, d), lambda i: (i, 0)),
                  pl.BlockSpec((1, d), lambda i: (0, 0))],
        out_specs=pl.BlockSpec((tm, d), lambda i: (i, 0)),
        out_shape=jax.ShapeDtypeStruct((t, d), BF16),
        compiler_params=_params("parallel"),
        name="prenorm",
    )(x, g.reshape(1, d))


def _ffn_up_kernel(h_ref, wg_ref, wu_ref, o_ref):
    h = h_ref[...]
    g = jnp.dot(h, wg_ref[...], preferred_element_type=F32)
    u = jnp.dot(h, wu_ref[...], preferred_element_type=F32)
    o_ref[...] = (g * jax.nn.sigmoid(g) * u).astype(o_ref.dtype)


def ffn_up(h, w_up, *, tm=512, tn=512):
    t, d = h.shape
    f = w_up.shape[1] // 2
    nj = f // tn
    return pl.pallas_call(
        _ffn_up_kernel,
        grid=(t // tm, nj),
        in_specs=[pl.BlockSpec((tm, d), lambda i, j: (i, 0)),
                  pl.BlockSpec((d, tn), lambda i, j: (0, j)),
                  pl.BlockSpec((d, tn), lambda i, j: (0, j + nj))],
        out_specs=pl.BlockSpec((tm, tn), lambda i, j: (i, j)),
        out_shape=jax.ShapeDtypeStruct((t, f), BF16),
        compiler_params=_params("parallel", "arbitrary"),
        name="ffn_up",
    )(h, w_up, w_up)


def _mm_norm_res_kernel(a_ref, w_ref, x_ref, pg_ref, ng_ref, xo_ref, *rest, res_scale, nkk, nj, rows, emit_h):
    kk = pl.program_id(1)
    j = pl.program_id(2)
    tm, tn = a_ref.shape[0], w_ref.shape[1]
    cols = pl.ds(pl.multiple_of(j * tn, tn), tn)

    @pl.when(kk == 0)
    def _():
        xo_ref[:, cols] = jnp.dot(a_ref[...], w_ref[...], preferred_element_type=F32)

    @pl.when(kk > 0)
    def _():
        xo_ref[:, cols] += jnp.dot(a_ref[...], w_ref[...], preferred_element_type=F32)

    @pl.when((kk == nkk - 1) & (j == nj - 1))
    def _():
        def chunk(c, carry):
            r = pl.ds(pl.multiple_of(c * rows, rows), rows)
            y = xo_ref[r, :]
            xn = x_ref[r, :] + res_scale * (y * _rms_scale(y) * pg_ref[...])
            xo_ref[r, :] = xn
            if emit_h:
                rest[0][r, :] = (xn * _rms_scale(xn) * ng_ref[...]).astype(BF16)
            return carry

        lax.fori_loop(0, tm // rows, chunk, 0)


def mm_norm_res(a, w, x, post_g, next_g, *, res_scale, emit_h=True, tm=512, tn=256, tkk=4096, rows=64):
    t, kdim = a.shape
    d = w.shape[1]
    nkk, nj = kdim // tkk, d // tn
    row_block = lambda i, kk, j: (i, 0)
    out_shape = [jax.ShapeDtypeStruct((t, d), F32)]
    out_specs = [pl.BlockSpec((tm, d), row_block)]
    if emit_h:
        out_shape.append(jax.ShapeDtypeStruct((t, d), BF16))
        out_specs.append(pl.BlockSpec((tm, d), row_block))
    res = pl.pallas_call(
        functools.partial(_mm_norm_res_kernel, res_scale=res_scale, nkk=nkk, nj=nj, rows=rows, emit_h=emit_h),
        grid=(t // tm, nkk, nj),
        in_specs=[pl.BlockSpec((tm, tkk), lambda i, kk, j: (i, kk)),
                  pl.BlockSpec((tkk, tn), lambda i, kk, j: (kk, j)),
                  pl.BlockSpec((tm, d), row_block, pipeline_mode=pl.Buffered(1)),
                  pl.BlockSpec((1, d), lambda i, kk, j: (0, 0)),
                  pl.BlockSpec((1, d), lambda i, kk, j: (0, 0))],
        out_specs=out_specs,
        out_shape=out_shape,
        compiler_params=_params("parallel", "arbitrary", "arbitrary"),
        name="mm_norm_res",
    )(a, w, x, post_g.reshape(1, d), next_g.reshape(1, d))
    return (res[0], res[1]) if emit_h else (res[0], None)


def _matmul_kernel(a_ref, w_ref, o_ref):
    o_ref[...] = jnp.dot(a_ref[...], w_ref[...], preferred_element_type=F32).astype(o_ref.dtype)


def matmul(a, w, *, tm=512, tn=1280):
    t, kdim = a.shape
    n = w.shape[1]
    return pl.pallas_call(
        _matmul_kernel,
        grid=(t // tm, n // tn),
        in_specs=[pl.BlockSpec((tm, kdim), lambda i, j: (i, 0)),
                  pl.BlockSpec((kdim, tn), lambda i, j: (0, j))],
        out_specs=pl.BlockSpec((tm, tn), lambda i, j: (i, j)),
        out_shape=jax.ShapeDtypeStruct((t, n), BF16),
        compiler_params=_params("parallel", "arbitrary"),
        name="in_proj",
    )(a, w)


def _gelu_exact(x):
    return 0.5 * x * (1.0 + lax.erf(x * (1.0 / math.sqrt(2.0))))


def _gmlp_kernel(z_ref, lng_ref, lnb_ref, ws_ref, bs_ref, o_ref, *, chunks):
    bw = BRANCH_WIDTH
    row = lax.broadcasted_iota(jnp.int32, (GMLP_CHUNK, GMLP_CHUNK), 0)
    col = lax.broadcasted_iota(jnp.int32, (GMLP_CHUNK, GMLP_CHUNK), 1)
    causal = row >= col
    for c in range(chunks):
        r = slice(c * GMLP_CHUNK, (c + 1) * GMLP_CHUNK)
        u = _gelu_exact(z_ref[r, :bw].astype(F32))
        v = _gelu_exact(z_ref[r, bw:].astype(F32))
        mu = jnp.mean(v, axis=-1, keepdims=True)
        vc = v - mu
        var = jnp.mean(vc * vc, axis=-1, keepdims=True)
        vn = (vc * lax.rsqrt(var + LN_EPS) * lng_ref[...] + lnb_ref[...]).astype(BF16)
        for g in range(GMLP_GROUPS):
            gs = slice(g * GMLP_GROUP_DIM, (g + 1) * GMLP_GROUP_DIM)
            w = jnp.where(causal, ws_ref[g], 0.0).astype(BF16)
            mixed = jnp.dot(w, vn[:, gs], preferred_element_type=F32) + bs_ref[:, g:g + 1]
            o_ref[r, gs] = (u[:, gs] * mixed).astype(o_ref.dtype)


def gmlp(proj, ln_g, ln_b, w_s, b_s, *, chunks=4):
    t = proj.shape[0]
    bw = BRANCH_WIDTH
    tm = chunks * GMLP_CHUNK
    return pl.pallas_call(
        functools.partial(_gmlp_kernel, chunks=chunks),
        grid=(t // tm,),
        in_specs=[pl.BlockSpec((tm, 2 * bw), lambda i: (i, 0)),
                  pl.BlockSpec((1, bw), lambda i: (0, 0)),
                  pl.BlockSpec((1, bw), lambda i: (0, 0)),
                  pl.BlockSpec((GMLP_GROUPS, GMLP_CHUNK, GMLP_CHUNK), lambda i: (0, 0, 0)),
                  pl.BlockSpec((GMLP_CHUNK, GMLP_GROUPS), lambda i: (0, 0))],
        out_specs=pl.BlockSpec((tm, bw), lambda i: (i, 0)),
        out_shape=jax.ShapeDtypeStruct((t, bw), BF16),
        compiler_params=_params("parallel"),
        name="gmlp",
    )(proj, ln_g.reshape(1, bw), ln_b.reshape(1, bw), w_s, b_s.T)


MOBA_MAX_BLOCKS = 32
MOBA_CHUNK_BLOCKS = 4
FEAT_SEL, FEAT_ONE, FEAT_BLK, FEAT_OFF, FEAT_END = 0, 32, 35, 38, 41


def _moba_key_features(seq):
    pos = jnp.arange(seq, dtype=jnp.int32)[:, None]
    kb, off = pos // MOBA_BLOCK, pos % MOBA_BLOCK
    lane = jnp.arange(LANES, dtype=jnp.int32)[None, :]
    feat = jnp.where(lane < FEAT_ONE, (lane == kb).astype(F32),
                     jnp.where(lane < FEAT_BLK, 1.0,
                               jnp.where(lane < FEAT_OFF, (kb * MOBA_BLOCK).astype(F32),
                                         jnp.where(lane < FEAT_END, off.astype(F32), 0.0))))
    return feat.astype(BF16)


def _split3(x):
    hi = x.astype(BF16).astype(F32)
    mid = (x - hi).astype(BF16).astype(F32)
    return hi, mid, x - hi - mid


def _moba_kernel(slopes_ref, q_ref, k_ref, v_ref, kfeat_ref, o_ref, kaug_ref, kmh_ref, kml_ref, s_ref, *, nb):
    blk = MOBA_BLOCK
    dh = MOBA_HEAD_DIM
    h = pl.program_id(1)
    qi = pl.program_id(2)
    slope = slopes_ref[h]
    scale = dh ** -0.5

    @pl.when(qi == 0)
    def _():
        kmh_ref[...] = jnp.zeros_like(kmh_ref)
        kml_ref[...] = jnp.zeros_like(kml_ref)
        for n in range(nb):
            rows = slice(n * blk, (n + 1) * blk)
            kn = k_ref[rows, :]
            kaug_ref[rows, :dh] = kn
            kaug_ref[rows, dh:] = kfeat_ref[rows, :]
            km = jnp.mean(kn.astype(F32), axis=0, keepdims=True)
            hi = km.astype(BF16)
            kmh_ref[n:n + 1, :] = hi
            kml_ref[n:n + 1, :] = (km - hi.astype(F32)).astype(BF16)

    q = q_ref[...]
    gate = _dot_nt(q, kmh_ref[...]) + _dot_nt(q, kml_ref[...])
    lane = lax.broadcasted_iota(jnp.int32, (blk, LANES), 1)
    gate = jnp.where(lane < qi, gate, -jnp.inf)
    sel_bias = jnp.where(lane == qi, 0.0, MASKED)
    for _ in range(MOBA_TOPK):
        best = jnp.max(gate, axis=-1, keepdims=True)
        first = jnp.min(jnp.where(gate == best, lane, LANES), axis=-1, keepdims=True)
        pick = (lane == first) & (best > -jnp.inf)
        sel_bias = jnp.where(pick, 0.0, sel_bias)
        gate = jnp.where(pick, -jnp.inf, gate)

    row = lax.broadcasted_iota(jnp.int32, (blk, LANES), 0)
    t_pos = (qi * blk + row).astype(F32)
    neg_st = _split3(-slope * t_pos)
    sl = _split3(jnp.full((blk, LANES), slope, F32))
    part = jnp.where(lane < FEAT_BLK, lane - FEAT_ONE, jnp.where(lane < FEAT_OFF, lane - FEAT_BLK, lane - FEAT_OFF))
    pick3 = lambda parts: jnp.where(part == 0, parts[0], jnp.where(part == 1, parts[1], parts[2]))
    feat = jnp.where(lane < FEAT_ONE, sel_bias,
                     jnp.where(lane < FEAT_BLK, pick3(neg_st), jnp.where(lane < FEAT_END, pick3(sl), 0.0)))
    q_aug = jnp.concatenate([(q.astype(F32) * scale).astype(BF16), feat.astype(BF16)], axis=1)

    cb = MOBA_CHUNK_BLOCKS
    span = cb * blk

    def chunk_rows(p):
        return pl.ds(pl.multiple_of(p * span, span), span)

    def scores(p):
        return _dot_nt(q_aug, kaug_ref[chunk_rows(p), :])

    def lanes_max(m, s):
        for c in range(0, s.shape[1], LANES):
            m = jnp.maximum(m, s[:, c:c + LANES])
        return m

    def pass1(p, mvec):
        s = scores(p)
        s_ref[p] = s
        return lanes_max(mvec, s)

    n_past = qi // cb
    mvec = lax.fori_loop(0, n_past, pass1, jnp.full((blk, LANES), -jnp.inf, F32))
    r_idx = lax.broadcasted_iota(jnp.int32, (blk, blk), 0)
    c_idx = lax.broadcasted_iota(jnp.int32, (blk, blk), 1)
    causal_bias = jnp.where(r_idx >= c_idx, 0.0, MASKED)
    own = qi % cb
    s = scores(n_past) + jnp.concatenate(
        [causal_bias * (own == c).astype(F32) for c in range(cb)], axis=1)
    s_ref[n_past] = s
    m = jnp.max(lanes_max(mvec, s), axis=-1, keepdims=True)

    def pass2(p, carry):
        l, acc = carry
        e = jnp.exp(s_ref[p] - m)
        l = l + jnp.sum(e, axis=-1, keepdims=True)
        acc = acc + jnp.dot(e.astype(BF16), v_ref[chunk_rows(p), :], preferred_element_type=F32)
        return l, acc

    l, acc = lax.fori_loop(0, n_past + 1, pass2,
                           (jnp.zeros((blk, 1), F32), jnp.zeros((blk, dh), F32)))
    o_ref[...] = (acc / l).astype(o_ref.dtype)


def moba(proj, slopes, *, batch, seq):
    blk = MOBA_BLOCK
    nb = seq // blk
    assert nb % MOBA_CHUNK_BLOCKS == 0 and nb <= MOBA_MAX_BLOCKS
    dh = MOBA_HEAD_DIM
    cq, ck, cv = COL_MOBA_Q // dh, COL_MOBA_K // dh, COL_MOBA_V // dh
    return pl.pallas_call(
        functools.partial(_moba_kernel, nb=nb),
        grid_spec=pltpu.PrefetchScalarGridSpec(
            num_scalar_prefetch=1,
            grid=(batch, MOBA_HEADS, nb),
            in_specs=[pl.BlockSpec((blk, dh), lambda b, h, i, s: (b * nb + i, cq + h)),
                      pl.BlockSpec((seq, dh), lambda b, h, i, s: (b, ck + h)),
                      pl.BlockSpec((seq, dh), lambda b, h, i, s: (b, cv + h)),
                      pl.BlockSpec((seq, LANES), lambda b, h, i, s: (0, 0))],
            out_specs=pl.BlockSpec((blk, dh), lambda b, h, i, s: (b * nb + i, h)),
            scratch_shapes=[pltpu.VMEM((seq, dh + LANES), BF16),
                            pltpu.VMEM((LANES, dh), BF16), pltpu.VMEM((LANES, dh), BF16),
                            pltpu.VMEM((nb // MOBA_CHUNK_BLOCKS, blk, MOBA_CHUNK_BLOCKS * blk), F32)],
        ),
        out_shape=jax.ShapeDtypeStruct((batch * seq, BRANCH_WIDTH), BF16),
        compiler_params=_params("parallel", "parallel", "arbitrary"),
        name="moba",
    )(slopes, proj, proj, proj, _moba_key_features(seq))


def _swa_kernel(slopes_ref, sinks_ref, q_ref, kp_ref, kc_ref, vp_ref, vc_ref, o_ref):
    w = SWA_WINDOW
    dh = SWA_HEAD_DIM
    n = pl.program_id(1)
    scale = dh ** -0.5
    lane = lax.broadcasted_iota(jnp.int32, (w, LANES), 1)
    low = lane < dh
    t_idx = lax.broadcasted_iota(jnp.int32, (w, w), 0)
    s_idx = lax.broadcasted_iota(jnp.int32, (w, w), 1)
    dist_cur = (t_idx - s_idx).astype(F32)
    dist_prev = dist_cur + float(w)
    ok_cur = s_idx <= t_idx
    ok_prev = (s_idx > t_idx) & (n > 0)

    def halves(ref):
        x = ref[...].astype(F32)
        xr = pltpu.roll(x, dh, axis=1)
        z = jnp.zeros_like(x)
        lo = (jnp.where(low, x, z).astype(BF16), jnp.where(low, z, xr).astype(BF16))
        hi = (jnp.where(low, xr, z).astype(BF16), jnp.where(low, z, x).astype(BF16))
        return lo, hi

    kp, kc, vp, vc = halves(kp_ref), halves(kc_ref), halves(vp_ref), halves(vc_ref)
    pairs = SWA_Q_HEADS // 2
    group = SWA_Q_HEADS // SWA_KV_HEADS
    for pr in range(pairs):
        q = q_ref[:, pr * LANES:(pr + 1) * LANES]
        kh = (2 * pr) // group
        out = None
        for half in range(2):
            hq = 2 * pr + half
            slope = slopes_ref[hq]
            sink = sinks_ref[hq]
            s_p = _dot_nt(q, kp[kh][half]) * scale
            s_c = _dot_nt(q, kc[kh][half]) * scale
            s_p = jnp.where(ok_prev, s_p - slope * dist_prev, MASKED)
            s_c = jnp.where(ok_cur, s_c - slope * dist_cur, MASKED)
            m = jnp.maximum(jnp.max(s_p, axis=-1, keepdims=True), jnp.max(s_c, axis=-1, keepdims=True))
            m = jnp.maximum(m, sink)
            e_p = jnp.exp(s_p - m)
            e_c = jnp.exp(s_c - m)
            denom = (jnp.sum(e_p, axis=-1, keepdims=True) + jnp.sum(e_c, axis=-1, keepdims=True)
                     + jnp.exp(sink - m))
            inv = 1.0 / denom
            o = (jnp.dot((e_p * inv).astype(BF16), vp[kh][half], preferred_element_type=F32)
                 + jnp.dot((e_c * inv).astype(BF16), vc[kh][half], preferred_element_type=F32))
            out = o if out is None else out + o
        o_ref[:, pr * LANES:(pr + 1) * LANES] = out.astype(o_ref.dtype)


def swa(proj, slopes, sinks, *, batch, seq):
    w = SWA_WINDOW
    nblk = seq // w
    bw = BRANCH_WIDTH
    cq = COL_SWA_Q // bw
    ck = COL_SWA_K // SWA_KV_WIDTH
    cv = COL_SWA_V // SWA_KV_WIDTH
    prev = lambda b, n, *_: (b * nblk + jnp.maximum(n - 1, 0))
    cur = lambda b, n, *_: (b * nblk + n)
    return pl.pallas_call(
        _swa_kernel,
        grid_spec=pltpu.PrefetchScalarGridSpec(
            num_scalar_prefetch=2,
            grid=(batch, nblk),
            in_specs=[pl.BlockSpec((w, bw), lambda b, n, *_: (cur(b, n), cq)),
                      pl.BlockSpec((w, SWA_KV_WIDTH), lambda b, n, *_: (prev(b, n), ck)),
                      pl.BlockSpec((w, SWA_KV_WIDTH), lambda b, n, *_: (cur(b, n), ck)),
                      pl.BlockSpec((w, SWA_KV_WIDTH), lambda b, n, *_: (prev(b, n), cv)),
                      pl.BlockSpec((w, SWA_KV_WIDTH), lambda b, n, *_: (cur(b, n), cv))],
            out_specs=pl.BlockSpec((w, bw), lambda b, n, *_: (cur(b, n), 0)),
        ),
        out_shape=jax.ShapeDtypeStruct((batch * seq, bw), BF16),
        compiler_params=_params("parallel", "arbitrary"),
        name="swa",
    )(slopes, sinks, proj, proj, proj, proj, proj)


def _merge_kernel(h_ref, wg_ref0, wg_ref1, wg_ref2, bg_ref, ya_ref, yb_ref, yc_ref, wb_ref, o_ref):
    h = h_ref[...]
    merged = None
    for n, (wg_ref, y_ref) in enumerate(((wg_ref0, ya_ref), (wg_ref1, yb_ref), (wg_ref2, yc_ref))):
        logits = jnp.dot(h, wg_ref[...], preferred_element_type=F32) + bg_ref[n]
        branch = jnp.dot(y_ref[...], wb_ref[n], preferred_element_type=F32)
        term = jax.nn.sigmoid(logits) * branch
        merged = term if merged is None else merged + term
    o_ref[...] = merged.astype(o_ref.dtype)


def merge(h, w_gate, b_gate, y_a, y_b, y_c, w_branch, *, tm=512, tn=256):
    t, d = h.shape
    bw = BRANCH_WIDTH
    nj = d // tn
    gate_spec = lambda n: pl.BlockSpec((d, tn), lambda i, j: (0, n * nj + j))
    y_spec = pl.BlockSpec((tm, bw), lambda i, j: (i, 0))
    return pl.pallas_call(
        _merge_kernel,
        grid=(t // tm, nj),
        in_specs=[pl.BlockSpec((tm, d), lambda i, j: (i, 0)),
                  gate_spec(0), gate_spec(1), gate_spec(2),
                  pl.BlockSpec((N_BRANCHES, 1, tn), lambda i, j: (0, 0, j)),
                  y_spec, y_spec, y_spec,
                  pl.BlockSpec((N_BRANCHES, bw, tn), lambda i, j: (0, 0, j))],
        out_specs=pl.BlockSpec((tm, tn), lambda i, j: (i, j)),
        out_shape=jax.ShapeDtypeStruct((t, d), BF16),
        compiler_params=_params("parallel", "arbitrary"),
        name="merge",
    )(h, w_gate, w_gate, w_gate, b_gate.reshape(N_BRANCHES, 1, d), y_a, y_b, y_c, w_branch)


def _alibi_slopes():
    i = jnp.arange(1, N_ALIBI_HEADS + 1, dtype=F32)
    s = jnp.exp2(-8.0 * i / N_ALIBI_HEADS)
    return s[:SWA_Q_HEADS], s[SWA_Q_HEADS:]


def kernel(x, ffn1_pre_g, ffn1_w_up, ffn1_w_down, ffn1_post_g, mix_pre_g, w_in, gmlp_ln_g, gmlp_ln_b,
           gmlp_w_s, gmlp_b_s, swa_sinks, w_gate, b_gate, w_branch, w_out, mix_post_g, ffn2_pre_g,
           ffn2_w_up, ffn2_w_down, ffn2_post_g):
    batch, seq, d = x.shape
    depth = ffn1_pre_g.shape[0]
    swa_slopes, moba_slopes = _alibi_slopes()
    xf = x.reshape(batch * seq, d)
    h = prenorm(xf, ffn1_pre_g[0])
    for i in range(depth):
        act = ffn_up(h, ffn1_w_up[i].astype(BF16))
        xf, h = mm_norm_res(act, ffn1_w_down[i].astype(BF16), xf, ffn1_post_g[i], mix_pre_g[i], res_scale=0.5)

        proj = matmul(h, w_in[i].astype(BF16))
        y_a = gmlp(proj, gmlp_ln_g[i], gmlp_ln_b[i], gmlp_w_s[i], gmlp_b_s[i])
        y_b = moba(proj, moba_slopes, batch=batch, seq=seq)
        y_c = swa(proj, swa_slopes, swa_sinks[i], batch=batch, seq=seq)
        merged = merge(h, w_gate[i].astype(BF16), b_gate[i], y_a, y_b, y_c, w_branch[i].astype(BF16))
        xf, h = mm_norm_res(merged, w_out[i].astype(BF16), xf, mix_post_g[i], ffn2_pre_g[i], res_scale=1.0)

        act = ffn_up(h, ffn2_w_up[i].astype(BF16))
        last = i == depth - 1
        next_g = ffn2_post_g[i] if last else ffn1_pre_g[i + 1]
        xf, h = mm_norm_res(act, ffn2_w_down[i].astype(BF16), xf, ffn2_post_g[i], next_g,
                            res_scale=0.5, emit_h=not last)
    return xf.reshape(batch, seq, d)
```

```python
import functools
import math

import jax
import jax.numpy as jnp
from jax import lax
from jax.experimental import pallas as pl
from jax.experimental.pallas import tpu as pltpu

D_MODEL = 4096
DEPTH = 2
BRANCH_WIDTH = D_MODEL // 4
N_BRANCHES = 3
GMLP_CHUNK = 128
GMLP_GROUP_DIM = 128
GMLP_GROUPS = BRANCH_WIDTH // GMLP_GROUP_DIM
MOBA_HEAD_DIM = 128
MOBA_HEADS = BRANCH_WIDTH // MOBA_HEAD_DIM
MOBA_BLOCK = 256
MOBA_TOPK = 3
SWA_HEAD_DIM = 64
SWA_Q_HEADS = BRANCH_WIDTH // SWA_HEAD_DIM
SWA_KV_HEADS = SWA_Q_HEADS // 8
SWA_WINDOW = 128
SWA_KV_WIDTH = SWA_KV_HEADS * SWA_HEAD_DIM
IN_COLS = 6 * BRANCH_WIDTH + 2 * SWA_KV_WIDTH
D_FF = 2 * D_MODEL
N_ALIBI_HEADS = SWA_Q_HEADS + MOBA_HEADS
RMS_EPS = 1e-6
LN_EPS = 1e-5

LANES = 128
VMEM_LIMIT = 56 * 1024 * 1024
MASKED = -1e30

F32 = jnp.float32
BF16 = jnp.bfloat16

COL_GMLP = 0
COL_MOBA_Q = 2 * BRANCH_WIDTH
COL_MOBA_K = 3 * BRANCH_WIDTH
COL_MOBA_V = 4 * BRANCH_WIDTH
COL_SWA_Q = 5 * BRANCH_WIDTH
COL_SWA_K = 6 * BRANCH_WIDTH
COL_SWA_V = 6 * BRANCH_WIDTH + SWA_KV_WIDTH


def _params(*semantics):
    return pltpu.CompilerParams(dimension_semantics=semantics, vmem_limit_bytes=VMEM_LIMIT)


def _rms_scale(x):
    return lax.rsqrt(jnp.mean(x * x, axis=-1, keepdims=True) + RMS_EPS)


def _dot_nt(a, b):
    return lax.dot_general(a, b, (((1,), (1,)), ((), ())), preferred_element_type=F32)


def _prenorm_kernel(x_ref, g_ref, h_ref):
    x = x_ref[...]
    h_ref[...] = (x * _rms_scale(x) * g_ref[...]).astype(h_ref.dtype)


def prenorm(x, g, *, tm=256):
    t, d = x.shape
    return pl.pallas_call(
        _prenorm_kernel,
        grid=(t // tm,),
        in_specs=[pl.BlockSpec((tm, d), lambda i: (i, 0)),
                  pl.BlockSpec((1, d), lambda i: (0, 0))],
        out_specs=pl.BlockSpec((tm, d), lambda i: (i, 0)),
        out_shape=jax.ShapeDtypeStruct((t, d), BF16),
        compiler_params=_params("parallel"),
        name="prenorm",
    )(x, g.reshape(1, d))


def _ffn_up_kernel(h_ref, wg_ref, wu_ref, o_ref):
    h = h_ref[...]
    g = jnp.dot(h, wg_ref[...], preferred_element_type=F32)
    u = jnp.dot(h, wu_ref[...], preferred_element_type=F32)
    o_ref[...] = (g * jax.nn.sigmoid(g) * u).astype(o_ref.dtype)


def ffn_up(h, w_up, *, tm=512, tn=512):
    t, d = h.shape
    f = w_up.shape[1] // 2
    nj = f // tn
    return pl.pallas_call(
        _ffn_up_kernel,
        grid=(t // tm, nj),
        in_specs=[pl.BlockSpec((tm, d), lambda i, j: (i, 0)),
                  pl.BlockSpec((d, tn), lambda i, j: (0, j)),
                  pl.BlockSpec((d, tn), lambda i, j: (0, j + nj))],
        out_specs=pl.BlockSpec((tm, tn), lambda i, j: (i, j)),
        out_shape=jax.ShapeDtypeStruct((t, f), BF16),
        compiler_params=_params("parallel", "arbitrary"),
        name="ffn_up",
    )(h, w_up, w_up)


def _mm_norm_res_kernel(*refs, res_scale, nkk, nj, rows, emit_h):
    if emit_h:
        a_ref, w_ref, x_ref, pg_ref, ng_ref, xo_ref, ho_ref = refs
    else:
        a_ref, w_ref, x_ref, pg_ref, xo_ref = refs
    kk = pl.program_id(1)
    j = pl.program_id(2)
    tm, tn = a_ref.shape[0], w_ref.shape[1]
    cols = pl.ds(pl.multiple_of(j * tn, tn), tn)

    @pl.when(kk == 0)
    def _():
        xo_ref[:, cols] = jnp.dot(a_ref[...], w_ref[...], preferred_element_type=F32)

    @pl.when(kk > 0)
    def _():
        xo_ref[:, cols] += jnp.dot(a_ref[...], w_ref[...], preferred_element_type=F32)

    @pl.when((kk == nkk - 1) & (j == nj - 1))
    def _():
        def chunk(c, carry):
            r = pl.ds(pl.multiple_of(c * rows, rows), rows)
            y = xo_ref[r, :]
            xn = x_ref[r, :] + res_scale * (y * _rms_scale(y) * pg_ref[...])
            xo_ref[r, :] = xn
            if emit_h:
                ho_ref[r, :] = (xn * _rms_scale(xn) * ng_ref[...]).astype(BF16)
            return carry

        lax.fori_loop(0, tm // rows, chunk, 0)


def _tiled_bf16(w, tk, tn):
    kdim, n = w.shape
    return w.astype(BF16).reshape(kdim // tk, tk, n // tn, tn).transpose(0, 2, 1, 3)


def mm_norm_res(a, w, x, post_g, next_g=None, *, res_scale, tm=512, tn=512, tkk=2048, rows=64):
    t, kdim = a.shape
    d = w.shape[1]
    nkk, nj = kdim // tkk, d // tn
    emit_h = next_g is not None
    row_block = lambda i, kk, j: (i, 0)
    gain_spec = pl.BlockSpec((1, d), lambda i, kk, j: (0, 0))
    operands = [a, _tiled_bf16(w, tkk, tn), x, post_g.reshape(1, d)]
    in_specs = [pl.BlockSpec((tm, tkk), lambda i, kk, j: (i, kk)),
                pl.BlockSpec((None, None, tkk, tn), lambda i, kk, j: (kk, j, 0, 0)),
                pl.BlockSpec((tm, d), row_block),
                gain_spec]
    out_shape = [jax.ShapeDtypeStruct((t, d), F32)]
    out_specs = [pl.BlockSpec((tm, d), row_block)]
    if emit_h:
        operands.append(next_g.reshape(1, d))
        in_specs.append(gain_spec)
        out_shape.append(jax.ShapeDtypeStruct((t, d), BF16))
        out_specs.append(pl.BlockSpec((tm, d), row_block))
    res = pl.pallas_call(
        functools.partial(_mm_norm_res_kernel, res_scale=res_scale, nkk=nkk, nj=nj, rows=rows, emit_h=emit_h),
        grid=(t // tm, nkk, nj),
        in_specs=in_specs,
        out_specs=out_specs,
        out_shape=out_shape,
        compiler_params=_params("parallel", "arbitrary", "arbitrary"),
        name="mm_norm_res",
    )(*operands)
    return (res[0], res[1]) if emit_h else (res[0], None)


def _matmul_kernel(a_ref, w_ref, o_ref):
    o_ref[...] = jnp.dot(a_ref[...], w_ref[...], preferred_element_type=F32).astype(o_ref.dtype)


def matmul(a, w, *, tm=512, tn=1280):
    t, kdim = a.shape
    n = w.shape[1]
    return pl.pallas_call(
        _matmul_kernel,
        grid=(t // tm, n // tn),
        in_specs=[pl.BlockSpec((tm, kdim), lambda i, j: (i, 0)),
                  pl.BlockSpec((kdim, tn), lambda i, j: (0, j))],
        out_specs=pl.BlockSpec((tm, tn), lambda i, j: (i, j)),
        out_shape=jax.ShapeDtypeStruct((t, n), BF16),
        compiler_params=_params("parallel", "arbitrary"),
        name="in_proj",
    )(a, w)


def _gelu_exact(x):
    return 0.5 * x * (1.0 + lax.erf(x * (1.0 / math.sqrt(2.0))))


def _gmlp_kernel(z_ref, lng_ref, lnb_ref, ws_ref, bs_ref, o_ref, *, chunks):
    bw = BRANCH_WIDTH
    row = lax.broadcasted_iota(jnp.int32, (GMLP_CHUNK, GMLP_CHUNK), 0)
    col = lax.broadcasted_iota(jnp.int32, (GMLP_CHUNK, GMLP_CHUNK), 1)
    causal = row >= col
    for c in range(chunks):
        r = slice(c * GMLP_CHUNK, (c + 1) * GMLP_CHUNK)
        u = _gelu_exact(z_ref[r, :bw].astype(F32))
        v = _gelu_exact(z_ref[r, bw:].astype(F32))
        mu = jnp.mean(v, axis=-1, keepdims=True)
        vc = v - mu
        var = jnp.mean(vc * vc, axis=-1, keepdims=True)
        vn = (vc * lax.rsqrt(var + LN_EPS) * lng_ref[...] + lnb_ref[...]).astype(BF16)
        for g in range(GMLP_GROUPS):
            gs = slice(g * GMLP_GROUP_DIM, (g + 1) * GMLP_GROUP_DIM)
            w = jnp.where(causal, ws_ref[g], 0.0).astype(BF16)
            mixed = jnp.dot(w, vn[:, gs], preferred_element_type=F32) + bs_ref[:, g:g + 1]
            o_ref[r, gs] = (u[:, gs] * mixed).astype(o_ref.dtype)


def gmlp(proj, ln_g, ln_b, w_s, b_s, *, chunks=4):
    t = proj.shape[0]
    bw = BRANCH_WIDTH
    tm = chunks * GMLP_CHUNK
    return pl.pallas_call(
        functools.partial(_gmlp_kernel, chunks=chunks),
        grid=(t // tm,),
        in_specs=[pl.BlockSpec((tm, 2 * bw), lambda i: (i, 0)),
                  pl.BlockSpec((1, bw), lambda i: (0, 0)),
                  pl.BlockSpec((1, bw), lambda i: (0, 0)),
                  pl.BlockSpec((GMLP_GROUPS, GMLP_CHUNK, GMLP_CHUNK), lambda i: (0, 0, 0)),
                  pl.BlockSpec((GMLP_CHUNK, GMLP_GROUPS), lambda i: (0, 0))],
        out_specs=pl.BlockSpec((tm, bw), lambda i: (i, 0)),
        out_shape=jax.ShapeDtypeStruct((t, bw), BF16),
        compiler_params=_params("parallel"),
        name="gmlp",
    )(proj, ln_g.reshape(1, bw), ln_b.reshape(1, bw), w_s, b_s.T)


MOBA_MAX_BLOCKS = 32
MOBA_CHUNK_BLOCKS = 4
MOBA_PREP_ROWS = 512
LOG2_E = math.log2(math.e)
FEAT_SEL, FEAT_ONE, FEAT_BLK, FEAT_OFF, FEAT_END = 0, 32, 35, 38, 41


def _moba_key_features(seq):
    pos = jnp.arange(seq, dtype=jnp.int32)[:, None]
    kb, off = pos // MOBA_BLOCK, pos % MOBA_BLOCK
    lane = jnp.arange(LANES, dtype=jnp.int32)[None, :]
    feat = jnp.where(lane < FEAT_ONE, (lane == kb).astype(F32),
                     jnp.where(lane < FEAT_BLK, 1.0,
                               jnp.where(lane < FEAT_OFF, (kb * MOBA_BLOCK).astype(F32),
                                         jnp.where(lane < FEAT_END, off.astype(F32), 0.0))))
    return feat.astype(BF16)


def _split3(x):
    hi = x.astype(BF16).astype(F32)
    mid = (x - hi).astype(BF16).astype(F32)
    return hi, mid, x - hi - mid


def _moba_prepare(slope, q_ref, k_ref, kfeat_ref, kaug_ref, qaug_ref, kmh_ref, kml_ref, *, nb):
    blk = MOBA_BLOCK
    dh = MOBA_HEAD_DIM
    seq = q_ref.shape[0]
    prep = MOBA_PREP_ROWS
    nbp = MOBA_MAX_BLOCKS
    scale = dh ** -0.5 * LOG2_E
    slope = slope * LOG2_E

    kmh_ref[...] = jnp.zeros_like(kmh_ref)
    kml_ref[...] = jnp.zeros_like(kml_ref)
    for n in range(nb):
        rows = slice(n * blk, (n + 1) * blk)
        kn = k_ref[rows, :]
        kaug_ref[rows, :dh] = kn
        kaug_ref[rows, dh:] = kfeat_ref[rows, :]
        km = jnp.mean(kn.astype(F32), axis=0, keepdims=True)
        hi = km.astype(BF16)
        kmh_ref[n:n + 1, :] = hi
        kml_ref[n:n + 1, :] = (km - hi.astype(F32)).astype(BF16)

    blk_id = lax.broadcasted_iota(jnp.int32, (nbp, prep), 0)
    q_off = lax.broadcasted_iota(jnp.int32, (nbp, prep), 1)
    lane = lax.broadcasted_iota(jnp.int32, (prep, LANES), 1)
    row = lax.broadcasted_iota(jnp.int32, (prep, LANES), 0)
    part = jnp.where(lane < FEAT_BLK, lane - FEAT_ONE, jnp.where(lane < FEAT_OFF, lane - FEAT_BLK, lane - FEAT_OFF))
    pick3 = lambda parts: jnp.where(part == 0, parts[0], jnp.where(part == 1, parts[1], parts[2]))
    slope_feat = pick3(_split3(jnp.full((prep, LANES), slope, F32)))

    def rows_chunk(c, carry):
        row0 = pl.multiple_of(c * prep, prep)
        rows = pl.ds(row0, prep)
        q = q_ref[rows, :]
        gate = _dot_nt(kmh_ref[:nbp, :], q) + _dot_nt(kml_ref[:nbp, :], q)
        q_blk = (row0 + q_off) // blk
        gate = jnp.where(blk_id < q_blk, gate, -jnp.inf)
        sel_bias = jnp.where(blk_id == q_blk, 0.0, MASKED)
        for _ in range(MOBA_TOPK):
            best = jnp.max(gate, axis=0, keepdims=True)
            first = jnp.min(jnp.where(gate == best, blk_id, nbp), axis=0, keepdims=True)
            pick = (blk_id == first) & (best > -jnp.inf)
            sel_bias = jnp.where(pick, 0.0, sel_bias)
            gate = jnp.where(pick, -jnp.inf, gate)
        sel_rows = jnp.concatenate([sel_bias, jnp.zeros((LANES - nbp, prep), F32)], axis=0).T

        t_pos = (row0 + row).astype(F32)
        feat = jnp.where(lane < FEAT_ONE, sel_rows,
                         jnp.where(lane < FEAT_BLK, pick3(_split3(-slope * t_pos)),
                                   jnp.where(lane < FEAT_END, slope_feat, 0.0)))
        qaug_ref[rows, :dh] = (q.astype(F32) * scale).astype(BF16)
        qaug_ref[rows, dh:] = feat.astype(BF16)
        return carry

    lax.fori_loop(0, seq // prep, rows_chunk, 0)


def _moba_kernel(slopes_ref, q_ref, k_ref, v_ref, kfeat_ref, o_ref, kaug_ref, qaug_ref, kmh_ref, kml_ref, s_ref,
                 *, nb):
    blk = MOBA_BLOCK
    dh = MOBA_HEAD_DIM
    h = pl.program_id(1)
    qi = pl.program_id(2)

    @pl.when(qi == 0)
    def _():
        _moba_prepare(slopes_ref[h], q_ref, k_ref, kfeat_ref, kaug_ref, qaug_ref, kmh_ref, kml_ref, nb=nb)

    q_aug = qaug_ref[pl.ds(pl.multiple_of(qi * blk, blk), blk), :]

    cb = MOBA_CHUNK_BLOCKS
    span = cb * blk

    def chunk_rows(p):
        return pl.ds(pl.multiple_of(p * span, span), span)

    def scores(p):
        return _dot_nt(q_aug, kaug_ref[chunk_rows(p), :])

    def lanes_max(m, s):
        for c in range(0, s.shape[1], LANES):
            m = jnp.maximum(m, s[:, c:c + LANES])
        return m

    def pass1(p, mvec):
        s = scores(p)
        s_ref[p] = s
        return lanes_max(mvec, s)

    n_past = qi // cb
    mvec = lax.fori_loop(0, n_past, pass1, jnp.full((blk, LANES), -jnp.inf, F32))
    r_idx = lax.broadcasted_iota(jnp.int32, (blk, blk), 0)
    c_idx = lax.broadcasted_iota(jnp.int32, (blk, blk), 1)
    causal_bias = jnp.where(r_idx >= c_idx, 0.0, MASKED)
    own = qi % cb
    s = scores(n_past) + jnp.concatenate(
        [causal_bias * (own == c).astype(F32) for c in range(cb)], axis=1)
    s_ref[n_past] = s
    m = jnp.max(lanes_max(mvec, s), axis=-1, keepdims=True)

    def pass2(p, carry):
        l, acc = carry
        e = jnp.exp2(s_ref[p] - m)
        l = l + jnp.sum(e, axis=-1, keepdims=True)
        acc = acc + jnp.dot(e.astype(BF16), v_ref[chunk_rows(p), :], preferred_element_type=F32)
        return l, acc

    l, acc = lax.fori_loop(0, n_past + 1, pass2,
                           (jnp.zeros((blk, 1), F32), jnp.zeros((blk, dh), F32)))
    o_ref[...] = (acc / l).astype(o_ref.dtype)


def moba(proj, slopes, *, batch, seq):
    blk = MOBA_BLOCK
    nb = seq // blk
    assert nb % MOBA_CHUNK_BLOCKS == 0 and nb <= MOBA_MAX_BLOCKS and seq % MOBA_PREP_ROWS == 0
    dh = MOBA_HEAD_DIM
    cq, ck, cv = COL_MOBA_Q // dh, COL_MOBA_K // dh, COL_MOBA_V // dh
    return pl.pallas_call(
        functools.partial(_moba_kernel, nb=nb),
        grid_spec=pltpu.PrefetchScalarGridSpec(
            num_scalar_prefetch=1,
            grid=(batch, MOBA_HEADS, nb),
            in_specs=[pl.BlockSpec((seq, dh), lambda b, h, i, s: (b, cq + h)),
                      pl.BlockSpec((seq, dh), lambda b, h, i, s: (b, ck + h)),
                      pl.BlockSpec((seq, dh), lambda b, h, i, s: (b, cv + h)),
                      pl.BlockSpec((seq, LANES), lambda b, h, i, s: (0, 0))],
            out_specs=pl.BlockSpec((blk, dh), lambda b, h, i, s: (b * nb + i, h)),
            scratch_shapes=[pltpu.VMEM((seq, dh + LANES), BF16), pltpu.VMEM((seq, dh + LANES), BF16),
                            pltpu.VMEM((LANES, dh), BF16), pltpu.VMEM((LANES, dh), BF16),
                            pltpu.VMEM((nb // MOBA_CHUNK_BLOCKS, blk, MOBA_CHUNK_BLOCKS * blk), F32)],
        ),
        out_shape=jax.ShapeDtypeStruct((batch * seq, BRANCH_WIDTH), BF16),
        compiler_params=_params("parallel", "parallel", "arbitrary"),
        name="moba",
    )(slopes, proj, proj, proj, _moba_key_features(seq))


def _swa_kernel(slopes_ref, sinks_ref, q_ref, kp_ref, kc_ref, vp_ref, vc_ref, o_ref):
    w = SWA_WINDOW
    dh = SWA_HEAD_DIM
    n = pl.program_id(1)
    scale = dh ** -0.5
    lane = lax.broadcasted_iota(jnp.int32, (w, LANES), 1)
    low = lane < dh
    t_idx = lax.broadcasted_iota(jnp.int32, (w, w), 0)
    s_idx = lax.broadcasted_iota(jnp.int32, (w, w), 1)
    dist_cur = (t_idx - s_idx).astype(F32)
    dist_prev = dist_cur + float(w)
    ok_cur = s_idx <= t_idx
    ok_prev = (s_idx > t_idx) & (n > 0)

    def halves(ref):
        x = ref[...].astype(F32)
        xr = pltpu.roll(x, dh, axis=1)
        z = jnp.zeros_like(x)
        lo = (jnp.where(low, x, z).astype(BF16), jnp.where(low, z, xr).astype(BF16))
        hi = (jnp.where(low, xr, z).astype(BF16), jnp.where(low, z, x).astype(BF16))
        return lo, hi

    kp, kc, vp, vc = halves(kp_ref), halves(kc_ref), halves(vp_ref), halves(vc_ref)
    pairs = SWA_Q_HEADS // 2
    group = SWA_Q_HEADS // SWA_KV_HEADS
    for pr in range(pairs):
        q = q_ref[:, pr * LANES:(pr + 1) * LANES]
        kh = (2 * pr) // group
        out = None
        for half in range(2):
            hq = 2 * pr + half
            slope = slopes_ref[hq]
            sink = sinks_ref[hq]
            s_p = _dot_nt(q, kp[kh][half]) * scale
            s_c = _dot_nt(q, kc[kh][half]) * scale
            s_p = jnp.where(ok_prev, s_p - slope * dist_prev, MASKED)
            s_c = jnp.where(ok_cur, s_c - slope * dist_cur, MASKED)
            m = jnp.maximum(jnp.max(s_p, axis=-1, keepdims=True), jnp.max(s_c, axis=-1, keepdims=True))
            m = jnp.maximum(m, sink)
            e_p = jnp.exp(s_p - m)
            e_c = jnp.exp(s_c - m)
            denom = (jnp.sum(e_p, axis=-1, keepdims=True) + jnp.sum(e_c, axis=-1, keepdims=True)
                     + jnp.exp(sink - m))
            inv = 1.0 / denom
            o = (jnp.dot((e_p * inv).astype(BF16), vp[kh][half], preferred_element_type=F32)
                 + jnp.dot((e_c * inv).astype(BF16), vc[kh][half], preferred_element_type=F32))
            out = o if out is None else out + o
        o_ref[:, pr * LANES:(pr + 1) * LANES] = out.astype(o_ref.dtype)


def swa(proj, slopes, sinks, *, batch, seq):
    w = SWA_WINDOW
    nblk = seq // w
    bw = BRANCH_WIDTH
    cq = COL_SWA_Q // bw
    ck = COL_SWA_K // SWA_KV_WIDTH
    cv = COL_SWA_V // SWA_KV_WIDTH
    prev = lambda b, n, *_: (b * nblk + jnp.maximum(n - 1, 0))
    cur = lambda b, n, *_: (b * nblk + n)
    return pl.pallas_call(
        _swa_kernel,
        grid_spec=pltpu.PrefetchScalarGridSpec(
            num_scalar_prefetch=2,
            grid=(batch, nblk),
            in_specs=[pl.BlockSpec((w, bw), lambda b, n, *_: (cur(b, n), cq)),
                      pl.BlockSpec((w, SWA_KV_WIDTH), lambda b, n, *_: (prev(b, n), ck)),
                      pl.BlockSpec((w, SWA_KV_WIDTH), lambda b, n, *_: (cur(b, n), ck)),
                      pl.BlockSpec((w, SWA_KV_WIDTH), lambda b, n, *_: (prev(b, n), cv)),
                      pl.BlockSpec((w, SWA_KV_WIDTH), lambda b, n, *_: (cur(b, n), cv))],
            out_specs=pl.BlockSpec((w, bw), lambda b, n, *_: (cur(b, n), 0)),
        ),
        out_shape=jax.ShapeDtypeStruct((batch * seq, bw), BF16),
        compiler_params=_params("parallel", "arbitrary"),
        name="swa",
    )(slopes, sinks, proj, proj, proj, proj, proj)


def _merge_kernel(h_ref, wg_ref0, wg_ref1, wg_ref2, bg_ref, ya_ref, yb_ref, yc_ref, wb_ref, o_ref):
    h = h_ref[...]
    merged = None
    for n, (wg_ref, y_ref) in enumerate(((wg_ref0, ya_ref), (wg_ref1, yb_ref), (wg_ref2, yc_ref))):
        logits = jnp.dot(h, wg_ref[...], preferred_element_type=F32) + bg_ref[n]
        branch = jnp.dot(y_ref[...], wb_ref[n], preferred_element_type=F32)
        term = jax.nn.sigmoid(logits) * branch
        merged = term if merged is None else merged + term
    o_ref[...] = merged.astype(o_ref.dtype)


def merge(h, w_gate, b_gate, y_a, y_b, y_c, w_branch, *, tm=512, tn=256):
    t, d = h.shape
    bw = BRANCH_WIDTH
    nj = d // tn
    gate_spec = lambda n: pl.BlockSpec((d, tn), lambda i, j: (0, n * nj + j))
    y_spec = pl.BlockSpec((tm, bw), lambda i, j: (i, 0))
    return pl.pallas_call(
        _merge_kernel,
        grid=(t // tm, nj),
        in_specs=[pl.BlockSpec((tm, d), lambda i, j: (i, 0)),
                  gate_spec(0), gate_spec(1), gate_spec(2),
                  pl.BlockSpec((N_BRANCHES, 1, tn), lambda i, j: (0, 0, j)),
                  y_spec, y_spec, y_spec,
                  pl.BlockSpec((N_BRANCHES, bw, tn), lambda i, j: (0, 0, j))],
        out_specs=pl.BlockSpec((tm, tn), lambda i, j: (i, j)),
        out_shape=jax.ShapeDtypeStruct((t, d), BF16),
        compiler_params=_params("parallel", "arbitrary"),
        name="merge",
    )(h, w_gate, w_gate, w_gate, b_gate.reshape(N_BRANCHES, 1, d), y_a, y_b, y_c, w_branch)


def _alibi_slopes():
    i = jnp.arange(1, N_ALIBI_HEADS + 1, dtype=F32)
    s = jnp.exp2(-8.0 * i / N_ALIBI_HEADS)
    return s[:SWA_Q_HEADS], s[SWA_Q_HEADS:]


def kernel(x, ffn1_pre_g, ffn1_w_up, ffn1_w_down, ffn1_post_g, mix_pre_g, w_in, gmlp_ln_g, gmlp_ln_b,
           gmlp_w_s, gmlp_b_s, swa_sinks, w_gate, b_gate, w_branch, w_out, mix_post_g, ffn2_pre_g,
           ffn2_w_up, ffn2_w_down, ffn2_post_g):
    batch, seq, d = x.shape
    depth = ffn1_pre_g.shape[0]
    swa_slopes, moba_slopes = _alibi_slopes()
    xf = x.reshape(batch * seq, d)
    h = prenorm(xf, ffn1_pre_g[0])
    for i in range(depth):
        act = ffn_up(h, ffn1_w_up[i].astype(BF16))
        xf, h = mm_norm_res(act, ffn1_w_down[i], xf, ffn1_post_g[i], mix_pre_g[i], res_scale=0.5)

        proj = matmul(h, w_in[i].astype(BF16))
        y_a = gmlp(proj, gmlp_ln_g[i], gmlp_ln_b[i], gmlp_w_s[i], gmlp_b_s[i])
        y_b = moba(proj, moba_slopes, batch=batch, seq=seq)
        y_c = swa(proj, swa_slopes, swa_sinks[i], batch=batch, seq=seq)
        merged = merge(h, w_gate[i].astype(BF16), b_gate[i], y_a, y_b, y_c, w_branch[i].astype(BF16))
        xf, h = mm_norm_res(merged, w_out[i], xf, mix_post_g[i], ffn2_pre_g[i], res_scale=1.0)

        act = ffn_up(h, ffn2_w_up[i].astype(BF16))
        next_g = ffn1_pre_g[i + 1] if i + 1 < depth else None
        xf, h = mm_norm_res(act, ffn2_w_down[i], xf, ffn2_post_g[i], next_g, res_scale=0.5)
    return xf.reshape(batch, seq, d)
```

```python
import functools
import math

import jax
import jax.numpy as jnp
from jax import lax
from jax.experimental import pallas as pl
from jax.experimental.pallas import tpu as pltpu

D_MODEL = 4096
DEPTH = 2
BRANCH_WIDTH = D_MODEL // 4
N_BRANCHES = 3
GMLP_CHUNK = 128
GMLP_GROUP_DIM = 128
GMLP_GROUPS = BRANCH_WIDTH // GMLP_GROUP_DIM
MOBA_HEAD_DIM = 128
MOBA_HEADS = BRANCH_WIDTH // MOBA_HEAD_DIM
MOBA_BLOCK = 256
MOBA_TOPK = 3
SWA_HEAD_DIM = 64
SWA_Q_HEADS = BRANCH_WIDTH // SWA_HEAD_DIM
SWA_KV_HEADS = SWA_Q_HEADS // 8
SWA_WINDOW = 128
SWA_KV_WIDTH = SWA_KV_HEADS * SWA_HEAD_DIM
IN_COLS = 6 * BRANCH_WIDTH + 2 * SWA_KV_WIDTH
D_FF = 2 * D_MODEL
N_ALIBI_HEADS = SWA_Q_HEADS + MOBA_HEADS
RMS_EPS = 1e-6
LN_EPS = 1e-5

LANES = 128
VMEM_LIMIT = 56 * 1024 * 1024
MASKED = -1e30

F32 = jnp.float32
BF16 = jnp.bfloat16

COL_GMLP = 0
COL_MOBA_Q = 2 * BRANCH_WIDTH
COL_MOBA_K = 3 * BRANCH_WIDTH
COL_MOBA_V = 4 * BRANCH_WIDTH
COL_SWA_Q = 5 * BRANCH_WIDTH
COL_SWA_K = 6 * BRANCH_WIDTH
COL_SWA_V = 6 * BRANCH_WIDTH + SWA_KV_WIDTH


def _params(*semantics):
    return pltpu.CompilerParams(dimension_semantics=semantics, vmem_limit_bytes=VMEM_LIMIT)


def _rms_scale(x):
    return lax.rsqrt(jnp.mean(x * x, axis=-1, keepdims=True) + RMS_EPS)


def _dot_nt(a, b):
    return lax.dot_general(a, b, (((1,), (1,)), ((), ())), preferred_element_type=F32)


def _prenorm_kernel(x_ref, g_ref, h_ref):
    x = x_ref[...]
    h_ref[...] = (x * _rms_scale(x) * g_ref[...]).astype(h_ref.dtype)


def prenorm(x, g, *, tm=256):
    t, d = x.shape
    return pl.pallas_call(
        _prenorm_kernel,
        grid=(t // tm,),
        in_specs=[pl.BlockSpec((tm, d), lambda i: (i, 0)),
                  pl.BlockSpec((1, d), lambda i: (0, 0))],
        out_specs=pl.BlockSpec((tm, d), lambda i: (i, 0)),
        out_shape=jax.ShapeDtypeStruct((t, d), BF16),
        compiler_params=_params("parallel"),
        name="prenorm",
    )(x, g.reshape(1, d))


def _ffn_up_kernel(h_ref, wg_ref, wu_ref, o_ref):
    h = h_ref[...]
    g = jnp.dot(h, wg_ref[...], preferred_element_type=F32)
    u = jnp.dot(h, wu_ref[...], preferred_element_type=F32)
    o_ref[...] = (g * jax.nn.sigmoid(g) * u).astype(o_ref.dtype)


def ffn_up(h, w_up, *, tm=1024, tn=512):
    t, d = h.shape
    f = w_up.shape[1] // 2
    nj = f // tn
    return pl.pallas_call(
        _ffn_up_kernel,
        grid=(t // tm, nj),
        in_specs=[pl.BlockSpec((tm, d), lambda i, j: (i, 0)),
                  pl.BlockSpec((d, tn), lambda i, j: (0, j)),
                  pl.BlockSpec((d, tn), lambda i, j: (0, j + nj))],
        out_specs=pl.BlockSpec((tm, tn), lambda i, j: (i, j)),
        out_shape=jax.ShapeDtypeStruct((t, f), BF16),
        compiler_params=_params("parallel", "arbitrary"),
        name="ffn_up",
    )(h, w_up, w_up)


def _mm_norm_res_kernel(*refs, res_scale, nkk, nj, rows, emit_h):
    if emit_h:
        a_ref, w_ref, x_ref, pg_ref, ng_ref, xo_ref, ho_ref, y_ref, ry_ref, sx_ref = refs
    else:
        a_ref, w_ref, x_ref, pg_ref, xo_ref, y_ref, ry_ref, sx_ref = refs
    s = pl.program_id(1)
    n_mm = nkk * nj
    tm, d = y_ref.shape
    tn = w_ref.shape[1]

    def tile_cols(e):
        return pl.ds(pl.multiple_of(e * tn, tn), tn)

    def for_row_chunks(body):
        def step(c, carry):
            body(pl.ds(pl.multiple_of(c * rows, rows), rows))
            return carry
        lax.fori_loop(0, tm // rows, step, 0)

    @pl.when(s < nj)
    def _():
        y_ref[:, tile_cols(s)] = jnp.dot(a_ref[...], w_ref[...], preferred_element_type=F32)

    @pl.when((s >= nj) & (s < n_mm))
    def _():
        y_ref[:, tile_cols(s % nj)] += jnp.dot(a_ref[...], w_ref[...], preferred_element_type=F32)

    @pl.when(s == n_mm)
    def _():
        def stats(r):
            y = y_ref[r, :]
            ry_ref[r, :] = _rms_scale(y)
            sx_ref[r, :] = jnp.zeros((rows, 1), F32)
        for_row_chunks(stats)

    @pl.when((s >= n_mm) & (s < n_mm + nj))
    def _():
        cols = tile_cols(s - n_mm)

        def residual(r):
            xn = x_ref[r, :] + res_scale * (y_ref[r, cols] * ry_ref[r, :] * pg_ref[...])
            xo_ref[r, :] = xn
            y_ref[r, cols] = xn
            sx_ref[r, :] += jnp.sum(xn * xn, axis=-1, keepdims=True)
        for_row_chunks(residual)

    if emit_h:
        @pl.when(s >= n_mm + nj)
        def _():
            cols = tile_cols(s - n_mm - nj)

            def next_norm(r):
                scale = lax.rsqrt(sx_ref[r, :] * (1.0 / d) + RMS_EPS)
                ho_ref[r, :] = (y_ref[r, cols] * scale * ng_ref[...]).astype(BF16)
            for_row_chunks(next_norm)


def _tiled_bf16(w, tk, tn):
    kdim, n = w.shape
    return w.astype(BF16).reshape(kdim // tk, tk, n // tn, tn).transpose(0, 2, 1, 3)


def mm_norm_res(a, w, x, post_g, next_g=None, *, res_scale, tm=1024, tn=512, tkk=2048, rows=256):
    t, kdim = a.shape
    d = w.shape[1]
    nkk, nj = kdim // tkk, d // tn
    n_mm = nkk * nj
    emit_h = next_g is not None
    mm_step = lambda s: jnp.minimum(s, n_mm - 1)
    res_tile = lambda s: jnp.clip(s - n_mm, 0, nj - 1)
    nxt_tile = lambda s: jnp.clip(s - n_mm - nj, 0, nj - 1)
    operands = [a, _tiled_bf16(w, tkk, tn), x, post_g.reshape(1, d)]
    in_specs = [pl.BlockSpec((tm, tkk), lambda i, s: (i, mm_step(s) // nj)),
                pl.BlockSpec((None, None, tkk, tn), lambda i, s: (mm_step(s) // nj, mm_step(s) % nj, 0, 0)),
                pl.BlockSpec((tm, tn), lambda i, s: (i, res_tile(s))),
                pl.BlockSpec((1, tn), lambda i, s: (0, res_tile(s)))]
    out_shape = [jax.ShapeDtypeStruct((t, d), F32)]
    out_specs = [pl.BlockSpec((tm, tn), lambda i, s: (i, res_tile(s)))]
    if emit_h:
        operands.append(next_g.reshape(1, d))
        in_specs.append(pl.BlockSpec((1, tn), lambda i, s: (0, nxt_tile(s))))
        out_shape.append(jax.ShapeDtypeStruct((t, d), BF16))
        out_specs.append(pl.BlockSpec((tm, tn), lambda i, s: (i, nxt_tile(s))))
    res = pl.pallas_call(
        functools.partial(_mm_norm_res_kernel, res_scale=res_scale, nkk=nkk, nj=nj, rows=rows, emit_h=emit_h),
        grid=(t // tm, n_mm + (2 if emit_h else 1) * nj),
        in_specs=in_specs,
        out_specs=out_specs,
        out_shape=out_shape,
        scratch_shapes=[pltpu.VMEM((tm, d), F32), pltpu.VMEM((tm, 1), F32), pltpu.VMEM((tm, 1), F32)],
        compiler_params=_params("parallel", "arbitrary"),
        name="mm_norm_res",
    )(*operands)
    return (res[0], res[1]) if emit_h else (res[0], None)


def _matmul_kernel(a_ref, w_ref, o_ref):
    o_ref[...] = jnp.dot(a_ref[...], w_ref[...], preferred_element_type=F32).astype(o_ref.dtype)


def matmul(a, w, *, tm=1024, tn=1280):
    t, kdim = a.shape
    n = w.shape[1]
    return pl.pallas_call(
        _matmul_kernel,
        grid=(t // tm, n // tn),
        in_specs=[pl.BlockSpec((tm, kdim), lambda i, j: (i, 0)),
                  pl.BlockSpec((kdim, tn), lambda i, j: (0, j))],
        out_specs=pl.BlockSpec((tm, tn), lambda i, j: (i, j)),
        out_shape=jax.ShapeDtypeStruct((t, n), BF16),
        compiler_params=_params("parallel", "arbitrary"),
        name="in_proj",
    )(a, w)


def _gelu_exact(x):
    return 0.5 * x * (1.0 + lax.erf(x * (1.0 / math.sqrt(2.0))))


def _gmlp_kernel(z_ref, lng_ref, lnb_ref, ws_ref, bs_ref, o_ref, *, chunks):
    bw = BRANCH_WIDTH
    row = lax.broadcasted_iota(jnp.int32, (GMLP_CHUNK, GMLP_CHUNK), 0)
    col = lax.broadcasted_iota(jnp.int32, (GMLP_CHUNK, GMLP_CHUNK), 1)
    causal = row >= col
    for c in range(chunks):
        r = slice(c * GMLP_CHUNK, (c + 1) * GMLP_CHUNK)
        u = _gelu_exact(z_ref[r, :bw].astype(F32))
        v = _gelu_exact(z_ref[r, bw:].astype(F32))
        mu = jnp.mean(v, axis=-1, keepdims=True)
        vc = v - mu
        var = jnp.mean(vc * vc, axis=-1, keepdims=True)
        vn = (vc * lax.rsqrt(var + LN_EPS) * lng_ref[...] + lnb_ref[...]).astype(BF16)
        for g in range(GMLP_GROUPS):
            gs = slice(g * GMLP_GROUP_DIM, (g + 1) * GMLP_GROUP_DIM)
            w = jnp.where(causal, ws_ref[g], 0.0).astype(BF16)
            mixed = jnp.dot(w, vn[:, gs], preferred_element_type=F32) + bs_ref[:, g:g + 1]
            o_ref[r, gs] = (u[:, gs] * mixed).astype(o_ref.dtype)


def gmlp(proj, ln_g, ln_b, w_s, b_s, *, chunks=4):
    t = proj.shape[0]
    bw = BRANCH_WIDTH
    tm = chunks * GMLP_CHUNK
    return pl.pallas_call(
        functools.partial(_gmlp_kernel, chunks=chunks),
        grid=(t // tm,),
        in_specs=[pl.BlockSpec((tm, 2 * bw), lambda i: (i, 0)),
                  pl.BlockSpec((1, bw), lambda i: (0, 0)),
                  pl.BlockSpec((1, bw), lambda i: (0, 0)),
                  pl.BlockSpec((GMLP_GROUPS, GMLP_CHUNK, GMLP_CHUNK), lambda i: (0, 0, 0)),
                  pl.BlockSpec((GMLP_CHUNK, GMLP_GROUPS), lambda i: (0, 0))],
        out_specs=pl.BlockSpec((tm, bw), lambda i: (i, 0)),
        out_shape=jax.ShapeDtypeStruct((t, bw), BF16),
        compiler_params=_params("parallel"),
        name="gmlp",
    )(proj, ln_g.reshape(1, bw), ln_b.reshape(1, bw), w_s, b_s.T)


MOBA_MAX_BLOCKS = 32
MOBA_CHUNK_BLOCKS = 4
MOBA_PREP_ROWS = 512
LOG2_E = math.log2(math.e)
FEAT_SEL, FEAT_ONE, FEAT_BLK, FEAT_OFF, FEAT_END = 0, 32, 35, 38, 41


def _moba_key_features(seq):
    pos = jnp.arange(seq, dtype=jnp.int32)[:, None]
    kb, off = pos // MOBA_BLOCK, pos % MOBA_BLOCK
    lane = jnp.arange(LANES, dtype=jnp.int32)[None, :]
    feat = jnp.where(lane < FEAT_ONE, (lane == kb).astype(F32),
                     jnp.where(lane < FEAT_BLK, 1.0,
                               jnp.where(lane < FEAT_OFF, (kb * MOBA_BLOCK).astype(F32),
                                         jnp.where(lane < FEAT_END, off.astype(F32), 0.0))))
    return feat.astype(BF16)


def _split3(x):
    hi = x.astype(BF16).astype(F32)
    mid = (x - hi).astype(BF16).astype(F32)
    return hi, mid, x - hi - mid


def _moba_prepare(slope, q_ref, k_ref, kfeat_ref, kaug_ref, qaug_ref, kmh_ref, kml_ref, *, nb):
    blk = MOBA_BLOCK
    dh = MOBA_HEAD_DIM
    seq = q_ref.shape[0]
    prep = MOBA_PREP_ROWS
    nbp = MOBA_MAX_BLOCKS
    scale = dh ** -0.5 * LOG2_E
    slope = slope * LOG2_E

    kmh_ref[...] = jnp.zeros_like(kmh_ref)
    kml_ref[...] = jnp.zeros_like(kml_ref)
    for n in range(nb):
        rows = slice(n * blk, (n + 1) * blk)
        kn = k_ref[rows, :]
        kaug_ref[rows, :dh] = kn
        kaug_ref[rows, dh:] = kfeat_ref[rows, :]
        km = jnp.mean(kn.astype(F32), axis=0, keepdims=True)
        hi = km.astype(BF16)
        kmh_ref[n:n + 1, :] = hi
        kml_ref[n:n + 1, :] = (km - hi.astype(F32)).astype(BF16)

    blk_id = lax.broadcasted_iota(jnp.int32, (nbp, prep), 0)
    q_off = lax.broadcasted_iota(jnp.int32, (nbp, prep), 1)
    lane = lax.broadcasted_iota(jnp.int32, (prep, LANES), 1)
    row = lax.broadcasted_iota(jnp.int32, (prep, LANES), 0)
    part = jnp.where(lane < FEAT_BLK, lane - FEAT_ONE, jnp.where(lane < FEAT_OFF, lane - FEAT_BLK, lane - FEAT_OFF))
    pick3 = lambda parts: jnp.where(part == 0, parts[0], jnp.where(part == 1, parts[1], parts[2]))
    slope_feat = pick3(_split3(jnp.full((prep, LANES), slope, F32)))

    def rows_chunk(c, carry):
        row0 = pl.multiple_of(c * prep, prep)
        rows = pl.ds(row0, prep)
        q = q_ref[rows, :]
        gate = _dot_nt(kmh_ref[:nbp, :], q) + _dot_nt(kml_ref[:nbp, :], q)
        q_blk = (row0 + q_off) // blk
        gate = jnp.where(blk_id < q_blk, gate, -jnp.inf)
        sel_bias = jnp.where(blk_id == q_blk, 0.0, MASKED)
        for _ in range(MOBA_TOPK):
            best = jnp.max(gate, axis=0, keepdims=True)
            first = jnp.min(jnp.where(gate == best, blk_id, nbp), axis=0, keepdims=True)
            pick = (blk_id == first) & (best > -jnp.inf)
            sel_bias = jnp.where(pick, 0.0, sel_bias)
            gate = jnp.where(pick, -jnp.inf, gate)
        sel_rows = jnp.concatenate([sel_bias, jnp.zeros((LANES - nbp, prep), F32)], axis=0).T

        t_pos = (row0 + row).astype(F32)
        feat = jnp.where(lane < FEAT_ONE, sel_rows,
                         jnp.where(lane < FEAT_BLK, pick3(_split3(-slope * t_pos)),
                                   jnp.where(lane < FEAT_END, slope_feat, 0.0)))
        qaug_ref[rows, :dh] = (q.astype(F32) * scale).astype(BF16)
        qaug_ref[rows, dh:] = feat.astype(BF16)
        return carry

    lax.fori_loop(0, seq // prep, rows_chunk, 0)


def _moba_kernel(slopes_ref, q_ref, k_ref, v_ref, kfeat_ref, o_ref, kaug_ref, qaug_ref, kmh_ref, kml_ref, s_ref,
                 *, nb):
    blk = MOBA_BLOCK
    dh = MOBA_HEAD_DIM
    h = pl.program_id(1)
    qi = pl.program_id(2)

    @pl.when(qi == 0)
    def _():
        _moba_prepare(slopes_ref[h], q_ref, k_ref, kfeat_ref, kaug_ref, qaug_ref, kmh_ref, kml_ref, nb=nb)

    q_aug = qaug_ref[pl.ds(pl.multiple_of(qi * blk, blk), blk), :]

    cb = MOBA_CHUNK_BLOCKS
    span = cb * blk

    def chunk_rows(p):
        return pl.ds(pl.multiple_of(p * span, span), span)

    def scores(p):
        return _dot_nt(q_aug, kaug_ref[chunk_rows(p), :])

    def lanes_max(m, s):
        for c in range(0, s.shape[1], LANES):
            m = jnp.maximum(m, s[:, c:c + LANES])
        return m

    def pass1(p, mvec):
        s = scores(p)
        s_ref[p] = s
        return lanes_max(mvec, s)

    n_past = qi // cb
    mvec = lax.fori_loop(0, n_past, pass1, jnp.full((blk, LANES), -jnp.inf, F32))
    r_idx = lax.broadcasted_iota(jnp.int32, (blk, blk), 0)
    c_idx = lax.broadcasted_iota(jnp.int32, (blk, blk), 1)
    causal_bias = jnp.where(r_idx >= c_idx, 0.0, MASKED)
    own = qi % cb
    s = scores(n_past) + jnp.concatenate(
        [causal_bias * (own == c).astype(F32) for c in range(cb)], axis=1)
    s_ref[n_past] = s
    m = jnp.max(lanes_max(mvec, s), axis=-1, keepdims=True)

    def pass2(p, carry):
        l, acc = carry
        e = jnp.exp2(s_ref[p] - m)
        l = l + jnp.sum(e, axis=-1, keepdims=True)
        acc = acc + jnp.dot(e.astype(BF16), v_ref[chunk_rows(p), :], preferred_element_type=F32)
        return l, acc

    l, acc = lax.fori_loop(0, n_past + 1, pass2,
                           (jnp.zeros((blk, 1), F32), jnp.zeros((blk, dh), F32)))
    o_ref[...] = (acc / l).astype(o_ref.dtype)


def moba(proj, slopes, *, batch, seq):
    blk = MOBA_BLOCK
    nb = seq // blk
    assert nb % MOBA_CHUNK_BLOCKS == 0 and nb <= MOBA_MAX_BLOCKS and seq % MOBA_PREP_ROWS == 0
    dh = MOBA_HEAD_DIM
    cq, ck, cv = COL_MOBA_Q // dh, COL_MOBA_K // dh, COL_MOBA_V // dh
    return pl.pallas_call(
        functools.partial(_moba_kernel, nb=nb),
        grid_spec=pltpu.PrefetchScalarGridSpec(
            num_scalar_prefetch=1,
            grid=(batch, MOBA_HEADS, nb),
            in_specs=[pl.BlockSpec((seq, dh), lambda b, h, i, s: (b, cq + h)),
                      pl.BlockSpec((seq, dh), lambda b, h, i, s: (b, ck + h)),
                      pl.BlockSpec((seq, dh), lambda b, h, i, s: (b, cv + h)),
                      pl.BlockSpec((seq, LANES), lambda b, h, i, s: (0, 0))],
            out_specs=pl.BlockSpec((blk, dh), lambda b, h, i, s: (b * nb + i, h)),
            scratch_shapes=[pltpu.VMEM((seq, dh + LANES), BF16), pltpu.VMEM((seq, dh + LANES), BF16),
                            pltpu.VMEM((LANES, dh), BF16), pltpu.VMEM((LANES, dh), BF16),
                            pltpu.VMEM((nb // MOBA_CHUNK_BLOCKS, blk, MOBA_CHUNK_BLOCKS * blk), F32)],
        ),
        out_shape=jax.ShapeDtypeStruct((batch * seq, BRANCH_WIDTH), BF16),
        compiler_params=_params("parallel", "parallel", "arbitrary"),
        name="moba",
    )(slopes, proj, proj, proj, _moba_key_features(seq))


def _swa_kernel(slopes_ref, sinks_ref, q_ref, kp_ref, kc_ref, vp_ref, vc_ref, o_ref):
    w = SWA_WINDOW
    dh = SWA_HEAD_DIM
    n = pl.program_id(1)
    scale = dh ** -0.5
    lane = lax.broadcasted_iota(jnp.int32, (w, LANES), 1)
    low = lane < dh
    t_idx = lax.broadcasted_iota(jnp.int32, (w, w), 0)
    s_idx = lax.broadcasted_iota(jnp.int32, (w, w), 1)
    dist_cur = (t_idx - s_idx).astype(F32)
    dist_prev = dist_cur + float(w)
    ok_cur = s_idx <= t_idx
    ok_prev = (s_idx > t_idx) & (n > 0)

    def halves(ref):
        x = ref[...].astype(F32)
        xr = pltpu.roll(x, dh, axis=1)
        z = jnp.zeros_like(x)
        lo = (jnp.where(low, x, z).astype(BF16), jnp.where(low, z, xr).astype(BF16))
        hi = (jnp.where(low, xr, z).astype(BF16), jnp.where(low, z, x).astype(BF16))
        return lo, hi

    kp, kc, vp, vc = halves(kp_ref), halves(kc_ref), halves(vp_ref), halves(vc_ref)
    pairs = SWA_Q_HEADS // 2
    group = SWA_Q_HEADS // SWA_KV_HEADS
    for pr in range(pairs):
        q = q_ref[:, pr * LANES:(pr + 1) * LANES]
        kh = (2 * pr) // group
        out = None
        for half in range(2):
            hq = 2 * pr + half
            slope = slopes_ref[hq]
            sink = sinks_ref[hq]
            s_p = _dot_nt(q, kp[kh][half]) * scale
            s_c = _dot_nt(q, kc[kh][half]) * scale
            s_p = jnp.where(ok_prev, s_p - slope * dist_prev, MASKED)
            s_c = jnp.where(ok_cur, s_c - slope * dist_cur, MASKED)
            m = jnp.maximum(jnp.max(s_p, axis=-1, keepdims=True), jnp.max(s_c, axis=-1, keepdims=True))
            m = jnp.maximum(m, sink)
            e_p = jnp.exp(s_p - m)
            e_c = jnp.exp(s_c - m)
            denom = (jnp.sum(e_p, axis=-1, keepdims=True) + jnp.sum(e_c, axis=-1, keepdims=True)
                     + jnp.exp(sink - m))
            inv = 1.0 / denom
            o = (jnp.dot((e_p * inv).astype(BF16), vp[kh][half], preferred_element_type=F32)
                 + jnp.dot((e_c * inv).astype(BF16), vc[kh][half], preferred_element_type=F32))
            out = o if out is None else out + o
        o_ref[:, pr * LANES:(pr + 1) * LANES] = out.astype(o_ref.dtype)


def swa(proj, slopes, sinks, *, batch, seq):
    w = SWA_WINDOW
    nblk = seq // w
    bw = BRANCH_WIDTH
    cq = COL_SWA_Q // bw
    ck = COL_SWA_K // SWA_KV_WIDTH
    cv = COL_SWA_V // SWA_KV_WIDTH
    prev = lambda b, n, *_: (b * nblk + jnp.maximum(n - 1, 0))
    cur = lambda b, n, *_: (b * nblk + n)
    return pl.pallas_call(
        _swa_kernel,
        grid_spec=pltpu.PrefetchScalarGridSpec(
            num_scalar_prefetch=2,
            grid=(batch, nblk),
            in_specs=[pl.BlockSpec((w, bw), lambda b, n, *_: (cur(b, n), cq)),
                      pl.BlockSpec((w, SWA_KV_WIDTH), lambda b, n, *_: (prev(b, n), ck)),
                      pl.BlockSpec((w, SWA_KV_WIDTH), lambda b, n, *_: (cur(b, n), ck)),
                      pl.BlockSpec((w, SWA_KV_WIDTH), lambda b, n, *_: (prev(b, n), cv)),
                      pl.BlockSpec((w, SWA_KV_WIDTH), lambda b, n, *_: (cur(b, n), cv))],
            out_specs=pl.BlockSpec((w, bw), lambda b, n, *_: (cur(b, n), 0)),
        ),
        out_shape=jax.ShapeDtypeStruct((batch * seq, bw), BF16),
        compiler_params=_params("parallel", "arbitrary"),
        name="swa",
    )(slopes, sinks, proj, proj, proj, proj, proj)


def _merge_kernel(h_ref, wg_ref0, wg_ref1, wg_ref2, bg_ref, ya_ref, yb_ref, yc_ref, wb_ref, o_ref):
    h = h_ref[...]
    merged = None
    for n, (wg_ref, y_ref) in enumerate(((wg_ref0, ya_ref), (wg_ref1, yb_ref), (wg_ref2, yc_ref))):
        logits = jnp.dot(h, wg_ref[...], preferred_element_type=F32) + bg_ref[n]
        branch = jnp.dot(y_ref[...], wb_ref[n], preferred_element_type=F32)
        term = jax.nn.sigmoid(logits) * branch
        merged = term if merged is None else merged + term
    o_ref[...] = merged.astype(o_ref.dtype)


def merge(h, w_gate, b_gate, y_a, y_b, y_c, w_branch, *, tm=1024, tn=256):
    t, d = h.shape
    bw = BRANCH_WIDTH
    nj = d // tn
    gate_spec = lambda n: pl.BlockSpec((d, tn), lambda i, j: (0, n * nj + j))
    y_spec = pl.BlockSpec((tm, bw), lambda i, j: (i, 0))
    return pl.pallas_call(
        _merge_kernel,
        grid=(t // tm, nj),
        in_specs=[pl.BlockSpec((tm, d), lambda i, j: (i, 0)),
                  gate_spec(0), gate_spec(1), gate_spec(2),
                  pl.BlockSpec((N_BRANCHES, 1, tn), lambda i, j: (0, 0, j)),
                  y_spec, y_spec, y_spec,
                  pl.BlockSpec((N_BRANCHES, bw, tn), lambda i, j: (0, 0, j))],
        out_specs=pl.BlockSpec((tm, tn), lambda i, j: (i, j)),
        out_shape=jax.ShapeDtypeStruct((t, d), BF16),
        compiler_params=_params("parallel", "arbitrary"),
        name="merge",
    )(h, w_gate, w_gate, w_gate, b_gate.reshape(N_BRANCHES, 1, d), y_a, y_b, y_c, w_branch)


def _alibi_slopes():
    i = jnp.arange(1, N_ALIBI_HEADS + 1, dtype=F32)
    s = jnp.exp2(-8.0 * i / N_ALIBI_HEADS)
    return s[:SWA_Q_HEADS], s[SWA_Q_HEADS:]


def kernel(x, ffn1_pre_g, ffn1_w_up, ffn1_w_down, ffn1_post_g, mix_pre_g, w_in, gmlp_ln_g, gmlp_ln_b,
           gmlp_w_s, gmlp_b_s, swa_sinks, w_gate, b_gate, w_branch, w_out, mix_post_g, ffn2_pre_g,
           ffn2_w_up, ffn2_w_down, ffn2_post_g):
    batch, seq, d = x.shape
    depth = ffn1_pre_g.shape[0]
    swa_slopes, moba_slopes = _alibi_slopes()
    xf = x.reshape(batch * seq, d)
    h = prenorm(xf, ffn1_pre_g[0])
    for i in range(depth):
        act = ffn_up(h, ffn1_w_up[i].astype(BF16))
        xf, h = mm_norm_res(act, ffn1_w_down[i], xf, ffn1_post_g[i], mix_pre_g[i], res_scale=0.5)

        proj = matmul(h, w_in[i].astype(BF16))
        y_a = gmlp(proj, gmlp_ln_g[i], gmlp_ln_b[i], gmlp_w_s[i], gmlp_b_s[i])
        y_b = moba(proj, moba_slopes, batch=batch, seq=seq)
        y_c = swa(proj, swa_slopes, swa_sinks[i], batch=batch, seq=seq)
        merged = merge(h, w_gate[i].astype(BF16), b_gate[i], y_a, y_b, y_c, w_branch[i].astype(BF16))
        xf, h = mm_norm_res(merged, w_out[i], xf, mix_post_g[i], ffn2_pre_g[i], res_scale=1.0)

        act = ffn_up(h, ffn2_w_up[i].astype(BF16))
        next_g = ffn1_pre_g[i + 1] if i + 1 < depth else None
        xf, h = mm_norm_res(act, ffn2_w_down[i], xf, ffn2_post_g[i], next_g, res_scale=0.5)
    return xf.reshape(batch, seq, d)
```

```python
import functools
import math

import jax
import jax.numpy as jnp
from jax import lax
from jax.experimental import pallas as pl
from jax.experimental.pallas import tpu as pltpu

D_MODEL = 4096
DEPTH = 2
BRANCH_WIDTH = D_MODEL // 4
N_BRANCHES = 3
GMLP_CHUNK = 128
GMLP_GROUP_DIM = 128
GMLP_GROUPS = BRANCH_WIDTH // GMLP_GROUP_DIM
MOBA_HEAD_DIM = 128
MOBA_HEADS = BRANCH_WIDTH // MOBA_HEAD_DIM
MOBA_BLOCK = 256
MOBA_TOPK = 3
SWA_HEAD_DIM = 64
SWA_Q_HEADS = BRANCH_WIDTH // SWA_HEAD_DIM
SWA_KV_HEADS = SWA_Q_HEADS // 8
SWA_WINDOW = 128
SWA_KV_WIDTH = SWA_KV_HEADS * SWA_HEAD_DIM
IN_COLS = 6 * BRANCH_WIDTH + 2 * SWA_KV_WIDTH
D_FF = 2 * D_MODEL
N_ALIBI_HEADS = SWA_Q_HEADS + MOBA_HEADS
RMS_EPS = 1e-6
LN_EPS = 1e-5

LANES = 128
VMEM_LIMIT = 56 * 1024 * 1024
MASKED = -1e30

F32 = jnp.float32
BF16 = jnp.bfloat16

COL_GMLP = 0
COL_MOBA_Q = 2 * BRANCH_WIDTH
COL_MOBA_K = 3 * BRANCH_WIDTH
COL_MOBA_V = 4 * BRANCH_WIDTH
COL_SWA_Q = 5 * BRANCH_WIDTH
COL_SWA_K = 6 * BRANCH_WIDTH
COL_SWA_V = 6 * BRANCH_WIDTH + SWA_KV_WIDTH


def _params(*semantics):
    return pltpu.CompilerParams(dimension_semantics=semantics, vmem_limit_bytes=VMEM_LIMIT)


def _rms_scale(x):
    return lax.rsqrt(jnp.mean(x * x, axis=-1, keepdims=True) + RMS_EPS)


def _dot_nt(a, b):
    return lax.dot_general(a, b, (((1,), (1,)), ((), ())), preferred_element_type=F32)


def _prenorm_kernel(x_ref, g_ref, h_ref, r_ref):
    x = x_ref[...]
    h_ref[...] = (x * g_ref[...]).astype(h_ref.dtype)
    r_ref[...] = _rms_scale(x)


def prenorm(x, g, *, tm=256):
    t, d = x.shape
    return pl.pallas_call(
        _prenorm_kernel,
        grid=(t // tm,),
        in_specs=[pl.BlockSpec((tm, d), lambda i: (i, 0)),
                  pl.BlockSpec((1, d), lambda i: (0, 0))],
        out_specs=[pl.BlockSpec((tm, d), lambda i: (i, 0)),
                   pl.BlockSpec((tm, 1), lambda i: (i, 0))],
        out_shape=[jax.ShapeDtypeStruct((t, d), BF16), jax.ShapeDtypeStruct((t, 1), F32)],
        compiler_params=_params("parallel"),
        name="prenorm",
    )(x, g.reshape(1, d))


def _ffn_up_kernel(h_ref, r_ref, wg_ref, wu_ref, o_ref):
    h = h_ref[...]
    r = r_ref[...]
    g = r * jnp.dot(h, wg_ref[...], preferred_element_type=F32)
    u = r * jnp.dot(h, wu_ref[...], preferred_element_type=F32)
    o_ref[...] = (g * jax.nn.sigmoid(g) * u).astype(o_ref.dtype)


def ffn_up(h, r, w_up, *, tm=1024, tn=512):
    t, d = h.shape
    f = w_up.shape[1] // 2
    nj = f // tn
    return pl.pallas_call(
        _ffn_up_kernel,
        grid=(t // tm, nj),
        in_specs=[pl.BlockSpec((tm, d), lambda i, j: (i, 0)),
                  pl.BlockSpec((tm, 1), lambda i, j: (i, 0)),
                  pl.BlockSpec((d, tn), lambda i, j: (0, j)),
                  pl.BlockSpec((d, tn), lambda i, j: (0, j + nj))],
        out_specs=pl.BlockSpec((tm, tn), lambda i, j: (i, j)),
        out_shape=jax.ShapeDtypeStruct((t, f), BF16),
        compiler_params=_params("parallel", "arbitrary"),
        name="ffn_up",
    )(h, r, w_up, w_up)


def _mm_norm_res_kernel(*refs, res_scale, nkk, nj, ni, rows, emit_h):
    if emit_h:
        a_ref, w_ref, x_ref, pg_ref, ng_ref, xo_ref, ho_ref, ro_ref, y_ref, ry_ref, sx_ref = refs
    else:
        a_ref, w_ref, x_ref, pg_ref, xo_ref, y_ref, ry_ref = refs
    b = pl.program_id(0)
    s = pl.program_id(1)
    tm, d = y_ref.shape
    tn = w_ref.shape[1]

    def tile_cols(e):
        return pl.ds(pl.multiple_of(e * tn, tn), tn)

    def for_row_chunks(body):
        def step(c, carry):
            body(pl.ds(pl.multiple_of(c * rows, rows), rows))
            return carry
        lax.fori_loop(0, tm // rows, step, 0)

    @pl.when((b > 0) & (s == 0))
    def _():
        def stats(r):
            ry_ref[r, :] = _rms_scale(y_ref[r, :])
            if emit_h:
                sx_ref[r, :] = jnp.zeros((rows, 1), F32)
        for_row_chunks(stats)

    @pl.when((b > 0) & (s < nj))
    def _():
        cols = tile_cols(s)

        def residual(r):
            xn = x_ref[r, :] + res_scale * (y_ref[r, cols] * ry_ref[r, :] * pg_ref[...])
            xo_ref[r, :] = xn
            if emit_h:
                ho_ref[r, :] = (xn * ng_ref[...]).astype(BF16)
                sx_ref[r, :] += jnp.sum(xn * xn, axis=-1, keepdims=True)
        for_row_chunks(residual)

    if emit_h:
        @pl.when((b > 0) & (s == nj - 1))
        def _():
            ro_ref[...] = lax.rsqrt(sx_ref[...] * (1.0 / d) + RMS_EPS)

    @pl.when((b < ni) & (s < nj))
    def _():
        y_ref[:, tile_cols(s)] = jnp.dot(a_ref[...], w_ref[...], preferred_element_type=F32)

    @pl.when((b < ni) & (s >= nj))
    def _():
        y_ref[:, tile_cols(s % nj)] += jnp.dot(a_ref[...], w_ref[...], preferred_element_type=F32)


def _tiled_bf16(w, tk, tn):
    kdim, n = w.shape
    return w.astype(BF16).reshape(kdim // tk, tk, n // tn, tn).transpose(0, 2, 1, 3)


def mm_norm_res(a, w, x, post_g, next_g=None, *, res_scale, tm=1024, tn=512, tkk=2048, rows=256):
    t, kdim = a.shape
    d = w.shape[1]
    ni, nkk, nj = t // tm, kdim // tkk, d // tn
    n_mm = nkk * nj
    emit_h = next_g is not None
    mm_row = lambda b: jnp.minimum(b, ni - 1)
    mm_step = lambda b, s: jnp.where(b < ni, s, n_mm - 1)
    ep_row = lambda b: jnp.maximum(b - 1, 0)
    ep_tile = lambda b, s: jnp.where(b > 0, jnp.minimum(s, nj - 1), 0)
    tile_spec = pl.BlockSpec((tm, tn), lambda b, s: (ep_row(b), ep_tile(b, s)))
    gain_spec = pl.BlockSpec((1, tn), lambda b, s: (0, ep_tile(b, s)))
    operands = [a, _tiled_bf16(w, tkk, tn), x, post_g.reshape(1, d)]
    in_specs = [pl.BlockSpec((tm, tkk), lambda b, s: (mm_row(b), mm_step(b, s) // nj)),
                pl.BlockSpec((None, None, tkk, tn), lambda b, s: (mm_step(b, s) // nj, mm_step(b, s) % nj, 0, 0)),
                tile_spec, gain_spec]
    out_shape = [jax.ShapeDtypeStruct((t, d), F32)]
    out_specs = [tile_spec]
    scratch = [pltpu.VMEM((tm, d), F32), pltpu.VMEM((tm, 1), F32)]
    if emit_h:
        operands.append(next_g.reshape(1, d))
        in_specs.append(gain_spec)
        out_shape += [jax.ShapeDtypeStruct((t, d), BF16), jax.ShapeDtypeStruct((t, 1), F32)]
        out_specs += [tile_spec, pl.BlockSpec((tm, 1), lambda b, s: (ep_row(b), 0))]
        scratch.append(pltpu.VMEM((tm, 1), F32))
    res = pl.pallas_call(
        functools.partial(_mm_norm_res_kernel, res_scale=res_scale, nkk=nkk, nj=nj, ni=ni, rows=rows,
                          emit_h=emit_h),
        grid=(ni + 1, n_mm),
        in_specs=in_specs,
        out_specs=out_specs,
        out_shape=out_shape,
        scratch_shapes=scratch,
        compiler_params=_params("arbitrary", "arbitrary"),
        name="mm_norm_res",
    )(*operands)
    return (res[0], res[1], res[2]) if emit_h else (res[0], None, None)


def _matmul_kernel(a_ref, r_ref, w_ref, o_ref):
    acc = jnp.dot(a_ref[...], w_ref[...], preferred_element_type=F32)
    o_ref[...] = (r_ref[...] * acc).astype(o_ref.dtype)


def matmul(a, r, w, *, tm=1024, tn=1280):
    t, kdim = a.shape
    n = w.shape[1]
    return pl.pallas_call(
        _matmul_kernel,
        grid=(t // tm, n // tn),
        in_specs=[pl.BlockSpec((tm, kdim), lambda i, j: (i, 0)),
                  pl.BlockSpec((tm, 1), lambda i, j: (i, 0)),
                  pl.BlockSpec((kdim, tn), lambda i, j: (0, j))],
        out_specs=pl.BlockSpec((tm, tn), lambda i, j: (i, j)),
        out_shape=jax.ShapeDtypeStruct((t, n), BF16),
        compiler_params=_params("parallel", "arbitrary"),
        name="in_proj",
    )(a, r, w)


def _gelu_exact(x):
    return 0.5 * x * (1.0 + lax.erf(x * (1.0 / math.sqrt(2.0))))


def _gmlp_kernel(z_ref, lng_ref, lnb_ref, ws_ref, bs_ref, o_ref, *, chunks):
    bw = BRANCH_WIDTH
    row = lax.broadcasted_iota(jnp.int32, (GMLP_CHUNK, GMLP_CHUNK), 0)
    col = lax.broadcasted_iota(jnp.int32, (GMLP_CHUNK, GMLP_CHUNK), 1)
    causal = row >= col
    for c in range(chunks):
        r = slice(c * GMLP_CHUNK, (c + 1) * GMLP_CHUNK)
        u = _gelu_exact(z_ref[r, :bw].astype(F32))
        v = _gelu_exact(z_ref[r, bw:].astype(F32))
        mu = jnp.mean(v, axis=-1, keepdims=True)
        vc = v - mu
        var = jnp.mean(vc * vc, axis=-1, keepdims=True)
        vn = (vc * lax.rsqrt(var + LN_EPS) * lng_ref[...] + lnb_ref[...]).astype(BF16)
        for g in range(GMLP_GROUPS):
            gs = slice(g * GMLP_GROUP_DIM, (g + 1) * GMLP_GROUP_DIM)
            w = jnp.where(causal, ws_ref[g], 0.0).astype(BF16)
            mixed = jnp.dot(w, vn[:, gs], preferred_element_type=F32) + bs_ref[:, g:g + 1]
            o_ref[r, gs] = (u[:, gs] * mixed).astype(o_ref.dtype)


def gmlp(proj, ln_g, ln_b, w_s, b_s, *, chunks=4):
    t = proj.shape[0]
    bw = BRANCH_WIDTH
    tm = chunks * GMLP_CHUNK
    return pl.pallas_call(
        functools.partial(_gmlp_kernel, chunks=chunks),
        grid=(t // tm,),
        in_specs=[pl.BlockSpec((tm, 2 * bw), lambda i: (i, 0)),
                  pl.BlockSpec((1, bw), lambda i: (0, 0)),
                  pl.BlockSpec((1, bw), lambda i: (0, 0)),
                  pl.BlockSpec((GMLP_GROUPS, GMLP_CHUNK, GMLP_CHUNK), lambda i: (0, 0, 0)),
                  pl.BlockSpec((GMLP_CHUNK, GMLP_GROUPS), lambda i: (0, 0))],
        out_specs=pl.BlockSpec((tm, bw), lambda i: (i, 0)),
        out_shape=jax.ShapeDtypeStruct((t, bw), BF16),
        compiler_params=_params("parallel"),
        name="gmlp",
    )(proj, ln_g.reshape(1, bw), ln_b.reshape(1, bw), w_s, b_s.T)


MOBA_MAX_BLOCKS = 32
MOBA_CHUNK_BLOCKS = 4
MOBA_PREP_ROWS = 512
LOG2_E = math.log2(math.e)
FEAT_SEL, FEAT_ONE, FEAT_BLK, FEAT_OFF, FEAT_END = 0, 32, 35, 38, 41


def _moba_key_features(seq):
    pos = jnp.arange(seq, dtype=jnp.int32)[:, None]
    kb, off = pos // MOBA_BLOCK, pos % MOBA_BLOCK
    lane = jnp.arange(LANES, dtype=jnp.int32)[None, :]
    feat = jnp.where(lane < FEAT_ONE, (lane == kb).astype(F32),
                     jnp.where(lane < FEAT_BLK, 1.0,
                               jnp.where(lane < FEAT_OFF, (kb * MOBA_BLOCK).astype(F32),
                                         jnp.where(lane < FEAT_END, off.astype(F32), 0.0))))
    return feat.astype(BF16)


def _split3(x):
    hi = x.astype(BF16).astype(F32)
    mid = (x - hi).astype(BF16).astype(F32)
    return hi, mid, x - hi - mid


def _moba_prepare(slope, q_ref, k_ref, kfeat_ref, kaug_ref, qaug_ref, kmh_ref, kml_ref, *, nb):
    blk = MOBA_BLOCK
    dh = MOBA_HEAD_DIM
    seq = q_ref.shape[0]
    prep = MOBA_PREP_ROWS
    nbp = MOBA_MAX_BLOCKS
    scale = dh ** -0.5 * LOG2_E
    slope = slope * LOG2_E

    kmh_ref[...] = jnp.zeros_like(kmh_ref)
    kml_ref[...] = jnp.zeros_like(kml_ref)
    for n in range(nb):
        rows = slice(n * blk, (n + 1) * blk)
        kn = k_ref[rows, :]
        kaug_ref[rows, :dh] = kn
        kaug_ref[rows, dh:] = kfeat_ref[rows, :]
        km = jnp.mean(kn.astype(F32), axis=0, keepdims=True)
        hi = km.astype(BF16)
        kmh_ref[n:n + 1, :] = hi
        kml_ref[n:n + 1, :] = (km - hi.astype(F32)).astype(BF16)

    blk_id = lax.broadcasted_iota(jnp.int32, (nbp, prep), 0)
    q_off = lax.broadcasted_iota(jnp.int32, (nbp, prep), 1)
    lane = lax.broadcasted_iota(jnp.int32, (prep, LANES), 1)
    row = lax.broadcasted_iota(jnp.int32, (prep, LANES), 0)
    part = jnp.where(lane < FEAT_BLK, lane - FEAT_ONE, jnp.where(lane < FEAT_OFF, lane - FEAT_BLK, lane - FEAT_OFF))
    pick3 = lambda parts: jnp.where(part == 0, parts[0], jnp.where(part == 1, parts[1], parts[2]))
    slope_feat = pick3(_split3(jnp.full((prep, LANES), slope, F32)))

    def rows_chunk(c, carry):
        row0 = pl.multiple_of(c * prep, prep)
        rows = pl.ds(row0, prep)
        q = q_ref[rows, :]
        gate = _dot_nt(kmh_ref[:nbp, :], q) + _dot_nt(kml_ref[:nbp, :], q)
        q_blk = (row0 + q_off) // blk
        gate = jnp.where(blk_id < q_blk, gate, -jnp.inf)
        sel_bias = jnp.where(blk_id == q_blk, 0.0, MASKED)
        for _ in range(MOBA_TOPK):
            best = jnp.max(gate, axis=0, keepdims=True)
            first = jnp.min(jnp.where(gate == best, blk_id, nbp), axis=0, keepdims=True)
            pick = (blk_id == first) & (best > -jnp.inf)
            sel_bias = jnp.where(pick, 0.0, sel_bias)
            gate = jnp.where(pick, -jnp.inf, gate)
        sel_rows = jnp.concatenate([sel_bias, jnp.zeros((LANES - nbp, prep), F32)], axis=0).T

        t_pos = (row0 + row).astype(F32)
        feat = jnp.where(lane < FEAT_ONE, sel_rows,
                         jnp.where(lane < FEAT_BLK, pick3(_split3(-slope * t_pos)),
                                   jnp.where(lane < FEAT_END, slope_feat, 0.0)))
        qaug_ref[rows, :dh] = (q.astype(F32) * scale).astype(BF16)
        qaug_ref[rows, dh:] = feat.astype(BF16)
        return carry

    lax.fori_loop(0, seq // prep, rows_chunk, 0)


def _moba_kernel(slopes_ref, q_ref, k_ref, v_ref, kfeat_ref, o_ref, kaug_ref, qaug_ref, kmh_ref, kml_ref, s_ref,
                 *, nb):
    blk = MOBA_BLOCK
    dh = MOBA_HEAD_DIM
    h = pl.program_id(1)
    qi = pl.program_id(2)

    @pl.when(qi == 0)
    def _():
        _moba_prepare(slopes_ref[h], q_ref, k_ref, kfeat_ref, kaug_ref, qaug_ref, kmh_ref, kml_ref, nb=nb)

    q_aug = qaug_ref[pl.ds(pl.multiple_of(qi * blk, blk), blk), :]

    cb = MOBA_CHUNK_BLOCKS
    span = cb * blk

    def chunk_rows(p):
        return pl.ds(pl.multiple_of(p * span, span), span)

    def scores(p):
        return _dot_nt(q_aug, kaug_ref[chunk_rows(p), :])

    def lanes_max(m, s):
        for c in range(0, s.shape[1], LANES):
            m = jnp.maximum(m, s[:, c:c + LANES])
        return m

    def pass1(p, mvec):
        s = scores(p)
        s_ref[p] = s
        return lanes_max(mvec, s)

    n_past = qi // cb
    mvec = lax.fori_loop(0, n_past, pass1, jnp.full((blk, LANES), -jnp.inf, F32))
    r_idx = lax.broadcasted_iota(jnp.int32, (blk, blk), 0)
    c_idx = lax.broadcasted_iota(jnp.int32, (blk, blk), 1)
    causal_bias = jnp.where(r_idx >= c_idx, 0.0, MASKED)
    own = qi % cb
    s = scores(n_past) + jnp.concatenate(
        [causal_bias * (own == c).astype(F32) for c in range(cb)], axis=1)
    s_ref[n_past] = s
    m = jnp.max(lanes_max(mvec, s), axis=-1, keepdims=True)

    def pass2(p, carry):
        l, acc = carry
        e = jnp.exp2(s_ref[p] - m)
        l = l + jnp.sum(e, axis=-1, keepdims=True)
        acc = acc + jnp.dot(e.astype(BF16), v_ref[chunk_rows(p), :], preferred_element_type=F32)
        return l, acc

    l, acc = lax.fori_loop(0, n_past + 1, pass2,
                           (jnp.zeros((blk, 1), F32), jnp.zeros((blk, dh), F32)))
    o_ref[...] = (acc / l).astype(o_ref.dtype)


def moba(proj, slopes, *, batch, seq):
    blk = MOBA_BLOCK
    nb = seq // blk
    assert nb % MOBA_CHUNK_BLOCKS == 0 and nb <= MOBA_MAX_BLOCKS and seq % MOBA_PREP_ROWS == 0
    dh = MOBA_HEAD_DIM
    cq, ck, cv = COL_MOBA_Q // dh, COL_MOBA_K // dh, COL_MOBA_V // dh
    return pl.pallas_call(
        functools.partial(_moba_kernel, nb=nb),
        grid_spec=pltpu.PrefetchScalarGridSpec(
            num_scalar_prefetch=1,
            grid=(batch, MOBA_HEADS, nb),
            in_specs=[pl.BlockSpec((seq, dh), lambda b, h, i, s: (b, cq + h)),
                      pl.BlockSpec((seq, dh), lambda b, h, i, s: (b, ck + h)),
                      pl.BlockSpec((seq, dh), lambda b, h, i, s: (b, cv + h)),
                      pl.BlockSpec((seq, LANES), lambda b, h, i, s: (0, 0))],
            out_specs=pl.BlockSpec((blk, dh), lambda b, h, i, s: (b * nb + i, h)),
            scratch_shapes=[pltpu.VMEM((seq, dh + LANES), BF16), pltpu.VMEM((seq, dh + LANES), BF16),
                            pltpu.VMEM((LANES, dh), BF16), pltpu.VMEM((LANES, dh), BF16),
                            pltpu.VMEM((nb // MOBA_CHUNK_BLOCKS, blk, MOBA_CHUNK_BLOCKS * blk), F32)],
        ),
        out_shape=jax.ShapeDtypeStruct((batch * seq, BRANCH_WIDTH), BF16),
        compiler_params=_params("parallel", "parallel", "arbitrary"),
        name="moba",
    )(slopes, proj, proj, proj, _moba_key_features(seq))


def _swa_kernel(slopes_ref, sinks_ref, q_ref, kp_ref, kc_ref, vp_ref, vc_ref, o_ref):
    w = SWA_WINDOW
    dh = SWA_HEAD_DIM
    n = pl.program_id(1)
    scale = dh ** -0.5
    lane = lax.broadcasted_iota(jnp.int32, (w, LANES), 1)
    low = lane < dh
    t_idx = lax.broadcasted_iota(jnp.int32, (w, w), 0)
    s_idx = lax.broadcasted_iota(jnp.int32, (w, w), 1)
    dist_cur = (t_idx - s_idx).astype(F32)
    dist_prev = dist_cur + float(w)
    ok_cur = s_idx <= t_idx
    ok_prev = (s_idx > t_idx) & (n > 0)

    def halves(ref):
        x = ref[...].astype(F32)
        xr = pltpu.roll(x, dh, axis=1)
        z = jnp.zeros_like(x)
        lo = (jnp.where(low, x, z).astype(BF16), jnp.where(low, z, xr).astype(BF16))
        hi = (jnp.where(low, xr, z).astype(BF16), jnp.where(low, z, x).astype(BF16))
        return lo, hi

    kp, kc, vp, vc = halves(kp_ref), halves(kc_ref), halves(vp_ref), halves(vc_ref)
    pairs = SWA_Q_HEADS // 2
    group = SWA_Q_HEADS // SWA_KV_HEADS
    for pr in range(pairs):
        q = q_ref[:, pr * LANES:(pr + 1) * LANES]
        kh = (2 * pr) // group
        out = None
        for half in range(2):
            hq = 2 * pr + half
            slope = slopes_ref[hq]
            sink = sinks_ref[hq]
            s_p = _dot_nt(q, kp[kh][half]) * scale
            s_c = _dot_nt(q, kc[kh][half]) * scale
            s_p = jnp.where(ok_prev, s_p - slope * dist_prev, MASKED)
            s_c = jnp.where(ok_cur, s_c - slope * dist_cur, MASKED)
            m = jnp.maximum(jnp.max(s_p, axis=-1, keepdims=True), jnp.max(s_c, axis=-1, keepdims=True))
            m = jnp.maximum(m, sink)
            e_p = jnp.exp(s_p - m)
            e_c = jnp.exp(s_c - m)
            denom = (jnp.sum(e_p, axis=-1, keepdims=True) + jnp.sum(e_c, axis=-1, keepdims=True)
                     + jnp.exp(sink - m))
            inv = 1.0 / denom
            o = (jnp.dot((e_p * inv).astype(BF16), vp[kh][half], preferred_element_type=F32)
                 + jnp.dot((e_c * inv).astype(BF16), vc[kh][half], preferred_element_type=F32))
            out = o if out is None else out + o
        o_ref[:, pr * LANES:(pr + 1) * LANES] = out.astype(o_ref.dtype)


def swa(proj, slopes, sinks, *, batch, seq):
    w = SWA_WINDOW
    nblk = seq // w
    bw = BRANCH_WIDTH
    cq = COL_SWA_Q // bw
    ck = COL_SWA_K // SWA_KV_WIDTH
    cv = COL_SWA_V // SWA_KV_WIDTH
    prev = lambda b, n, *_: (b * nblk + jnp.maximum(n - 1, 0))
    cur = lambda b, n, *_: (b * nblk + n)
    return pl.pallas_call(
        _swa_kernel,
        grid_spec=pltpu.PrefetchScalarGridSpec(
            num_scalar_prefetch=2,
            grid=(batch, nblk),
            in_specs=[pl.BlockSpec((w, bw), lambda b, n, *_: (cur(b, n), cq)),
                      pl.BlockSpec((w, SWA_KV_WIDTH), lambda b, n, *_: (prev(b, n), ck)),
                      pl.BlockSpec((w, SWA_KV_WIDTH), lambda b, n, *_: (cur(b, n), ck)),
                      pl.BlockSpec((w, SWA_KV_WIDTH), lambda b, n, *_: (prev(b, n), cv)),
                      pl.BlockSpec((w, SWA_KV_WIDTH), lambda b, n, *_: (cur(b, n), cv))],
            out_specs=pl.BlockSpec((w, bw), lambda b, n, *_: (cur(b, n), 0)),
        ),
        out_shape=jax.ShapeDtypeStruct((batch * seq, bw), BF16),
        compiler_params=_params("parallel", "arbitrary"),
        name="swa",
    )(slopes, sinks, proj, proj, proj, proj, proj)


def _merge_kernel(h_ref, r_ref, wg_ref0, wg_ref1, wg_ref2, bg_ref, ya_ref, yb_ref, yc_ref, wb_ref, o_ref):
    h = h_ref[...]
    r = r_ref[...]
    merged = None
    for n, (wg_ref, y_ref) in enumerate(((wg_ref0, ya_ref), (wg_ref1, yb_ref), (wg_ref2, yc_ref))):
        logits = r * jnp.dot(h, wg_ref[...], preferred_element_type=F32) + bg_ref[n]
        branch = jnp.dot(y_ref[...], wb_ref[n], preferred_element_type=F32)
        term = jax.nn.sigmoid(logits) * branch
        merged = term if merged is None else merged + term
    o_ref[...] = merged.astype(o_ref.dtype)


def merge(h, r, w_gate, b_gate, y_a, y_b, y_c, w_branch, *, tm=1024, tn=256):
    t, d = h.shape
    bw = BRANCH_WIDTH
    nj = d // tn
    gate_spec = lambda n: pl.BlockSpec((d, tn), lambda i, j: (0, n * nj + j))
    y_spec = pl.BlockSpec((tm, bw), lambda i, j: (i, 0))
    return pl.pallas_call(
        _merge_kernel,
        grid=(t // tm, nj),
        in_specs=[pl.BlockSpec((tm, d), lambda i, j: (i, 0)),
                  pl.BlockSpec((tm, 1), lambda i, j: (i, 0)),
                  gate_spec(0), gate_spec(1), gate_spec(2),
                  pl.BlockSpec((N_BRANCHES, 1, tn), lambda i, j: (0, 0, j)),
                  y_spec, y_spec, y_spec,
                  pl.BlockSpec((N_BRANCHES, bw, tn), lambda i, j: (0, 0, j))],
        out_specs=pl.BlockSpec((tm, tn), lambda i, j: (i, j)),
        out_shape=jax.ShapeDtypeStruct((t, d), BF16),
        compiler_params=_params("parallel", "arbitrary"),
        name="merge",
    )(h, r, w_gate, w_gate, w_gate, b_gate.reshape(N_BRANCHES, 1, d), y_a, y_b, y_c, w_branch)


def _alibi_slopes():
    i = jnp.arange(1, N_ALIBI_HEADS + 1, dtype=F32)
    s = jnp.exp2(-8.0 * i / N_ALIBI_HEADS)
    return s[:SWA_Q_HEADS], s[SWA_Q_HEADS:]


def kernel(x, ffn1_pre_g, ffn1_w_up, ffn1_w_down, ffn1_post_g, mix_pre_g, w_in, gmlp_ln_g, gmlp_ln_b,
           gmlp_w_s, gmlp_b_s, swa_sinks, w_gate, b_gate, w_branch, w_out, mix_post_g, ffn2_pre_g,
           ffn2_w_up, ffn2_w_down, ffn2_post_g):
    batch, seq, d = x.shape
    depth = ffn1_pre_g.shape[0]
    swa_slopes, moba_slopes = _alibi_slopes()
    xf = x.reshape(batch * seq, d)
    h, r = prenorm(xf, ffn1_pre_g[0])
    for i in range(depth):
        act = ffn_up(h, r, ffn1_w_up[i].astype(BF16))
        xf, h, r = mm_norm_res(act, ffn1_w_down[i], xf, ffn1_post_g[i], mix_pre_g[i], res_scale=0.5)

        proj = matmul(h, r, w_in[i].astype(BF16))
        y_a = gmlp(proj, gmlp_ln_g[i], gmlp_ln_b[i], gmlp_w_s[i], gmlp_b_s[i])
        y_b = moba(proj, moba_slopes, batch=batch, seq=seq)
        y_c = swa(proj, swa_slopes, swa_sinks[i], batch=batch, seq=seq)
        merged = merge(h, r, w_gate[i].astype(BF16), b_gate[i], y_a, y_b, y_c, w_branch[i].astype(BF16))
        xf, h, r = mm_norm_res(merged, w_out[i], xf, mix_post_g[i], ffn2_pre_g[i], res_scale=1.0)

        act = ffn_up(h, r, ffn2_w_up[i].astype(BF16))
        next_g = ffn1_pre_g[i + 1] if i + 1 < depth else None
        xf, h, r = mm_norm_res(act, ffn2_w_down[i], xf, ffn2_post_g[i], next_g, res_scale=0.5)
    return xf.reshape(batch, seq, d)
```

```python
import functools
import math

import jax
import jax.numpy as jnp
from jax import lax
from jax.experimental import pallas as pl
from jax.experimental.pallas import tpu as pltpu

D_MODEL = 4096
DEPTH = 2
BRANCH_WIDTH = D_MODEL // 4
N_BRANCHES = 3
GMLP_CHUNK = 128
GMLP_GROUP_DIM = 128
GMLP_GROUPS = BRANCH_WIDTH // GMLP_GROUP_DIM
MOBA_HEAD_DIM = 128
MOBA_HEADS = BRANCH_WIDTH // MOBA_HEAD_DIM
MOBA_BLOCK = 256
MOBA_TOPK = 3
SWA_HEAD_DIM = 64
SWA_Q_HEADS = BRANCH_WIDTH // SWA_HEAD_DIM
SWA_KV_HEADS = SWA_Q_HEADS // 8
SWA_WINDOW = 128
SWA_KV_WIDTH = SWA_KV_HEADS * SWA_HEAD_DIM
IN_COLS = 6 * BRANCH_WIDTH + 2 * SWA_KV_WIDTH
D_FF = 2 * D_MODEL
N_ALIBI_HEADS = SWA_Q_HEADS + MOBA_HEADS
RMS_EPS = 1e-6
LN_EPS = 1e-5

LANES = 128
VMEM_LIMIT = 56 * 1024 * 1024
MASKED = -1e30

F32 = jnp.float32
BF16 = jnp.bfloat16

COL_GMLP = 0
COL_MOBA_Q = 2 * BRANCH_WIDTH
COL_MOBA_K = 3 * BRANCH_WIDTH
COL_MOBA_V = 4 * BRANCH_WIDTH
COL_SWA_Q = 5 * BRANCH_WIDTH
COL_SWA_K = 6 * BRANCH_WIDTH
COL_SWA_V = 6 * BRANCH_WIDTH + SWA_KV_WIDTH


def _params(*semantics):
    return pltpu.CompilerParams(dimension_semantics=semantics, vmem_limit_bytes=VMEM_LIMIT)


def _rms_scale(x):
    return lax.rsqrt(jnp.mean(x * x, axis=-1, keepdims=True) + RMS_EPS)


def _dot_nt(a, b):
    return lax.dot_general(a, b, (((1,), (1,)), ((), ())), preferred_element_type=F32)


def _prenorm_kernel(x_ref, g_ref, h_ref, r_ref):
    x = x_ref[...]
    h_ref[...] = (x * g_ref[...]).astype(h_ref.dtype)
    r_ref[...] = _rms_scale(x)


def prenorm(x, g, *, tm=256):
    t, d = x.shape
    return pl.pallas_call(
        _prenorm_kernel,
        grid=(t // tm,),
        in_specs=[pl.BlockSpec((tm, d), lambda i: (i, 0)),
                  pl.BlockSpec((1, d), lambda i: (0, 0))],
        out_specs=[pl.BlockSpec((tm, d), lambda i: (i, 0)),
                   pl.BlockSpec((tm, 1), lambda i: (i, 0))],
        out_shape=[jax.ShapeDtypeStruct((t, d), BF16), jax.ShapeDtypeStruct((t, 1), F32)],
        compiler_params=_params("parallel"),
        name="prenorm",
    )(x, g.reshape(1, d))


def _ffn_up_kernel(h_ref, r_ref, wg_ref, wu_ref, o_ref):
    h = h_ref[...]
    r = r_ref[...]
    g = r * jnp.dot(h, wg_ref[...], preferred_element_type=F32)
    u = r * jnp.dot(h, wu_ref[...], preferred_element_type=F32)
    o_ref[...] = (g * jax.nn.sigmoid(g) * u).astype(o_ref.dtype)


def ffn_up(h, r, w_up, *, tm=1024, tn=512):
    t, d = h.shape
    f = w_up.shape[1] // 2
    nj = f // tn
    return pl.pallas_call(
        _ffn_up_kernel,
        grid=(t // tm, nj),
        in_specs=[pl.BlockSpec((tm, d), lambda i, j: (i, 0)),
                  pl.BlockSpec((tm, 1), lambda i, j: (i, 0)),
                  pl.BlockSpec((d, tn), lambda i, j: (0, j)),
                  pl.BlockSpec((d, tn), lambda i, j: (0, j + nj))],
        out_specs=pl.BlockSpec((tm, tn), lambda i, j: (i, j)),
        out_shape=jax.ShapeDtypeStruct((t, f), BF16),
        compiler_params=_params("parallel", "arbitrary"),
        name="ffn_up",
    )(h, r, w_up, w_up)


def _mm_norm_res_kernel(*refs, res_scale, nkk, nj, ni, rows, emit_h):
    if emit_h:
        a_ref, w_ref, x_ref, pg_ref, ng_ref, xo_ref, ho_ref, ro_ref, y_ref, ry_ref, sx_ref = refs
    else:
        a_ref, w_ref, x_ref, pg_ref, xo_ref, y_ref, ry_ref = refs
    b = pl.program_id(0)
    s = pl.program_id(1)
    tm, d = y_ref.shape
    tn = w_ref.shape[1]

    def tile_cols(e):
        return pl.ds(pl.multiple_of(e * tn, tn), tn)

    def for_row_chunks(body):
        def step(c, carry):
            body(pl.ds(pl.multiple_of(c * rows, rows), rows))
            return carry
        lax.fori_loop(0, tm // rows, step, 0)

    @pl.when((b > 0) & (s == 0))
    def _():
        def stats(r):
            ry_ref[r, :] = _rms_scale(y_ref[r, :])
            if emit_h:
                sx_ref[r, :] = jnp.zeros((rows, 1), F32)
        for_row_chunks(stats)

    def residual(r, cols):
        xn = x_ref[r, :] + res_scale * (y_ref[r, cols] * ry_ref[r, :] * pg_ref[...])
        xo_ref[r, :] = xn
        if emit_h:
            ho_ref[r, :] = (xn * ng_ref[...]).astype(BF16)
            sx_ref[r, :] += jnp.sum(xn * xn, axis=-1, keepdims=True)

    first_k = s < nj

    @pl.when((b > 0) & (b < ni) & first_k)
    def _():
        cols = tile_cols(s)
        for c in range(tm // rows):
            residual(pl.ds(c * rows, rows), cols)
        y_ref[:, cols] = jnp.dot(a_ref[...], w_ref[...], preferred_element_type=F32)

    @pl.when((b == 0) & first_k)
    def _():
        y_ref[:, tile_cols(s)] = jnp.dot(a_ref[...], w_ref[...], preferred_element_type=F32)

    @pl.when((b == ni) & first_k)
    def _():
        cols = tile_cols(s)
        for_row_chunks(lambda r: residual(r, cols))

    if emit_h:
        @pl.when((b > 0) & (s == nj - 1))
        def _():
            ro_ref[...] = lax.rsqrt(sx_ref[...] * (1.0 / d) + RMS_EPS)

    @pl.when((b < ni) & (s >= nj))
    def _():
        y_ref[:, tile_cols(s % nj)] += jnp.dot(a_ref[...], w_ref[...], preferred_element_type=F32)


def _tiled_bf16(w, tk, tn):
    kdim, n = w.shape
    return w.astype(BF16).reshape(kdim // tk, tk, n // tn, tn).transpose(0, 2, 1, 3)


def mm_norm_res(a, w, x, post_g, next_g=None, *, res_scale, tm=1024, tn=512, tkk=2048, rows=256):
    t, kdim = a.shape
    d = w.shape[1]
    ni, nkk, nj = t // tm, kdim // tkk, d // tn
    n_mm = nkk * nj
    emit_h = next_g is not None
    mm_row = lambda b: jnp.minimum(b, ni - 1)
    mm_step = lambda b, s: jnp.where(b < ni, s, n_mm - 1)
    ep_row = lambda b: jnp.maximum(b - 1, 0)
    ep_tile = lambda b, s: jnp.where(b > 0, jnp.minimum(s, nj - 1), 0)
    tile_spec = pl.BlockSpec((tm, tn), lambda b, s: (ep_row(b), ep_tile(b, s)))
    gain_spec = pl.BlockSpec((1, tn), lambda b, s: (0, ep_tile(b, s)))
    operands = [a, _tiled_bf16(w, tkk, tn), x, post_g.reshape(1, d)]
    in_specs = [pl.BlockSpec((tm, tkk), lambda b, s: (mm_row(b), mm_step(b, s) // nj)),
                pl.BlockSpec((None, None, tkk, tn), lambda b, s: (mm_step(b, s) // nj, mm_step(b, s) % nj, 0, 0)),
                tile_spec, gain_spec]
    out_shape = [jax.ShapeDtypeStruct((t, d), F32)]
    out_specs = [tile_spec]
    scratch = [pltpu.VMEM((tm, d), F32), pltpu.VMEM((tm, 1), F32)]
    if emit_h:
        operands.append(next_g.reshape(1, d))
        in_specs.append(gain_spec)
        out_shape += [jax.ShapeDtypeStruct((t, d), BF16), jax.ShapeDtypeStruct((t, 1), F32)]
        out_specs += [tile_spec, pl.BlockSpec((tm, 1), lambda b, s: (ep_row(b), 0))]
        scratch.append(pltpu.VMEM((tm, 1), F32))
    res = pl.pallas_call(
        functools.partial(_mm_norm_res_kernel, res_scale=res_scale, nkk=nkk, nj=nj, ni=ni, rows=rows,
                          emit_h=emit_h),
        grid=(ni + 1, n_mm),
        in_specs=in_specs,
        out_specs=out_specs,
        out_shape=out_shape,
        scratch_shapes=scratch,
        compiler_params=_params("arbitrary", "arbitrary"),
        name="mm_norm_res",
    )(*operands)
    return (res[0], res[1], res[2]) if emit_h else (res[0], None, None)


def _matmul_kernel(a_ref, r_ref, w_ref, o_ref):
    acc = jnp.dot(a_ref[...], w_ref[...], preferred_element_type=F32)
    o_ref[...] = (r_ref[...] * acc).astype(o_ref.dtype)


def matmul(a, r, w, *, tm=1024, tn=1280):
    t, kdim = a.shape
    n = w.shape[1]
    return pl.pallas_call(
        _matmul_kernel,
        grid=(t // tm, n // tn),
        in_specs=[pl.BlockSpec((tm, kdim), lambda i, j: (i, 0)),
                  pl.BlockSpec((tm, 1), lambda i, j: (i, 0)),
                  pl.BlockSpec((kdim, tn), lambda i, j: (0, j))],
        out_specs=pl.BlockSpec((tm, tn), lambda i, j: (i, j)),
        out_shape=jax.ShapeDtypeStruct((t, n), BF16),
        compiler_params=_params("parallel", "arbitrary"),
        name="in_proj",
    )(a, r, w)


def _gelu_exact(x):
    return 0.5 * x * (1.0 + lax.erf(x * (1.0 / math.sqrt(2.0))))


def _gmlp_kernel(z_ref, lng_ref, lnb_ref, ws_ref, bs_ref, o_ref, *, chunks):
    bw = BRANCH_WIDTH
    row = lax.broadcasted_iota(jnp.int32, (GMLP_CHUNK, GMLP_CHUNK), 0)
    col = lax.broadcasted_iota(jnp.int32, (GMLP_CHUNK, GMLP_CHUNK), 1)
    causal = row >= col
    for c in range(chunks):
        r = slice(c * GMLP_CHUNK, (c + 1) * GMLP_CHUNK)
        u = _gelu_exact(z_ref[r, :bw].astype(F32))
        v = _gelu_exact(z_ref[r, bw:].astype(F32))
        mu = jnp.mean(v, axis=-1, keepdims=True)
        vc = v - mu
        var = jnp.mean(vc * vc, axis=-1, keepdims=True)
        vn = (vc * lax.rsqrt(var + LN_EPS) * lng_ref[...] + lnb_ref[...]).astype(BF16)
        for g in range(GMLP_GROUPS):
            gs = slice(g * GMLP_GROUP_DIM, (g + 1) * GMLP_GROUP_DIM)
            w = jnp.where(causal, ws_ref[g], 0.0).astype(BF16)
            mixed = jnp.dot(w, vn[:, gs], preferred_element_type=F32) + bs_ref[:, g:g + 1]
            o_ref[r, gs] = (u[:, gs] * mixed).astype(o_ref.dtype)


def gmlp(proj, ln_g, ln_b, w_s, b_s, *, chunks=4):
    t = proj.shape[0]
    bw = BRANCH_WIDTH
    tm = chunks * GMLP_CHUNK
    return pl.pallas_call(
        functools.partial(_gmlp_kernel, chunks=chunks),
        grid=(t // tm,),
        in_specs=[pl.BlockSpec((tm, 2 * bw), lambda i: (i, 0)),
                  pl.BlockSpec((1, bw), lambda i: (0, 0)),
                  pl.BlockSpec((1, bw), lambda i: (0, 0)),
                  pl.BlockSpec((GMLP_GROUPS, GMLP_CHUNK, GMLP_CHUNK), lambda i: (0, 0, 0)),
                  pl.BlockSpec((GMLP_CHUNK, GMLP_GROUPS), lambda i: (0, 0))],
        out_specs=pl.BlockSpec((tm, bw), lambda i: (i, 0)),
        out_shape=jax.ShapeDtypeStruct((t, bw), BF16),
        compiler_params=_params("parallel"),
        name="gmlp",
    )(proj, ln_g.reshape(1, bw), ln_b.reshape(1, bw), w_s, b_s.T)


MOBA_MAX_BLOCKS = 32
MOBA_CHUNK_BLOCKS = 4
MOBA_Q_TILE_BLOCKS = 2
MOBA_PREP_ROWS = 512
LOG2_E = math.log2(math.e)
FEAT_SEL, FEAT_ONE, FEAT_BLK, FEAT_OFF, FEAT_END = 0, 32, 35, 38, 41


def _moba_key_features(seq):
    pos = jnp.arange(seq, dtype=jnp.int32)[:, None]
    kb, off = pos // MOBA_BLOCK, pos % MOBA_BLOCK
    lane = jnp.arange(LANES, dtype=jnp.int32)[None, :]
    feat = jnp.where(lane < FEAT_ONE, (lane == kb).astype(F32),
                     jnp.where(lane < FEAT_BLK, 1.0,
                               jnp.where(lane < FEAT_OFF, (kb * MOBA_BLOCK).astype(F32),
                                         jnp.where(lane < FEAT_END, off.astype(F32), 0.0))))
    return feat.astype(BF16)


def _split3(x):
    hi = x.astype(BF16).astype(F32)
    mid = (x - hi).astype(BF16).astype(F32)
    return hi, mid, x - hi - mid


def _moba_prepare(slope, q_ref, k_ref, kfeat_ref, kaug_ref, qaug_ref, kmh_ref, kml_ref, *, nb):
    blk = MOBA_BLOCK
    dh = MOBA_HEAD_DIM
    seq = q_ref.shape[0]
    prep = MOBA_PREP_ROWS
    nbp = MOBA_MAX_BLOCKS
    scale = dh ** -0.5 * LOG2_E
    slope = slope * LOG2_E

    kmh_ref[...] = jnp.zeros_like(kmh_ref)
    kml_ref[...] = jnp.zeros_like(kml_ref)
    for n in range(nb):
        rows = slice(n * blk, (n + 1) * blk)
        kn = k_ref[rows, :]
        kaug_ref[rows, :dh] = kn
        kaug_ref[rows, dh:] = kfeat_ref[rows, :]
        km = jnp.mean(kn.astype(F32), axis=0, keepdims=True)
        hi = km.astype(BF16)
        kmh_ref[n:n + 1, :] = hi
        kml_ref[n:n + 1, :] = (km - hi.astype(F32)).astype(BF16)

    blk_id = lax.broadcasted_iota(jnp.int32, (nbp, prep), 0)
    q_off = lax.broadcasted_iota(jnp.int32, (nbp, prep), 1)
    lane = lax.broadcasted_iota(jnp.int32, (prep, LANES), 1)
    row = lax.broadcasted_iota(jnp.int32, (prep, LANES), 0)
    part = jnp.where(lane < FEAT_BLK, lane - FEAT_ONE, jnp.where(lane < FEAT_OFF, lane - FEAT_BLK, lane - FEAT_OFF))
    pick3 = lambda parts: jnp.where(part == 0, parts[0], jnp.where(part == 1, parts[1], parts[2]))
    slope_feat = pick3(_split3(jnp.full((prep, LANES), slope, F32)))

    def rows_chunk(c, carry):
        row0 = pl.multiple_of(c * prep, prep)
        rows = pl.ds(row0, prep)
        q = q_ref[rows, :]
        gate = _dot_nt(kmh_ref[:nbp, :], q) + _dot_nt(kml_ref[:nbp, :], q)
        q_blk = (row0 + q_off) // blk
        gate = jnp.where(blk_id < q_blk, gate, -jnp.inf)
        sel_bias = jnp.where(blk_id == q_blk, 0.0, MASKED)
        for _ in range(MOBA_TOPK):
            best = jnp.max(gate, axis=0, keepdims=True)
            first = jnp.min(jnp.where(gate == best, blk_id, nbp), axis=0, keepdims=True)
            pick = (blk_id == first) & (best > -jnp.inf)
            sel_bias = jnp.where(pick, 0.0, sel_bias)
            gate = jnp.where(pick, -jnp.inf, gate)
        sel_rows = jnp.concatenate([sel_bias, jnp.zeros((LANES - nbp, prep), F32)], axis=0).T

        t_pos = (row0 + row).astype(F32)
        feat = jnp.where(lane < FEAT_ONE, sel_rows,
                         jnp.where(lane < FEAT_BLK, pick3(_split3(-slope * t_pos)),
                                   jnp.where(lane < FEAT_END, slope_feat, 0.0)))
        qaug_ref[rows, :dh] = (q.astype(F32) * scale).astype(BF16)
        qaug_ref[rows, dh:] = feat.astype(BF16)
        return carry

    lax.fori_loop(0, seq // prep, rows_chunk, 0)


def _moba_kernel(slopes_ref, q_ref, k_ref, v_ref, kfeat_ref, o_ref, kaug_ref, qaug_ref, kmh_ref, kml_ref, s_ref,
                 *, nb):
    blk = MOBA_BLOCK
    dh = MOBA_HEAD_DIM
    qt = MOBA_Q_TILE_BLOCKS * blk
    cb = MOBA_CHUNK_BLOCKS
    span = cb * blk
    h = pl.program_id(1)
    ti = pl.program_id(2)
    first_blk = ti * MOBA_Q_TILE_BLOCKS

    @pl.when(ti == 0)
    def _():
        _moba_prepare(slopes_ref[h], q_ref, k_ref, kfeat_ref, kaug_ref, qaug_ref, kmh_ref, kml_ref, nb=nb)

    q_aug = qaug_ref[pl.ds(pl.multiple_of(ti * qt, qt), qt), :]

    def chunk_rows(p):
        return pl.ds(pl.multiple_of(p * span, span), span)

    def scores(p):
        return _dot_nt(q_aug, kaug_ref[chunk_rows(p), :])

    def lanes_max(m, s):
        for c in range(0, s.shape[1], LANES):
            m = jnp.maximum(m, s[:, c:c + LANES])
        return m

    def pass1(p, mvec):
        s = scores(p)
        s_ref[p] = s
        return lanes_max(mvec, s)

    n_past = first_blk // cb
    mvec = lax.fori_loop(0, n_past, pass1, jnp.full((qt, LANES), -jnp.inf, F32))
    r_idx = lax.broadcasted_iota(jnp.int32, (blk, blk), 0)
    c_idx = lax.broadcasted_iota(jnp.int32, (blk, blk), 1)
    causal_bias = jnp.where(r_idx >= c_idx, 0.0, MASKED)
    own = first_blk % cb
    bias = jnp.concatenate(
        [jnp.concatenate([causal_bias * (own + r == c).astype(F32) for c in range(cb)], axis=1)
         for r in range(MOBA_Q_TILE_BLOCKS)], axis=0)
    s = scores(n_past) + bias
    s_ref[n_past] = s
    m = jnp.max(lanes_max(mvec, s), axis=-1, keepdims=True)

    def pass2(p, carry):
        l, acc = carry
        e = jnp.exp2(s_ref[p] - m)
        l = l + jnp.sum(e, axis=-1, keepdims=True)
        acc = acc + jnp.dot(e.astype(BF16), v_ref[chunk_rows(p), :], preferred_element_type=F32)
        return l, acc

    l, acc = lax.fori_loop(0, n_past + 1, pass2,
                           (jnp.zeros((qt, 1), F32), jnp.zeros((qt, dh), F32)))
    o_ref[...] = (acc / l).astype(o_ref.dtype)


def moba(proj, slopes, *, batch, seq):
    blk = MOBA_BLOCK
    nb = seq // blk
    qt = MOBA_Q_TILE_BLOCKS * blk
    nt = seq // qt
    assert nb % MOBA_CHUNK_BLOCKS == 0 and nb <= MOBA_MAX_BLOCKS and seq % MOBA_PREP_ROWS == 0
    assert MOBA_CHUNK_BLOCKS % MOBA_Q_TILE_BLOCKS == 0
    dh = MOBA_HEAD_DIM
    cq, ck, cv = COL_MOBA_Q // dh, COL_MOBA_K // dh, COL_MOBA_V // dh
    return pl.pallas_call(
        functools.partial(_moba_kernel, nb=nb),
        grid_spec=pltpu.PrefetchScalarGridSpec(
            num_scalar_prefetch=1,
            grid=(batch, MOBA_HEADS, nt),
            in_specs=[pl.BlockSpec((seq, dh), lambda b, h, i, s: (b, cq + h)),
                      pl.BlockSpec((seq, dh), lambda b, h, i, s: (b, ck + h)),
                      pl.BlockSpec((seq, dh), lambda b, h, i, s: (b, cv + h)),
                      pl.BlockSpec((seq, LANES), lambda b, h, i, s: (0, 0))],
            out_specs=pl.BlockSpec((qt, dh), lambda b, h, i, s: (b * nt + i, h)),
            scratch_shapes=[pltpu.VMEM((seq, dh + LANES), BF16), pltpu.VMEM((seq, dh + LANES), BF16),
                            pltpu.VMEM((LANES, dh), BF16), pltpu.VMEM((LANES, dh), BF16),
                            pltpu.VMEM((nb // MOBA_CHUNK_BLOCKS, qt, MOBA_CHUNK_BLOCKS * blk), F32)],
        ),
        out_shape=jax.ShapeDtypeStruct((batch * seq, BRANCH_WIDTH), BF16),
        compiler_params=_params("parallel", "parallel", "arbitrary"),
        name="moba",
    )(slopes, proj, proj, proj, _moba_key_features(seq))


def _swa_kernel(slopes_ref, sinks_ref, q_ref, kp_ref, kc_ref, vp_ref, vc_ref, o_ref):
    w = SWA_WINDOW
    dh = SWA_HEAD_DIM
    n = pl.program_id(1)
    scale = dh ** -0.5
    lane = lax.broadcasted_iota(jnp.int32, (w, LANES), 1)
    low = lane < dh
    t_idx = lax.broadcasted_iota(jnp.int32, (w, w), 0)
    s_idx = lax.broadcasted_iota(jnp.int32, (w, w), 1)
    dist_cur = (t_idx - s_idx).astype(F32)
    dist_prev = dist_cur + float(w)
    ok_cur = s_idx <= t_idx
    ok_prev = (s_idx > t_idx) & (n > 0)

    def halves(ref):
        x = ref[...].astype(F32)
        xr = pltpu.roll(x, dh, axis=1)
        z = jnp.zeros_like(x)
        lo = (jnp.where(low, x, z).astype(BF16), jnp.where(low, z, xr).astype(BF16))
        hi = (jnp.where(low, xr, z).astype(BF16), jnp.where(low, z, x).astype(BF16))
        return lo, hi

    kp, kc, vp, vc = halves(kp_ref), halves(kc_ref), halves(vp_ref), halves(vc_ref)
    pairs = SWA_Q_HEADS // 2
    group = SWA_Q_HEADS // SWA_KV_HEADS
    for pr in range(pairs):
        q = q_ref[:, pr * LANES:(pr + 1) * LANES]
        kh = (2 * pr) // group
        out = None
        for half in range(2):
            hq = 2 * pr + half
            slope = slopes_ref[hq]
            sink = sinks_ref[hq]
            s_p = _dot_nt(q, kp[kh][half]) * scale
            s_c = _dot_nt(q, kc[kh][half]) * scale
            s_p = jnp.where(ok_prev, s_p - slope * dist_prev, MASKED)
            s_c = jnp.where(ok_cur, s_c - slope * dist_cur, MASKED)
            m = jnp.maximum(jnp.max(s_p, axis=-1, keepdims=True), jnp.max(s_c, axis=-1, keepdims=True))
            m = jnp.maximum(m, sink)
            e_p = jnp.exp(s_p - m)
            e_c = jnp.exp(s_c - m)
            denom = (jnp.sum(e_p, axis=-1, keepdims=True) + jnp.sum(e_c, axis=-1, keepdims=True)
                     + jnp.exp(sink - m))
            inv = 1.0 / denom
            o = (jnp.dot((e_p * inv).astype(BF16), vp[kh][half], preferred_element_type=F32)
                 + jnp.dot((e_c * inv).astype(BF16), vc[kh][half], preferred_element_type=F32))
            out = o if out is None else out + o
        o_ref[:, pr * LANES:(pr + 1) * LANES] = out.astype(o_ref.dtype)


def swa(proj, slopes, sinks, *, batch, seq):
    w = SWA_WINDOW
    nblk = seq // w
    bw = BRANCH_WIDTH
    cq = COL_SWA_Q // bw
    ck = COL_SWA_K // SWA_KV_WIDTH
    cv = COL_SWA_V // SWA_KV_WIDTH
    prev = lambda b, n, *_: (b * nblk + jnp.maximum(n - 1, 0))
    cur = lambda b, n, *_: (b * nblk + n)
    return pl.pallas_call(
        _swa_kernel,
        grid_spec=pltpu.PrefetchScalarGridSpec(
            num_scalar_prefetch=2,
            grid=(batch, nblk),
            in_specs=[pl.BlockSpec((w, bw), lambda b, n, *_: (cur(b, n), cq)),
                      pl.BlockSpec((w, SWA_KV_WIDTH), lambda b, n, *_: (prev(b, n), ck)),
                      pl.BlockSpec((w, SWA_KV_WIDTH), lambda b, n, *_: (cur(b, n), ck)),
                      pl.BlockSpec((w, SWA_KV_WIDTH), lambda b, n, *_: (prev(b, n), cv)),
                      pl.BlockSpec((w, SWA_KV_WIDTH), lambda b, n, *_: (cur(b, n), cv))],
            out_specs=pl.BlockSpec((w, bw), lambda b, n, *_: (cur(b, n), 0)),
        ),
        out_shape=jax.ShapeDtypeStruct((batch * seq, bw), BF16),
        compiler_params=_params("parallel", "arbitrary"),
        name="swa",
    )(slopes, sinks, proj, proj, proj, proj, proj)


def _merge_kernel(h_ref, r_ref, wg_ref0, wg_ref1, wg_ref2, bg_ref, ya_ref, yb_ref, yc_ref, wb_ref, o_ref):
    h = h_ref[...]
    r = r_ref[...]
    merged = None
    for n, (wg_ref, y_ref) in enumerate(((wg_ref0, ya_ref), (wg_ref1, yb_ref), (wg_ref2, yc_ref))):
        logits = r * jnp.dot(h, wg_ref[...], preferred_element_type=F32) + bg_ref[n]
        branch = jnp.dot(y_ref[...], wb_ref[n], preferred_element_type=F32)
        term = jax.nn.sigmoid(logits) * branch
        merged = term if merged is None else merged + term
    o_ref[...] = merged.astype(o_ref.dtype)


def merge(h, r, w_gate, b_gate, y_a, y_b, y_c, w_branch, *, tm=1024, tn=256):
    t, d = h.shape
    bw = BRANCH_WIDTH
    nj = d // tn
    gate_spec = lambda n: pl.BlockSpec((d, tn), lambda i, j: (0, n * nj + j))
    y_spec = pl.BlockSpec((tm, bw), lambda i, j: (i, 0))
    return pl.pallas_call(
        _merge_kernel,
        grid=(t // tm, nj),
        in_specs=[pl.BlockSpec((tm, d), lambda i, j: (i, 0)),
                  pl.BlockSpec((tm, 1), lambda i, j: (i, 0)),
                  gate_spec(0), gate_spec(1), gate_spec(2),
                  pl.BlockSpec((N_BRANCHES, 1, tn), lambda i, j: (0, 0, j)),
                  y_spec, y_spec, y_spec,
                  pl.BlockSpec((N_BRANCHES, bw, tn), lambda i, j: (0, 0, j))],
        out_specs=pl.BlockSpec((tm, tn), lambda i, j: (i, j)),
        out_shape=jax.ShapeDtypeStruct((t, d), BF16),
        compiler_params=_params("parallel", "arbitrary"),
        name="merge",
    )(h, r, w_gate, w_gate, w_gate, b_gate.reshape(N_BRANCHES, 1, d), y_a, y_b, y_c, w_branch)


def _alibi_slopes():
    i = jnp.arange(1, N_ALIBI_HEADS + 1, dtype=F32)
    s = jnp.exp2(-8.0 * i / N_ALIBI_HEADS)
    return s[:SWA_Q_HEADS], s[SWA_Q_HEADS:]


def kernel(x, ffn1_pre_g, ffn1_w_up, ffn1_w_down, ffn1_post_g, mix_pre_g, w_in, gmlp_ln_g, gmlp_ln_b,
           gmlp_w_s, gmlp_b_s, swa_sinks, w_gate, b_gate, w_branch, w_out, mix_post_g, ffn2_pre_g,
           ffn2_w_up, ffn2_w_down, ffn2_post_g):
    batch, seq, d = x.shape
    depth = ffn1_pre_g.shape[0]
    swa_slopes, moba_slopes = _alibi_slopes()
    xf = x.reshape(batch * seq, d)
    h, r = prenorm(xf, ffn1_pre_g[0])
    for i in range(depth):
        act = ffn_up(h, r, ffn1_w_up[i].astype(BF16))
        xf, h, r = mm_norm_res(act, ffn1_w_down[i], xf, ffn1_post_g[i], mix_pre_g[i], res_scale=0.5)

        proj = matmul(h, r, w_in[i].astype(BF16))
        y_a = gmlp(proj, gmlp_ln_g[i], gmlp_ln_b[i], gmlp_w_s[i], gmlp_b_s[i])
        y_b = moba(proj, moba_slopes, batch=batch, seq=seq)
        y_c = swa(proj, swa_slopes, swa_sinks[i], batch=batch, seq=seq)
        merged = merge(h, r, w_gate[i].astype(BF16), b_gate[i], y_a, y_b, y_c, w_branch[i].astype(BF16))
        xf, h, r = mm_norm_res(merged, w_out[i], xf, mix_post_g[i], ffn2_pre_g[i], res_scale=1.0)

        act = ffn_up(h, r, ffn2_w_up[i].astype(BF16))
        next_g = ffn1_pre_g[i + 1] if i + 1 < depth else None
        xf, h, r = mm_norm_res(act, ffn2_w_down[i], xf, ffn2_post_g[i], next_g, res_scale=0.5)
    return xf.reshape(batch, seq, d)
```

```python
import functools
import math
from typing import NamedTuple, Optional, Tuple

import jax
import jax.numpy as jnp
from jax import lax
from jax.experimental import pallas as pl
from jax.experimental.pallas import tpu as pltpu

D_MODEL = 4096
DEPTH = 2
BRANCH_WIDTH = D_MODEL // 4
N_BRANCHES = 3
GMLP_CHUNK = 128
GMLP_GROUP_DIM = 128
GMLP_GROUPS = BRANCH_WIDTH // GMLP_GROUP_DIM
MOBA_HEAD_DIM = 128
MOBA_HEADS = BRANCH_WIDTH // MOBA_HEAD_DIM
MOBA_BLOCK = 256
MOBA_TOPK = 3
SWA_HEAD_DIM = 64
SWA_Q_HEADS = BRANCH_WIDTH // SWA_HEAD_DIM
SWA_KV_HEADS = SWA_Q_HEADS // 8
SWA_WINDOW = 128
SWA_KV_WIDTH = SWA_KV_HEADS * SWA_HEAD_DIM
IN_COLS = 6 * BRANCH_WIDTH + 2 * SWA_KV_WIDTH
D_FF = 2 * D_MODEL
N_ALIBI_HEADS = SWA_Q_HEADS + MOBA_HEADS
RMS_EPS = 1e-6
LN_EPS = 1e-5

LANES = 128
VMEM_LIMIT = 60 * 1024 * 1024
MM_TILE = (2048, 512)
MASKED = -1e30

F32 = jnp.float32
BF16 = jnp.bfloat16

COL_GMLP = 0
COL_MOBA_Q = 2 * BRANCH_WIDTH
COL_MOBA_K = 3 * BRANCH_WIDTH
COL_MOBA_V = 4 * BRANCH_WIDTH
COL_SWA_Q = 5 * BRANCH_WIDTH
COL_SWA_K = 6 * BRANCH_WIDTH
COL_SWA_V = 6 * BRANCH_WIDTH + SWA_KV_WIDTH


def _params(*semantics):
    return pltpu.CompilerParams(dimension_semantics=semantics, vmem_limit_bytes=VMEM_LIMIT)


def _rms_scale(x):
    return lax.rsqrt(jnp.mean(x * x, axis=-1, keepdims=True) + RMS_EPS)


def _dot_nt(a, b):
    return lax.dot_general(a, b, (((1,), (1,)), ((), ())), preferred_element_type=F32)


def _prenorm_kernel(x_ref, g_ref, h_ref, r_ref):
    x = x_ref[...]
    h_ref[...] = (x * g_ref[...]).astype(h_ref.dtype)
    r_ref[...] = _rms_scale(x)


def prenorm(x, g, *, tm=256):
    t, d = x.shape
    return pl.pallas_call(
        _prenorm_kernel,
        grid=(t // tm,),
        in_specs=[pl.BlockSpec((tm, d), lambda i: (i, 0)),
                  pl.BlockSpec((1, d), lambda i: (0, 0))],
        out_specs=[pl.BlockSpec((tm, d), lambda i: (i, 0)),
                   pl.BlockSpec((tm, 1), lambda i: (i, 0))],
        out_shape=[jax.ShapeDtypeStruct((t, d), BF16), jax.ShapeDtypeStruct((t, 1), F32)],
        compiler_params=_params("parallel"),
        name="prenorm",
    )(x, g.reshape(1, d))


class _SideCast(NamedTuple):
    src: jax.Array
    rows: int
    tile: Optional[Tuple[int, int]]


def _side_cast_plan(casts, steps, step_of):
    in_specs, out_specs, out_shapes = [], [], []
    for c in casts:
        nrow, ncol = c.src.shape
        nblk = nrow // c.rows
        assert nrow % c.rows == 0 and steps % nblk == 0
        blk_of = lambda *ids, hold=steps // nblk: step_of(*ids) // hold
        in_specs.append(pl.BlockSpec((c.rows, ncol), lambda *ids, f=blk_of: (f(*ids), 0)))
        if c.tile is None:
            out_specs.append(pl.BlockSpec((c.rows, ncol), lambda *ids, f=blk_of: (f(*ids), 0)))
            out_shapes.append(jax.ShapeDtypeStruct((nrow, ncol), BF16))
        else:
            tk, tn = c.tile
            per = tk // c.rows
            assert tk % c.rows == 0 and nrow % tk == 0 and ncol % tn == 0
            out_specs.append(pl.BlockSpec((None, ncol // tn, c.rows, tn),
                                          lambda *ids, f=blk_of, per=per: (f(*ids) // per, 0, f(*ids) % per, 0)))
            out_shapes.append(jax.ShapeDtypeStruct((nrow // tk, ncol // tn, tk, tn), BF16))
    return in_specs, out_specs, out_shapes


def _side_cast_run(tiles, src_refs, dst_refs):
    for tile, src, dst in zip(tiles, src_refs, dst_refs, strict=True):
        if tile is None:
            dst[...] = src[...].astype(BF16)
        else:
            tn = tile[1]
            for j in range(dst.shape[0]):
                dst[j] = src[:, j * tn:(j + 1) * tn].astype(BF16)


def _tiled_bf16(w, tk, tn):
    kdim, n = w.shape
    return w.astype(BF16).reshape(kdim // tk, tk, n // tn, tn).transpose(0, 2, 1, 3)


def _ffn_up_kernel(*refs, tiles):
    n = len(tiles)
    h_ref, r_ref, wg_ref, wu_ref = refs[:4]
    o_ref = refs[4 + n]
    h = h_ref[...]
    r = r_ref[...]
    g = r * jnp.dot(h, wg_ref[...], preferred_element_type=F32)
    u = r * jnp.dot(h, wu_ref[...], preferred_element_type=F32)
    o_ref[...] = (g * jax.nn.sigmoid(g) * u).astype(o_ref.dtype)
    _side_cast_run(tiles, refs[4:4 + n], refs[5 + n:])


def ffn_up(h, r, w_up, *, casts=(), tm=1024, tn=512):
    t, d = h.shape
    f = w_up.shape[1] // 2
    ni, nj = t // tm, f // tn
    c_in, c_out, c_shapes = _side_cast_plan(casts, ni * nj, lambda i, j: i * nj + j)
    return pl.pallas_call(
        functools.partial(_ffn_up_kernel, tiles=tuple(c.tile for c in casts)),
        grid=(ni, nj),
        in_specs=[pl.BlockSpec((tm, d), lambda i, j: (i, 0)),
                  pl.BlockSpec((tm, 1), lambda i, j: (i, 0)),
                  pl.BlockSpec((d, tn), lambda i, j: (0, j)),
                  pl.BlockSpec((d, tn), lambda i, j: (0, j + nj))] + c_in,
        out_specs=[pl.BlockSpec((tm, tn), lambda i, j: (i, j))] + c_out,
        out_shape=[jax.ShapeDtypeStruct((t, f), BF16)] + c_shapes,
        compiler_params=_params("arbitrary", "arbitrary"),
        name="ffn_up",
    )(h, r, w_up, w_up, *[c.src for c in casts])


def _mm_norm_res_kernel(*refs, res_scale, nkk, nj, ni, rows, emit_h):
    if emit_h:
        a_ref, w_ref, x_ref, pg_ref, ng_ref, xo_ref, ho_ref, ro_ref, y_ref, ry_ref, sx_ref = refs
    else:
        a_ref, w_ref, x_ref, pg_ref, xo_ref, y_ref, ry_ref = refs
    b = pl.program_id(0)
    s = pl.program_id(1)
    tm, d = y_ref.shape
    tn = w_ref.shape[1]

    def tile_cols(e):
        return pl.ds(pl.multiple_of(e * tn, tn), tn)

    def for_row_chunks(body):
        def step(c, carry):
            body(pl.ds(pl.multiple_of(c * rows, rows), rows))
            return carry
        lax.fori_loop(0, tm // rows, step, 0)

    @pl.when((b > 0) & (s == 0))
    def _():
        def stats(r):
            ry_ref[r, :] = _rms_scale(y_ref[r, :])
            if emit_h:
                sx_ref[r, :] = jnp.zeros((rows, 1), F32)
        for_row_chunks(stats)

    def residual(r, cols):
        xn = x_ref[r, :] + res_scale * (y_ref[r, cols] * ry_ref[r, :] * pg_ref[...])
        xo_ref[r, :] = xn
        if emit_h:
            ho_ref[r, :] = (xn * ng_ref[...]).astype(BF16)
            sx_ref[r, :] += jnp.sum(xn * xn, axis=-1, keepdims=True)

    first_k = s < nj

    @pl.when((b > 0) & (b < ni) & first_k)
    def _():
        cols = tile_cols(s)
        for c in range(tm // rows):
            residual(pl.ds(c * rows, rows), cols)
        y_ref[:, cols] = jnp.dot(a_ref[...], w_ref[...], preferred_element_type=F32)

    @pl.when((b == 0) & first_k)
    def _():
        y_ref[:, tile_cols(s)] = jnp.dot(a_ref[...], w_ref[...], preferred_element_type=F32)

    @pl.when((b == ni) & first_k)
    def _():
        cols = tile_cols(s)
        for_row_chunks(lambda r: residual(r, cols))

    if emit_h:
        @pl.when((b > 0) & (s == nj - 1))
        def _():
            ro_ref[...] = lax.rsqrt(sx_ref[...] * (1.0 / d) + RMS_EPS)

    @pl.when((b < ni) & (s >= nj))
    def _():
        y_ref[:, tile_cols(s % nj)] += jnp.dot(a_ref[...], w_ref[...], preferred_element_type=F32)


def mm_norm_res(a, w_tiled, x, post_g, next_g=None, *, res_scale, tm=1024, rows=256):
    t, kdim = a.shape
    nkk, nj, tkk, tn = w_tiled.shape
    d = nj * tn
    assert kdim == nkk * tkk
    ni = t // tm
    n_mm = nkk * nj
    emit_h = next_g is not None
    mm_row = lambda b: jnp.minimum(b, ni - 1)
    mm_step = lambda b, s: jnp.where(b < ni, s, n_mm - 1)
    ep_row = lambda b: jnp.maximum(b - 1, 0)
    ep_tile = lambda b, s: jnp.where(b > 0, jnp.minimum(s, nj - 1), 0)
    tile_spec = pl.BlockSpec((tm, tn), lambda b, s: (ep_row(b), ep_tile(b, s)))
    gain_spec = pl.BlockSpec((1, tn), lambda b, s: (0, ep_tile(b, s)))
    operands = [a, w_tiled, x, post_g.reshape(1, d)]
    in_specs = [pl.BlockSpec((tm, tkk), lambda b, s: (mm_row(b), mm_step(b, s) // nj)),
                pl.BlockSpec((None, None, tkk, tn), lambda b, s: (mm_step(b, s) // nj, mm_step(b, s) % nj, 0, 0)),
                tile_spec, gain_spec]
    out_shape = [jax.ShapeDtypeStruct((t, d), F32)]
    out_specs = [tile_spec]
    scratch = [pltpu.VMEM((tm, d), F32), pltpu.VMEM((tm, 1), F32)]
    if emit_h:
        operands.append(next_g.reshape(1, d))
        in_specs.append(gain_spec)
        out_shape += [jax.ShapeDtypeStruct((t, d), BF16), jax.ShapeDtypeStruct((t, 1), F32)]
        out_specs += [tile_spec, pl.BlockSpec((tm, 1), lambda b, s: (ep_row(b), 0))]
        scratch.append(pltpu.VMEM((tm, 1), F32))
    res = pl.pallas_call(
        functools.partial(_mm_norm_res_kernel, res_scale=res_scale, nkk=nkk, nj=nj, ni=ni, rows=rows,
                          emit_h=emit_h),
        grid=(ni + 1, n_mm),
        in_specs=in_specs,
        out_specs=out_specs,
        out_shape=out_shape,
        scratch_shapes=scratch,
        compiler_params=_params("arbitrary", "arbitrary"),
        name="mm_norm_res",
    )(*operands)
    return (res[0], res[1], res[2]) if emit_h else (res[0], None, None)


def _matmul_kernel(a_ref, r_ref, w_ref, o_ref):
    acc = jnp.dot(a_ref[...], w_ref[...], preferred_element_type=F32)
    o_ref[...] = (r_ref[...] * acc).astype(o_ref.dtype)


def matmul(a, r, w, *, tm=1024, tn=1280):
    t, kdim = a.shape
    n = w.shape[1]
    return pl.pallas_call(
        _matmul_kernel,
        grid=(t // tm, n // tn),
        in_specs=[pl.BlockSpec((tm, kdim), lambda i, j: (i, 0)),
                  pl.BlockSpec((tm, 1), lambda i, j: (i, 0)),
                  pl.BlockSpec((kdim, tn), lambda i, j: (0, j))],
        out_specs=pl.BlockSpec((tm, tn), lambda i, j: (i, j)),
        out_shape=jax.ShapeDtypeStruct((t, n), BF16),
        compiler_params=_params("parallel", "arbitrary"),
        name="in_proj",
    )(a, r, w)


def _gelu_exact(x):
    return 0.5 * x * (1.0 + lax.erf(x * (1.0 / math.sqrt(2.0))))


def _gmlp_kernel(z_ref, lng_ref, lnb_ref, ws_ref, bs_ref, o_ref, *, chunks):
    bw = BRANCH_WIDTH
    row = lax.broadcasted_iota(jnp.int32, (GMLP_CHUNK, GMLP_CHUNK), 0)
    col = lax.broadcasted_iota(jnp.int32, (GMLP_CHUNK, GMLP_CHUNK), 1)
    causal = row >= col
    for c in range(chunks):
        r = slice(c * GMLP_CHUNK, (c + 1) * GMLP_CHUNK)
        u = _gelu_exact(z_ref[r, :bw].astype(F32))
        v = _gelu_exact(z_ref[r, bw:].astype(F32))
        mu = jnp.mean(v, axis=-1, keepdims=True)
        vc = v - mu
        var = jnp.mean(vc * vc, axis=-1, keepdims=True)
        vn = (vc * lax.rsqrt(var + LN_EPS) * lng_ref[...] + lnb_ref[...]).astype(BF16)
        for g in range(GMLP_GROUPS):
            gs = slice(g * GMLP_GROUP_DIM, (g + 1) * GMLP_GROUP_DIM)
            w = jnp.where(causal, ws_ref[g], 0.0).astype(BF16)
            mixed = jnp.dot(w, vn[:, gs], preferred_element_type=F32) + bs_ref[:, g:g + 1]
            o_ref[r, gs] = (u[:, gs] * mixed).astype(o_ref.dtype)


def gmlp(proj, ln_g, ln_b, w_s, b_s, *, chunks=4):
    t = proj.shape[0]
    bw = BRANCH_WIDTH
    tm = chunks * GMLP_CHUNK
    return pl.pallas_call(
        functools.partial(_gmlp_kernel, chunks=chunks),
        grid=(t // tm,),
        in_specs=[pl.BlockSpec((tm, 2 * bw), lambda i: (i, 0)),
                  pl.BlockSpec((1, bw), lambda i: (0, 0)),
                  pl.BlockSpec((1, bw), lambda i: (0, 0)),
                  pl.BlockSpec((GMLP_GROUPS, GMLP_CHUNK, GMLP_CHUNK), lambda i: (0, 0, 0)),
                  pl.BlockSpec((GMLP_CHUNK, GMLP_GROUPS), lambda i: (0, 0))],
        out_specs=pl.BlockSpec((tm, bw), lambda i: (i, 0)),
        out_shape=jax.ShapeDtypeStruct((t, bw), BF16),
        compiler_params=_params("parallel"),
        name="gmlp",
    )(proj, ln_g.reshape(1, bw), ln_b.reshape(1, bw), w_s, b_s.T)


MOBA_MAX_BLOCKS = 32
MOBA_CHUNK_BLOCKS = 4
MOBA_Q_TILE_BLOCKS = 2
MOBA_PREP_ROWS = 512
LOG2_E = math.log2(math.e)
FEAT_SEL, FEAT_ONE, FEAT_BLK, FEAT_OFF, FEAT_END = 0, 32, 35, 38, 41


def _moba_key_features(seq):
    pos = jnp.arange(seq, dtype=jnp.int32)[:, None]
    kb, off = pos // MOBA_BLOCK, pos % MOBA_BLOCK
    lane = jnp.arange(LANES, dtype=jnp.int32)[None, :]
    feat = jnp.where(lane < FEAT_ONE, (lane == kb).astype(F32),
                     jnp.where(lane < FEAT_BLK, 1.0,
                               jnp.where(lane < FEAT_OFF, (kb * MOBA_BLOCK).astype(F32),
                                         jnp.where(lane < FEAT_END, off.astype(F32), 0.0))))
    return feat.astype(BF16)


def _split3(x):
    hi = x.astype(BF16).astype(F32)
    mid = (x - hi).astype(BF16).astype(F32)
    return hi, mid, x - hi - mid


def _moba_prepare(slope, q_ref, k_ref, kfeat_ref, kaug_ref, qaug_ref, kmh_ref, kml_ref, *, nb):
    blk = MOBA_BLOCK
    dh = MOBA_HEAD_DIM
    seq = q_ref.shape[0]
    prep = MOBA_PREP_ROWS
    nbp = MOBA_MAX_BLOCKS
    scale = dh ** -0.5 * LOG2_E
    slope = slope * LOG2_E

    kmh_ref[...] = jnp.zeros_like(kmh_ref)
    kml_ref[...] = jnp.zeros_like(kml_ref)
    for n in range(nb):
        rows = slice(n * blk, (n + 1) * blk)
        kn = k_ref[rows, :]
        kaug_ref[rows, :dh] = kn
        kaug_ref[rows, dh:] = kfeat_ref[rows, :]
        km = jnp.mean(kn.astype(F32), axis=0, keepdims=True)
        hi = km.astype(BF16)
        kmh_ref[n:n + 1, :] = hi
        kml_ref[n:n + 1, :] = (km - hi.astype(F32)).astype(BF16)

    blk_id = lax.broadcasted_iota(jnp.int32, (nbp, prep), 0)
    q_off = lax.broadcasted_iota(jnp.int32, (nbp, prep), 1)
    lane = lax.broadcasted_iota(jnp.int32, (prep, LANES), 1)
    row = lax.broadcasted_iota(jnp.int32, (prep, LANES), 0)
    part = jnp.where(lane < FEAT_BLK, lane - FEAT_ONE, jnp.where(lane < FEAT_OFF, lane - FEAT_BLK, lane - FEAT_OFF))
    pick3 = lambda parts: jnp.where(part == 0, parts[0], jnp.where(part == 1, parts[1], parts[2]))
    slope_feat = pick3(_split3(jnp.full((prep, LANES), slope, F32)))

    def rows_chunk(c, carry):
        row0 = pl.multiple_of(c * prep, prep)
        rows = pl.ds(row0, prep)
        q = q_ref[rows, :]
        gate = _dot_nt(kmh_ref[:nbp, :], q) + _dot_nt(kml_ref[:nbp, :], q)
        q_blk = (row0 + q_off) // blk
        gate = jnp.where(blk_id < q_blk, gate, -jnp.inf)
        sel_bias = jnp.where(blk_id == q_blk, 0.0, MASKED)
        for _ in range(MOBA_TOPK):
            best = jnp.max(gate, axis=0, keepdims=True)
            first = jnp.min(jnp.where(gate == best, blk_id, nbp), axis=0, keepdims=True)
            pick = (blk_id == first) & (best > -jnp.inf)
            sel_bias = jnp.where(pick, 0.0, sel_bias)
            gate = jnp.where(pick, -jnp.inf, gate)
        sel_rows = jnp.concatenate([sel_bias, jnp.zeros((LANES - nbp, prep), F32)], axis=0).T

        t_pos = (row0 + row).astype(F32)
        feat = jnp.where(lane < FEAT_ONE, sel_rows,
                         jnp.where(lane < FEAT_BLK, pick3(_split3(-slope * t_pos)),
                                   jnp.where(lane < FEAT_END, slope_feat, 0.0)))
        qaug_ref[rows, :dh] = (q.astype(F32) * scale).astype(BF16)
        qaug_ref[rows, dh:] = feat.astype(BF16)
        return carry

    lax.fori_loop(0, seq // prep, rows_chunk, 0)


def _moba_kernel(slopes_ref, q_ref, k_ref, v_ref, kfeat_ref, o_ref, kaug_ref, qaug_ref, kmh_ref, kml_ref, s_ref,
                 *, nb):
    blk = MOBA_BLOCK
    dh = MOBA_HEAD_DIM
    qt = MOBA_Q_TILE_BLOCKS * blk
    cb = MOBA_CHUNK_BLOCKS
    span = cb * blk
    h = pl.program_id(1)
    ti = pl.program_id(2)
    first_blk = ti * MOBA_Q_TILE_BLOCKS

    @pl.when(ti == 0)
    def _():
        _moba_prepare(slopes_ref[h], q_ref, k_ref, kfeat_ref, kaug_ref, qaug_ref, kmh_ref, kml_ref, nb=nb)

    q_aug = qaug_ref[pl.ds(pl.multiple_of(ti * qt, qt), qt), :]

    def chunk_rows(p):
        return pl.ds(pl.multiple_of(p * span, span), span)

    def scores(p):
        return _dot_nt(q_aug, kaug_ref[chunk_rows(p), :])

    def lanes_max(m, s):
        for c in range(0, s.shape[1], LANES):
            m = jnp.maximum(m, s[:, c:c + LANES])
        return m

    def pass1(p, mvec):
        s = scores(p)
        s_ref[p] = s
        return lanes_max(mvec, s)

    n_past = first_blk // cb
    mvec = lax.fori_loop(0, n_past, pass1, jnp.full((qt, LANES), -jnp.inf, F32))
    r_idx = lax.broadcasted_iota(jnp.int32, (blk, blk), 0)
    c_idx = lax.broadcasted_iota(jnp.int32, (blk, blk), 1)
    causal_bias = jnp.where(r_idx >= c_idx, 0.0, MASKED)
    own = first_blk % cb
    bias = jnp.concatenate(
        [jnp.concatenate([causal_bias * (own + r == c).astype(F32) for c in range(cb)], axis=1)
         for r in range(MOBA_Q_TILE_BLOCKS)], axis=0)
    s = scores(n_past) + bias
    s_ref[n_past] = s
    m = jnp.max(lanes_max(mvec, s), axis=-1, keepdims=True)

    def pass2(p, carry):
        l, acc = carry
        e = jnp.exp2(s_ref[p] - m)
        l = l + jnp.sum(e, axis=-1, keepdims=True)
        acc = acc + jnp.dot(e.astype(BF16), v_ref[chunk_rows(p), :], preferred_element_type=F32)
        return l, acc

    l, acc = lax.fori_loop(0, n_past + 1, pass2,
                           (jnp.zeros((qt, 1), F32), jnp.zeros((qt, dh), F32)))
    o_ref[...] = (acc / l).astype(o_ref.dtype)


def moba(proj, slopes, *, batch, seq):
    blk = MOBA_BLOCK
    nb = seq // blk
    qt = MOBA_Q_TILE_BLOCKS * blk
    nt = seq // qt
    assert nb % MOBA_CHUNK_BLOCKS == 0 and nb <= MOBA_MAX_BLOCKS and seq % MOBA_PREP_ROWS == 0
    assert MOBA_CHUNK_BLOCKS % MOBA_Q_TILE_BLOCKS == 0
    dh = MOBA_HEAD_DIM
    cq, ck, cv = COL_MOBA_Q // dh, COL_MOBA_K // dh, COL_MOBA_V // dh
    return pl.pallas_call(
        functools.partial(_moba_kernel, nb=nb),
        grid_spec=pltpu.PrefetchScalarGridSpec(
            num_scalar_prefetch=1,
            grid=(batch, MOBA_HEADS, nt),
            in_specs=[pl.BlockSpec((seq, dh), lambda b, h, i, s: (b, cq + h)),
                      pl.BlockSpec((seq, dh), lambda b, h, i, s: (b, ck + h)),
                      pl.BlockSpec((seq, dh), lambda b, h, i, s: (b, cv + h)),
                      pl.BlockSpec((seq, LANES), lambda b, h, i, s: (0, 0))],
            out_specs=pl.BlockSpec((qt, dh), lambda b, h, i, s: (b * nt + i, h)),
            scratch_shapes=[pltpu.VMEM((seq, dh + LANES), BF16), pltpu.VMEM((seq, dh + LANES), BF16),
                            pltpu.VMEM((LANES, dh), BF16), pltpu.VMEM((LANES, dh), BF16),
                            pltpu.VMEM((nb // MOBA_CHUNK_BLOCKS, qt, MOBA_CHUNK_BLOCKS * blk), F32)],
        ),
        out_shape=jax.ShapeDtypeStruct((batch * seq, BRANCH_WIDTH), BF16),
        compiler_params=_params("parallel", "parallel", "arbitrary"),
        name="moba",
    )(slopes, proj, proj, proj, _moba_key_features(seq))


def _swa_kernel(slopes_ref, sinks_ref, q_ref, kp_ref, kc_ref, vp_ref, vc_ref, o_ref):
    w = SWA_WINDOW
    dh = SWA_HEAD_DIM
    n = pl.program_id(1)
    scale = dh ** -0.5
    lane = lax.broadcasted_iota(jnp.int32, (w, LANES), 1)
    low = lane < dh
    t_idx = lax.broadcasted_iota(jnp.int32, (w, w), 0)
    s_idx = lax.broadcasted_iota(jnp.int32, (w, w), 1)
    dist_cur = (t_idx - s_idx).astype(F32)
    dist_prev = dist_cur + float(w)
    ok_cur = s_idx <= t_idx
    ok_prev = (s_idx > t_idx) & (n > 0)

    def halves(ref):
        x = ref[...].astype(F32)
        xr = pltpu.roll(x, dh, axis=1)
        z = jnp.zeros_like(x)
        lo = (jnp.where(low, x, z).astype(BF16), jnp.where(low, z, xr).astype(BF16))
        hi = (jnp.where(low, xr, z).astype(BF16), jnp.where(low, z, x).astype(BF16))
        return lo, hi

    kp, kc, vp, vc = halves(kp_ref), halves(kc_ref), halves(vp_ref), halves(vc_ref)
    pairs = SWA_Q_HEADS // 2
    group = SWA_Q_HEADS // SWA_KV_HEADS
    for pr in range(pairs):
        q = q_ref[:, pr * LANES:(pr + 1) * LANES]
        kh = (2 * pr) // group
        out = None
        for half in range(2):
            hq = 2 * pr + half
            slope = slopes_ref[hq]
            sink = sinks_ref[hq]
            s_p = _dot_nt(q, kp[kh][half]) * scale
            s_c = _dot_nt(q, kc[kh][half]) * scale
            s_p = jnp.where(ok_prev, s_p - slope * dist_prev, MASKED)
            s_c = jnp.where(ok_cur, s_c - slope * dist_cur, MASKED)
            m = jnp.maximum(jnp.max(s_p, axis=-1, keepdims=True), jnp.max(s_c, axis=-1, keepdims=True))
            m = jnp.maximum(m, sink)
            e_p = jnp.exp(s_p - m)
            e_c = jnp.exp(s_c - m)
            denom = (jnp.sum(e_p, axis=-1, keepdims=True) + jnp.sum(e_c, axis=-1, keepdims=True)
                     + jnp.exp(sink - m))
            inv = 1.0 / denom
            o = (jnp.dot((e_p * inv).astype(BF16), vp[kh][half], preferred_element_type=F32)
                 + jnp.dot((e_c * inv).astype(BF16), vc[kh][half], preferred_element_type=F32))
            out = o if out is None else out + o
        o_ref[:, pr * LANES:(pr + 1) * LANES] = out.astype(o_ref.dtype)


def swa(proj, slopes, sinks, *, batch, seq):
    w = SWA_WINDOW
    nblk = seq // w
    bw = BRANCH_WIDTH
    cq = COL_SWA_Q // bw
    ck = COL_SWA_K // SWA_KV_WIDTH
    cv = COL_SWA_V // SWA_KV_WIDTH
    prev = lambda b, n, *_: (b * nblk + jnp.maximum(n - 1, 0))
    cur = lambda b, n, *_: (b * nblk + n)
    return pl.pallas_call(
        _swa_kernel,
        grid_spec=pltpu.PrefetchScalarGridSpec(
            num_scalar_prefetch=2,
            grid=(batch, nblk),
            in_specs=[pl.BlockSpec((w, bw), lambda b, n, *_: (cur(b, n), cq)),
                      pl.BlockSpec((w, SWA_KV_WIDTH), lambda b, n, *_: (prev(b, n), ck)),
                      pl.BlockSpec((w, SWA_KV_WIDTH), lambda b, n, *_: (cur(b, n), ck)),
                      pl.BlockSpec((w, SWA_KV_WIDTH), lambda b, n, *_: (prev(b, n), cv)),
                      pl.BlockSpec((w, SWA_KV_WIDTH), lambda b, n, *_: (cur(b, n), cv))],
            out_specs=pl.BlockSpec((w, bw), lambda b, n, *_: (cur(b, n), 0)),
        ),
        out_shape=jax.ShapeDtypeStruct((batch * seq, bw), BF16),
        compiler_params=_params("parallel", "arbitrary"),
        name="swa",
    )(slopes, sinks, proj, proj, proj, proj, proj)


def _merge_kernel(*refs, tiles):
    n = len(tiles)
    h_ref, r_ref, wg_ref0, wg_ref1, wg_ref2, bg_ref, ya_ref, yb_ref, yc_ref, wb_ref = refs[:10]
    o_ref = refs[10 + n]
    h = h_ref[...]
    r = r_ref[...]
    merged = None
    for br, (wg_ref, y_ref) in enumerate(((wg_ref0, ya_ref), (wg_ref1, yb_ref), (wg_ref2, yc_ref))):
        logits = r * jnp.dot(h, wg_ref[...], preferred_element_type=F32) + bg_ref[br]
        branch = jnp.dot(y_ref[...], wb_ref[br], preferred_element_type=F32)
        term = jax.nn.sigmoid(logits) * branch
        merged = term if merged is None else merged + term
    o_ref[...] = merged.astype(o_ref.dtype)
    _side_cast_run(tiles, refs[10:10 + n], refs[11 + n:])


def merge(h, r, w_gate, b_gate, y_a, y_b, y_c, w_branch, *, casts=(), tm=1024, tn=256):
    t, d = h.shape
    bw = BRANCH_WIDTH
    ni, nj = t // tm, d // tn
    c_in, c_out, c_shapes = _side_cast_plan(casts, ni * nj, lambda i, j: i * nj + j)
    gate_spec = lambda n: pl.BlockSpec((d, tn), lambda i, j: (0, n * nj + j))
    y_spec = pl.BlockSpec((tm, bw), lambda i, j: (i, 0))
    return pl.pallas_call(
        functools.partial(_merge_kernel, tiles=tuple(c.tile for c in casts)),
        grid=(ni, nj),
        in_specs=[pl.BlockSpec((tm, d), lambda i, j: (i, 0)),
                  pl.BlockSpec((tm, 1), lambda i, j: (i, 0)),
                  gate_spec(0), gate_spec(1), gate_spec(2),
                  pl.BlockSpec((N_BRANCHES, 1, tn), lambda i, j: (0, 0, j)),
                  y_spec, y_spec, y_spec,
                  pl.BlockSpec((N_BRANCHES, bw, tn), lambda i, j: (0, 0, j))] + c_in,
        out_specs=[pl.BlockSpec((tm, tn), lambda i, j: (i, j))] + c_out,
        out_shape=[jax.ShapeDtypeStruct((t, d), BF16)] + c_shapes,
        compiler_params=_params("arbitrary", "arbitrary"),
        name="merge",
    )(h, r, w_gate, w_gate, w_gate, b_gate.reshape(N_BRANCHES, 1, d), y_a, y_b, y_c, w_branch,
      *[c.src for c in casts])


def _alibi_slopes():
    i = jnp.arange(1, N_ALIBI_HEADS + 1, dtype=F32)
    s = jnp.exp2(-8.0 * i / N_ALIBI_HEADS)
    return s[:SWA_Q_HEADS], s[SWA_Q_HEADS:]


def kernel(x, ffn1_pre_g, ffn1_w_up, ffn1_w_down, ffn1_post_g, mix_pre_g, w_in, gmlp_ln_g, gmlp_ln_b,
           gmlp_w_s, gmlp_b_s, swa_sinks, w_gate, b_gate, w_branch, w_out, mix_post_g, ffn2_pre_g,
           ffn2_w_up, ffn2_w_down, ffn2_post_g):
    batch, seq, d = x.shape
    depth = ffn1_pre_g.shape[0]
    swa_slopes, moba_slopes = _alibi_slopes()
    xf = x.reshape(batch * seq, d)
    h, r = prenorm(xf, ffn1_pre_g[0])
    steps = (batch * seq // 1024) * 16
    per_step = lambda w: w.shape[0] // steps
    tiled = lambda w: _SideCast(w, per_step(w), MM_TILE)
    plain = lambda w, nblk=steps: _SideCast(w, w.shape[0] // nblk, None)
    w_up1 = ffn1_w_up[0].astype(BF16)
    for i in range(depth):
        act, w_down1, w_in_b, w_gate_b, w_branch_b, w_out_b = ffn_up(
            h, r, w_up1, casts=(tiled(ffn1_w_down[i]), plain(w_in[i]), plain(w_gate[i]),
                                plain(w_branch[i].reshape(-1, d), steps // 4), tiled(w_out[i])))
        xf, h, r = mm_norm_res(act, w_down1, xf, ffn1_post_g[i], mix_pre_g[i], res_scale=0.5)

        proj = matmul(h, r, w_in_b)
        y_a = gmlp(proj, gmlp_ln_g[i], gmlp_ln_b[i], gmlp_w_s[i], gmlp_b_s[i])
        y_b = moba(proj, moba_slopes, batch=batch, seq=seq)
        y_c = swa(proj, swa_slopes, swa_sinks[i], batch=batch, seq=seq)
        merged, w_up2 = merge(h, r, w_gate_b, b_gate[i], y_a, y_b, y_c, w_branch_b.reshape(w_branch[i].shape),
                              casts=(plain(ffn2_w_up[i]),))
        xf, h, r = mm_norm_res(merged, w_out_b, xf, mix_post_g[i], ffn2_pre_g[i], res_scale=1.0)

        last = i + 1 == depth
        casts = (tiled(ffn2_w_down[i]),) + (() if last else (plain(ffn1_w_up[i + 1]),))
        act, w_down2, *nxt = ffn_up(h, r, w_up2, casts=casts)
        xf, h, r = mm_norm_res(act, w_down2, xf, ffn2_post_g[i], None if last else ffn1_pre_g[i + 1],
                               res_scale=0.5)
        if not last:
            w_up1 = nxt[0]
    return xf.reshape(batch, seq, d)
```

```python
import functools
import math
from typing import NamedTuple, Optional, Tuple

import jax
import jax.numpy as jnp
from jax import lax
from jax.experimental import pallas as pl
from jax.experimental.pallas import tpu as pltpu

D_MODEL = 4096
DEPTH = 2
BRANCH_WIDTH = D_MODEL // 4
N_BRANCHES = 3
GMLP_CHUNK = 128
GMLP_GROUP_DIM = 128
GMLP_GROUPS = BRANCH_WIDTH // GMLP_GROUP_DIM
MOBA_HEAD_DIM = 128
MOBA_HEADS = BRANCH_WIDTH // MOBA_HEAD_DIM
MOBA_BLOCK = 256
MOBA_TOPK = 3
SWA_HEAD_DIM = 64
SWA_Q_HEADS = BRANCH_WIDTH // SWA_HEAD_DIM
SWA_KV_HEADS = SWA_Q_HEADS // 8
SWA_WINDOW = 128
SWA_KV_WIDTH = SWA_KV_HEADS * SWA_HEAD_DIM
IN_COLS = 6 * BRANCH_WIDTH + 2 * SWA_KV_WIDTH
D_FF = 2 * D_MODEL
N_ALIBI_HEADS = SWA_Q_HEADS + MOBA_HEADS
RMS_EPS = 1e-6
LN_EPS = 1e-5

LANES = 128
VMEM_LIMIT = 60 * 1024 * 1024
MM_TILE = (2048, 512)
MASKED = -1e30

F32 = jnp.float32
BF16 = jnp.bfloat16

COL_GMLP = 0
COL_MOBA_Q = 2 * BRANCH_WIDTH
COL_MOBA_K = 3 * BRANCH_WIDTH
COL_MOBA_V = 4 * BRANCH_WIDTH
COL_SWA_Q = 5 * BRANCH_WIDTH
COL_SWA_K = 6 * BRANCH_WIDTH
COL_SWA_V = 6 * BRANCH_WIDTH + SWA_KV_WIDTH


def _params(*semantics):
    return pltpu.CompilerParams(dimension_semantics=semantics, vmem_limit_bytes=VMEM_LIMIT)


def _rms_scale(x):
    return lax.rsqrt(jnp.mean(x * x, axis=-1, keepdims=True) + RMS_EPS)


def _dot_nt(a, b):
    return lax.dot_general(a, b, (((1,), (1,)), ((), ())), preferred_element_type=F32)


def _prenorm_kernel(x_ref, g_ref, h_ref, r_ref):
    x = x_ref[...]
    h_ref[...] = (x * g_ref[...]).astype(h_ref.dtype)
    r_ref[...] = _rms_scale(x)


def prenorm(x, g, *, tm=256):
    t, d = x.shape
    return pl.pallas_call(
        _prenorm_kernel,
        grid=(t // tm,),
        in_specs=[pl.BlockSpec((tm, d), lambda i: (i, 0)),
                  pl.BlockSpec((1, d), lambda i: (0, 0))],
        out_specs=[pl.BlockSpec((tm, d), lambda i: (i, 0)),
                   pl.BlockSpec((tm, 1), lambda i: (i, 0))],
        out_shape=[jax.ShapeDtypeStruct((t, d), BF16), jax.ShapeDtypeStruct((t, 1), F32)],
        compiler_params=_params("parallel"),
        name="prenorm",
    )(x, g.reshape(1, d))


class _SideCast(NamedTuple):
    src: jax.Array
    layer: int
    rows: int
    tile: Optional[Tuple[int, int]]


def _side_cast_plan(casts, steps, step_of):
    in_specs, out_specs, out_shapes = [], [], []
    for c in casts:
        _, nrow, ncol = c.src.shape
        nblk = nrow // c.rows
        assert nrow % c.rows == 0 and steps % nblk == 0
        blk_of = lambda *ids, hold=steps // nblk: step_of(*ids) // hold
        in_specs.append(pl.BlockSpec((None, c.rows, ncol), lambda *ids, f=blk_of, layer=c.layer: (layer, f(*ids), 0)))
        if c.tile is None:
            out_specs.append(pl.BlockSpec((c.rows, ncol), lambda *ids, f=blk_of: (f(*ids), 0)))
            out_shapes.append(jax.ShapeDtypeStruct((nrow, ncol), BF16))
        else:
            tk, tn = c.tile
            per = tk // c.rows
            assert tk % c.rows == 0 and nrow % tk == 0 and ncol % tn == 0
            out_specs.append(pl.BlockSpec((None, ncol // tn, c.rows, tn),
                                          lambda *ids, f=blk_of, per=per: (f(*ids) // per, 0, f(*ids) % per, 0)))
            out_shapes.append(jax.ShapeDtypeStruct((nrow // tk, ncol // tn, tk, tn), BF16))
    return in_specs, out_specs, out_shapes


def _side_cast_run(tiles, src_refs, dst_refs):
    for tile, src, dst in zip(tiles, src_refs, dst_refs, strict=True):
        if tile is None:
            dst[...] = src[...].astype(BF16)
        else:
            tn = tile[1]
            for j in range(dst.shape[0]):
                dst[j] = src[:, j * tn:(j + 1) * tn].astype(BF16)


def _tiled_bf16(w, tk, tn):
    kdim, n = w.shape
    return w.astype(BF16).reshape(kdim // tk, tk, n // tn, tn).transpose(0, 2, 1, 3)


def _ffn_up_kernel(*refs, tiles):
    n = len(tiles)
    h_ref, r_ref, wg_ref, wu_ref = refs[:4]
    o_ref = refs[4 + n]
    h = h_ref[...]
    r = r_ref[...]
    g = r * jnp.dot(h, wg_ref[...], preferred_element_type=F32)
    u = r * jnp.dot(h, wu_ref[...], preferred_element_type=F32)
    o_ref[...] = (g * jax.nn.sigmoid(g) * u).astype(o_ref.dtype)
    _side_cast_run(tiles, refs[4:4 + n], refs[5 + n:])


def ffn_up(h, r, w_up, *, casts=(), tm=1024, tn=512):
    t, d = h.shape
    f = w_up.shape[1] // 2
    ni, nj = t // tm, f // tn
    c_in, c_out, c_shapes = _side_cast_plan(casts, ni * nj, lambda i, j: i * nj + j)
    return pl.pallas_call(
        functools.partial(_ffn_up_kernel, tiles=tuple(c.tile for c in casts)),
        grid=(ni, nj),
        in_specs=[pl.BlockSpec((tm, d), lambda i, j: (i, 0)),
                  pl.BlockSpec((tm, 1), lambda i, j: (i, 0)),
                  pl.BlockSpec((d, tn), lambda i, j: (0, j)),
                  pl.BlockSpec((d, tn), lambda i, j: (0, j + nj))] + c_in,
        out_specs=[pl.BlockSpec((tm, tn), lambda i, j: (i, j))] + c_out,
        out_shape=[jax.ShapeDtypeStruct((t, f), BF16)] + c_shapes,
        compiler_params=_params("arbitrary", "arbitrary"),
        name="ffn_up",
    )(h, r, w_up, w_up, *[c.src for c in casts])


def _mm_norm_res_kernel(*refs, res_scale, nkk, nj, ni, rows, emit_h):
    if emit_h:
        a_ref, w_ref, x_ref, pg_ref, ng_ref, xo_ref, ho_ref, ro_ref, y_ref, ry_ref, sx_ref = refs
    else:
        a_ref, w_ref, x_ref, pg_ref, xo_ref, y_ref, ry_ref = refs
    b = pl.program_id(0)
    s = pl.program_id(1)
    tm, d = y_ref.shape
    tn = w_ref.shape[1]

    def tile_cols(e):
        return pl.ds(pl.multiple_of(e * tn, tn), tn)

    def for_row_chunks(body):
        def step(c, carry):
            body(pl.ds(pl.multiple_of(c * rows, rows), rows))
            return carry
        lax.fori_loop(0, tm // rows, step, 0)

    @pl.when((b > 0) & (s == 0))
    def _():
        def stats(r):
            ry_ref[r, :] = _rms_scale(y_ref[r, :])
            if emit_h:
                sx_ref[r, :] = jnp.zeros((rows, 1), F32)
        for_row_chunks(stats)

    def residual(r, cols):
        xn = x_ref[r, :] + res_scale * (y_ref[r, cols] * ry_ref[r, :] * pg_ref[...])
        xo_ref[r, :] = xn
        if emit_h:
            ho_ref[r, :] = (xn * ng_ref[...]).astype(BF16)
            sx_ref[r, :] += jnp.sum(xn * xn, axis=-1, keepdims=True)

    first_k = s < nj

    @pl.when((b > 0) & (b < ni) & first_k)
    def _():
        cols = tile_cols(s)
        for c in range(tm // rows):
            residual(pl.ds(c * rows, rows), cols)
        y_ref[:, cols] = jnp.dot(a_ref[...], w_ref[...], preferred_element_type=F32)

    @pl.when((b == 0) & first_k)
    def _():
        y_ref[:, tile_cols(s)] = jnp.dot(a_ref[...], w_ref[...], preferred_element_type=F32)

    @pl.when((b == ni) & first_k)
    def _():
        cols = tile_cols(s)
        for_row_chunks(lambda r: residual(r, cols))

    if emit_h:
        @pl.when((b > 0) & (s == nj - 1))
        def _():
            ro_ref[...] = lax.rsqrt(sx_ref[...] * (1.0 / d) + RMS_EPS)

    @pl.when((b < ni) & (s >= nj))
    def _():
        y_ref[:, tile_cols(s % nj)] += jnp.dot(a_ref[...], w_ref[...], preferred_element_type=F32)


def mm_norm_res(a, w_tiled, x, post_g, next_g=None, *, res_scale, tm=1024, rows=256):
    t, kdim = a.shape
    nkk, nj, tkk, tn = w_tiled.shape
    d = nj * tn
    assert kdim == nkk * tkk
    ni = t // tm
    n_mm = nkk * nj
    emit_h = next_g is not None
    mm_row = lambda b: jnp.minimum(b, ni - 1)
    mm_step = lambda b, s: jnp.where(b < ni, s, n_mm - 1)
    ep_row = lambda b: jnp.maximum(b - 1, 0)
    ep_tile = lambda b, s: jnp.where(b > 0, jnp.minimum(s, nj - 1), 0)
    tile_spec = pl.BlockSpec((tm, tn), lambda b, s: (ep_row(b), ep_tile(b, s)))
    gain_spec = pl.BlockSpec((1, tn), lambda b, s: (0, ep_tile(b, s)))
    operands = [a, w_tiled, x, post_g.reshape(1, d)]
    in_specs = [pl.BlockSpec((tm, tkk), lambda b, s: (mm_row(b), mm_step(b, s) // nj)),
                pl.BlockSpec((None, None, tkk, tn), lambda b, s: (mm_step(b, s) // nj, mm_step(b, s) % nj, 0, 0)),
                tile_spec, gain_spec]
    out_shape = [jax.ShapeDtypeStruct((t, d), F32)]
    out_specs = [tile_spec]
    scratch = [pltpu.VMEM((tm, d), F32), pltpu.VMEM((tm, 1), F32)]
    if emit_h:
        operands.append(next_g.reshape(1, d))
        in_specs.append(gain_spec)
        out_shape += [jax.ShapeDtypeStruct((t, d), BF16), jax.ShapeDtypeStruct((t, 1), F32)]
        out_specs += [tile_spec, pl.BlockSpec((tm, 1), lambda b, s: (ep_row(b), 0))]
        scratch.append(pltpu.VMEM((tm, 1), F32))
    res = pl.pallas_call(
        functools.partial(_mm_norm_res_kernel, res_scale=res_scale, nkk=nkk, nj=nj, ni=ni, rows=rows,
                          emit_h=emit_h),
        grid=(ni + 1, n_mm),
        in_specs=in_specs,
        out_specs=out_specs,
        out_shape=out_shape,
        scratch_shapes=scratch,
        compiler_params=_params("arbitrary", "arbitrary"),
        name="mm_norm_res",
    )(*operands)
    return (res[0], res[1], res[2]) if emit_h else (res[0], None, None)


def _matmul_kernel(a_ref, r_ref, w_ref, o_ref):
    acc = jnp.dot(a_ref[...], w_ref[...], preferred_element_type=F32)
    o_ref[...] = (r_ref[...] * acc).astype(o_ref.dtype)


def matmul(a, r, w, *, tm=1024, tn=1280):
    t, kdim = a.shape
    n = w.shape[1]
    return pl.pallas_call(
        _matmul_kernel,
        grid=(t // tm, n // tn),
        in_specs=[pl.BlockSpec((tm, kdim), lambda i, j: (i, 0)),
                  pl.BlockSpec((tm, 1), lambda i, j: (i, 0)),
                  pl.BlockSpec((kdim, tn), lambda i, j: (0, j))],
        out_specs=pl.BlockSpec((tm, tn), lambda i, j: (i, j)),
        out_shape=jax.ShapeDtypeStruct((t, n), BF16),
        compiler_params=_params("parallel", "arbitrary"),
        name="in_proj",
    )(a, r, w)


def _gelu_exact(x):
    return 0.5 * x * (1.0 + lax.erf(x * (1.0 / math.sqrt(2.0))))


def _gmlp_kernel(z_ref, lng_ref, lnb_ref, ws_ref, bs_ref, o_ref, *, chunks):
    bw = BRANCH_WIDTH
    row = lax.broadcasted_iota(jnp.int32, (GMLP_CHUNK, GMLP_CHUNK), 0)
    col = lax.broadcasted_iota(jnp.int32, (GMLP_CHUNK, GMLP_CHUNK), 1)
    causal = row >= col
    for c in range(chunks):
        r = slice(c * GMLP_CHUNK, (c + 1) * GMLP_CHUNK)
        u = _gelu_exact(z_ref[r, :bw].astype(F32))
        v = _gelu_exact(z_ref[r, bw:].astype(F32))
        mu = jnp.mean(v, axis=-1, keepdims=True)
        vc = v - mu
        var = jnp.mean(vc * vc, axis=-1, keepdims=True)
        vn = (vc * lax.rsqrt(var + LN_EPS) * lng_ref[...] + lnb_ref[...]).astype(BF16)
        for g in range(GMLP_GROUPS):
            gs = slice(g * GMLP_GROUP_DIM, (g + 1) * GMLP_GROUP_DIM)
            w = jnp.where(causal, ws_ref[g], 0.0).astype(BF16)
            mixed = jnp.dot(w, vn[:, gs], preferred_element_type=F32) + bs_ref[:, g:g + 1]
            o_ref[r, gs] = (u[:, gs] * mixed).astype(o_ref.dtype)


def gmlp(proj, ln_g, ln_b, w_s, b_s, *, chunks=4):
    t = proj.shape[0]
    bw = BRANCH_WIDTH
    tm = chunks * GMLP_CHUNK
    return pl.pallas_call(
        functools.partial(_gmlp_kernel, chunks=chunks),
        grid=(t // tm,),
        in_specs=[pl.BlockSpec((tm, 2 * bw), lambda i: (i, 0)),
                  pl.BlockSpec((1, bw), lambda i: (0, 0)),
                  pl.BlockSpec((1, bw), lambda i: (0, 0)),
                  pl.BlockSpec((GMLP_GROUPS, GMLP_CHUNK, GMLP_CHUNK), lambda i: (0, 0, 0)),
                  pl.BlockSpec((GMLP_CHUNK, GMLP_GROUPS), lambda i: (0, 0))],
        out_specs=pl.BlockSpec((tm, bw), lambda i: (i, 0)),
        out_shape=jax.ShapeDtypeStruct((t, bw), BF16),
        compiler_params=_params("parallel"),
        name="gmlp",
    )(proj, ln_g.reshape(1, bw), ln_b.reshape(1, bw), w_s, b_s.T)


MOBA_MAX_BLOCKS = 32
MOBA_CHUNK_BLOCKS = 4
MOBA_Q_TILE_BLOCKS = 2
MOBA_PREP_ROWS = 512
LOG2_E = math.log2(math.e)
FEAT_SEL, FEAT_ONE, FEAT_BLK, FEAT_OFF, FEAT_END = 0, 32, 35, 38, 41


def _moba_key_features(seq):
    pos = jnp.arange(seq, dtype=jnp.int32)[:, None]
    kb, off = pos // MOBA_BLOCK, pos % MOBA_BLOCK
    lane = jnp.arange(LANES, dtype=jnp.int32)[None, :]
    feat = jnp.where(lane < FEAT_ONE, (lane == kb).astype(F32),
                     jnp.where(lane < FEAT_BLK, 1.0,
                               jnp.where(lane < FEAT_OFF, (kb * MOBA_BLOCK).astype(F32),
                                         jnp.where(lane < FEAT_END, off.astype(F32), 0.0))))
    return feat.astype(BF16)


def _split3(x):
    hi = x.astype(BF16).astype(F32)
    mid = (x - hi).astype(BF16).astype(F32)
    return hi, mid, x - hi - mid


def _moba_prepare(slope, q_ref, k_ref, kfeat_ref, kaug_ref, qaug_ref, kmh_ref, kml_ref, *, nb):
    blk = MOBA_BLOCK
    dh = MOBA_HEAD_DIM
    seq = q_ref.shape[0]
    prep = MOBA_PREP_ROWS
    nbp = MOBA_MAX_BLOCKS
    scale = dh ** -0.5 * LOG2_E
    slope = slope * LOG2_E

    kmh_ref[...] = jnp.zeros_like(kmh_ref)
    kml_ref[...] = jnp.zeros_like(kml_ref)
    for n in range(nb):
        rows = slice(n * blk, (n + 1) * blk)
        kn = k_ref[rows, :]
        kaug_ref[rows, :dh] = kn
        kaug_ref[rows, dh:] = kfeat_ref[rows, :]
        km = jnp.mean(kn.astype(F32), axis=0, keepdims=True)
        hi = km.astype(BF16)
        kmh_ref[n:n + 1, :] = hi
        kml_ref[n:n + 1, :] = (km - hi.astype(F32)).astype(BF16)

    blk_id = lax.broadcasted_iota(jnp.int32, (nbp, prep), 0)
    q_off = lax.broadcasted_iota(jnp.int32, (nbp, prep), 1)
    lane = lax.broadcasted_iota(jnp.int32, (prep, LANES), 1)
    row = lax.broadcasted_iota(jnp.int32, (prep, LANES), 0)
    part = jnp.where(lane < FEAT_BLK, lane - FEAT_ONE, jnp.where(lane < FEAT_OFF, lane - FEAT_BLK, lane - FEAT_OFF))
    pick3 = lambda parts: jnp.where(part == 0, parts[0], jnp.where(part == 1, parts[1], parts[2]))
    slope_feat = pick3(_split3(jnp.full((prep, LANES), slope, F32)))

    def rows_chunk(c, carry):
        row0 = pl.multiple_of(c * prep, prep)
        rows = pl.ds(row0, prep)
        q = q_ref[rows, :]
        gate = _dot_nt(kmh_ref[:nbp, :], q) + _dot_nt(kml_ref[:nbp, :], q)
        q_blk = (row0 + q_off) // blk
        gate = jnp.where(blk_id < q_blk, gate, -jnp.inf)
        sel_bias = jnp.where(blk_id == q_blk, 0.0, MASKED)
        for _ in range(MOBA_TOPK):
            best = jnp.max(gate, axis=0, keepdims=True)
            first = jnp.min(jnp.where(gate == best, blk_id, nbp), axis=0, keepdims=True)
            pick = (blk_id == first) & (best > -jnp.inf)
            sel_bias = jnp.where(pick, 0.0, sel_bias)
            gate = jnp.where(pick, -jnp.inf, gate)
        sel_rows = jnp.concatenate([sel_bias, jnp.zeros((LANES - nbp, prep), F32)], axis=0).T

        t_pos = (row0 + row).astype(F32)
        feat = jnp.where(lane < FEAT_ONE, sel_rows,
                         jnp.where(lane < FEAT_BLK, pick3(_split3(-slope * t_pos)),
                                   jnp.where(lane < FEAT_END, slope_feat, 0.0)))
        qaug_ref[rows, :dh] = (q.astype(F32) * scale).astype(BF16)
        qaug_ref[rows, dh:] = feat.astype(BF16)
        return carry

    lax.fori_loop(0, seq // prep, rows_chunk, 0)


def _moba_kernel(slopes_ref, q_ref, k_ref, v_ref, kfeat_ref, o_ref, kaug_ref, qaug_ref, kmh_ref, kml_ref, s_ref,
                 *, nb):
    blk = MOBA_BLOCK
    dh = MOBA_HEAD_DIM
    qt = MOBA_Q_TILE_BLOCKS * blk
    cb = MOBA_CHUNK_BLOCKS
    span = cb * blk
    h = pl.program_id(1)
    ti = pl.program_id(2)
    first_blk = ti * MOBA_Q_TILE_BLOCKS

    @pl.when(ti == 0)
    def _():
        _moba_prepare(slopes_ref[h], q_ref, k_ref, kfeat_ref, kaug_ref, qaug_ref, kmh_ref, kml_ref, nb=nb)

    q_aug = qaug_ref[pl.ds(pl.multiple_of(ti * qt, qt), qt), :]

    def chunk_rows(p):
        return pl.ds(pl.multiple_of(p * span, span), span)

    def scores(p):
        return _dot_nt(q_aug, kaug_ref[chunk_rows(p), :])

    def lanes_max(m, s):
        for c in range(0, s.shape[1], LANES):
            m = jnp.maximum(m, s[:, c:c + LANES])
        return m

    def pass1(p, mvec):
        s = scores(p)
        s_ref[p] = s
        return lanes_max(mvec, s)

    n_past = first_blk // cb
    mvec = lax.fori_loop(0, n_past, pass1, jnp.full((qt, LANES), -jnp.inf, F32))
    r_idx = lax.broadcasted_iota(jnp.int32, (blk, blk), 0)
    c_idx = lax.broadcasted_iota(jnp.int32, (blk, blk), 1)
    causal_bias = jnp.where(r_idx >= c_idx, 0.0, MASKED)
    own = first_blk % cb
    bias = jnp.concatenate(
        [jnp.concatenate([causal_bias * (own + r == c).astype(F32) for c in range(cb)], axis=1)
         for r in range(MOBA_Q_TILE_BLOCKS)], axis=0)
    s = scores(n_past) + bias
    s_ref[n_past] = s
    m = jnp.max(lanes_max(mvec, s), axis=-1, keepdims=True)

    def pass2(p, carry):
        l, acc = carry
        e = jnp.exp2(s_ref[p] - m)
        l = l + jnp.sum(e, axis=-1, keepdims=True)
        acc = acc + jnp.dot(e.astype(BF16), v_ref[chunk_rows(p), :], preferred_element_type=F32)
        return l, acc

    l, acc = lax.fori_loop(0, n_past + 1, pass2,
                           (jnp.zeros((qt, 1), F32), jnp.zeros((qt, dh), F32)))
    o_ref[...] = (acc / l).astype(o_ref.dtype)


def moba(proj, slopes, *, batch, seq):
    blk = MOBA_BLOCK
    nb = seq // blk
    qt = MOBA_Q_TILE_BLOCKS * blk
    nt = seq // qt
    assert nb % MOBA_CHUNK_BLOCKS == 0 and nb <= MOBA_MAX_BLOCKS and seq % MOBA_PREP_ROWS == 0
    assert MOBA_CHUNK_BLOCKS % MOBA_Q_TILE_BLOCKS == 0
    dh = MOBA_HEAD_DIM
    cq, ck, cv = COL_MOBA_Q // dh, COL_MOBA_K // dh, COL_MOBA_V // dh
    return pl.pallas_call(
        functools.partial(_moba_kernel, nb=nb),
        grid_spec=pltpu.PrefetchScalarGridSpec(
            num_scalar_prefetch=1,
            grid=(batch, MOBA_HEADS, nt),
            in_specs=[pl.BlockSpec((seq, dh), lambda b, h, i, s: (b, cq + h)),
                      pl.BlockSpec((seq, dh), lambda b, h, i, s: (b, ck + h)),
                      pl.BlockSpec((seq, dh), lambda b, h, i, s: (b, cv + h)),
                      pl.BlockSpec((seq, LANES), lambda b, h, i, s: (0, 0))],
            out_specs=pl.BlockSpec((qt, dh), lambda b, h, i, s: (b * nt + i, h)),
            scratch_shapes=[pltpu.VMEM((seq, dh + LANES), BF16), pltpu.VMEM((seq, dh + LANES), BF16),
                            pltpu.VMEM((LANES, dh), BF16), pltpu.VMEM((LANES, dh), BF16),
                            pltpu.VMEM((nb // MOBA_CHUNK_BLOCKS, qt, MOBA_CHUNK_BLOCKS * blk), F32)],
        ),
        out_shape=jax.ShapeDtypeStruct((batch * seq, BRANCH_WIDTH), BF16),
        compiler_params=_params("parallel", "parallel", "arbitrary"),
        name="moba",
    )(slopes, proj, proj, proj, _moba_key_features(seq))


def _swa_kernel(slopes_ref, sinks_ref, q_ref, kp_ref, kc_ref, vp_ref, vc_ref, o_ref):
    w = SWA_WINDOW
    dh = SWA_HEAD_DIM
    n = pl.program_id(1)
    scale = dh ** -0.5
    low2 = lax.broadcasted_iota(jnp.int32, (2 * w, LANES), 1) < dh
    t_idx = lax.broadcasted_iota(jnp.int32, (w, 4 * w), 0)
    col = lax.broadcasted_iota(jnp.int32, (w, 4 * w), 1)
    s_idx = col % w
    is_prev = (col % (2 * w)) < w
    second = col >= 2 * w
    dist = (t_idx - s_idx).astype(F32) + jnp.where(is_prev, float(w), 0.0)
    ok = (is_prev & (s_idx > t_idx) & (n > 0)) | (jnp.logical_not(is_prev) & (s_idx <= t_idx))

    def stacked(prev_ref, cur_ref):
        x = jnp.concatenate([prev_ref[...], cur_ref[...]], axis=0).astype(F32)
        xr = pltpu.roll(x, dh, axis=1)
        z = jnp.zeros_like(x)
        head0 = jnp.concatenate([jnp.where(low2, x, z), jnp.where(low2, z, xr)], axis=0)
        head1 = jnp.concatenate([jnp.where(low2, xr, z), jnp.where(low2, z, x)], axis=0)
        return head0.astype(BF16), head1.astype(BF16)

    k_st = stacked(kp_ref, kc_ref)
    v_st = stacked(vp_ref, vc_ref)
    pairs = SWA_Q_HEADS // 2
    group = SWA_Q_HEADS // SWA_KV_HEADS
    for pr in range(pairs):
        q = q_ref[:, pr * LANES:(pr + 1) * LANES]
        kh = (2 * pr) // group
        slope = jnp.where(second, slopes_ref[2 * pr + 1], slopes_ref[2 * pr])
        s = _dot_nt(q, k_st[kh]) * scale
        s = jnp.where(ok, s - slope * dist, MASKED)
        probs = []
        for half in range(2):
            sink = sinks_ref[2 * pr + half]
            sh = s[:, half * 2 * w:(half + 1) * 2 * w]
            m = jnp.maximum(jnp.max(sh, axis=-1, keepdims=True), sink)
            e = jnp.exp(sh - m)
            denom = jnp.sum(e, axis=-1, keepdims=True) + jnp.exp(sink - m)
            probs.append((e * (1.0 / denom)).astype(BF16))
        out = jnp.dot(jnp.concatenate(probs, axis=1), v_st[kh], preferred_element_type=F32)
        o_ref[:, pr * LANES:(pr + 1) * LANES] = out.astype(o_ref.dtype)


def swa(proj, slopes, sinks, *, batch, seq):
    w = SWA_WINDOW
    nblk = seq // w
    bw = BRANCH_WIDTH
    cq = COL_SWA_Q // bw
    ck = COL_SWA_K // SWA_KV_WIDTH
    cv = COL_SWA_V // SWA_KV_WIDTH
    prev = lambda b, n, *_: (b * nblk + jnp.maximum(n - 1, 0))
    cur = lambda b, n, *_: (b * nblk + n)
    return pl.pallas_call(
        _swa_kernel,
        grid_spec=pltpu.PrefetchScalarGridSpec(
            num_scalar_prefetch=2,
            grid=(batch, nblk),
            in_specs=[pl.BlockSpec((w, bw), lambda b, n, *_: (cur(b, n), cq)),
                      pl.BlockSpec((w, SWA_KV_WIDTH), lambda b, n, *_: (prev(b, n), ck)),
                      pl.BlockSpec((w, SWA_KV_WIDTH), lambda b, n, *_: (cur(b, n), ck)),
                      pl.BlockSpec((w, SWA_KV_WIDTH), lambda b, n, *_: (prev(b, n), cv)),
                      pl.BlockSpec((w, SWA_KV_WIDTH), lambda b, n, *_: (cur(b, n), cv))],
            out_specs=pl.BlockSpec((w, bw), lambda b, n, *_: (cur(b, n), 0)),
        ),
        out_shape=jax.ShapeDtypeStruct((batch * seq, bw), BF16),
        compiler_params=_params("parallel", "arbitrary"),
        name="swa",
    )(slopes, sinks, proj, proj, proj, proj, proj)


def _merge_kernel(*refs, tiles):
    n = len(tiles)
    h_ref, r_ref, wg_ref0, wg_ref1, wg_ref2, bg_ref, ya_ref, yb_ref, yc_ref, wb_ref = refs[:10]
    o_ref = refs[10 + n]
    h = h_ref[...]
    r = r_ref[...]
    merged = None
    for br, (wg_ref, y_ref) in enumerate(((wg_ref0, ya_ref), (wg_ref1, yb_ref), (wg_ref2, yc_ref))):
        logits = r * jnp.dot(h, wg_ref[...], preferred_element_type=F32) + bg_ref[br]
        branch = jnp.dot(y_ref[...], wb_ref[br], preferred_element_type=F32)
        term = jax.nn.sigmoid(logits) * branch
        merged = term if merged is None else merged + term
    o_ref[...] = merged.astype(o_ref.dtype)
    _side_cast_run(tiles, refs[10:10 + n], refs[11 + n:])


def merge(h, r, w_gate, b_gate, y_a, y_b, y_c, w_branch, *, casts=(), tm=1024, tn=256):
    t, d = h.shape
    bw = BRANCH_WIDTH
    ni, nj = t // tm, d // tn
    c_in, c_out, c_shapes = _side_cast_plan(casts, ni * nj, lambda i, j: i * nj + j)
    gate_spec = lambda n: pl.BlockSpec((d, tn), lambda i, j: (0, n * nj + j))
    y_spec = pl.BlockSpec((tm, bw), lambda i, j: (i, 0))
    return pl.pallas_call(
        functools.partial(_merge_kernel, tiles=tuple(c.tile for c in casts)),
        grid=(ni, nj),
        in_specs=[pl.BlockSpec((tm, d), lambda i, j: (i, 0)),
                  pl.BlockSpec((tm, 1), lambda i, j: (i, 0)),
                  gate_spec(0), gate_spec(1), gate_spec(2),
                  pl.BlockSpec((N_BRANCHES, 1, tn), lambda i, j: (0, 0, j)),
                  y_spec, y_spec, y_spec,
                  pl.BlockSpec((N_BRANCHES, bw, tn), lambda i, j: (0, 0, j))] + c_in,
        out_specs=[pl.BlockSpec((tm, tn), lambda i, j: (i, j))] + c_out,
        out_shape=[jax.ShapeDtypeStruct((t, d), BF16)] + c_shapes,
        compiler_params=_params("arbitrary", "arbitrary"),
        name="merge",
    )(h, r, w_gate, w_gate, w_gate, b_gate.reshape(N_BRANCHES, 1, d), y_a, y_b, y_c, w_branch,
      *[c.src for c in casts])


def _alibi_slopes():
    i = jnp.arange(1, N_ALIBI_HEADS + 1, dtype=F32)
    s = jnp.exp2(-8.0 * i / N_ALIBI_HEADS)
    return s[:SWA_Q_HEADS], s[SWA_Q_HEADS:]


def kernel(x, ffn1_pre_g, ffn1_w_up, ffn1_w_down, ffn1_post_g, mix_pre_g, w_in, gmlp_ln_g, gmlp_ln_b,
           gmlp_w_s, gmlp_b_s, swa_sinks, w_gate, b_gate, w_branch, w_out, mix_post_g, ffn2_pre_g,
           ffn2_w_up, ffn2_w_down, ffn2_post_g):
    batch, seq, d = x.shape
    depth = ffn1_pre_g.shape[0]
    swa_slopes, moba_slopes = _alibi_slopes()
    xf = x.reshape(batch * seq, d)
    h, r = prenorm(xf, ffn1_pre_g[0])
    steps = (batch * seq // 1024) * 16
    tiled = lambda w, i: _SideCast(w, i, w.shape[1] // steps, MM_TILE)
    plain = lambda w, i, nblk=steps: _SideCast(w, i, w.shape[1] // nblk, None)
    w_branch_rows = w_branch.reshape(depth, -1, d)
    w_up1 = ffn1_w_up[0].astype(BF16)
    for i in range(depth):
        act, w_down1, w_in_b, w_gate_b, w_branch_b, w_out_b = ffn_up(
            h, r, w_up1, casts=(tiled(ffn1_w_down, i), plain(w_in, i), plain(w_gate, i),
                                plain(w_branch_rows, i, steps // 4), tiled(w_out, i)))
        xf, h, r = mm_norm_res(act, w_down1, xf, ffn1_post_g[i], mix_pre_g[i], res_scale=0.5)

        proj = matmul(h, r, w_in_b)
        y_a = gmlp(proj, gmlp_ln_g[i], gmlp_ln_b[i], gmlp_w_s[i], gmlp_b_s[i])
        y_b = moba(proj, moba_slopes, batch=batch, seq=seq)
        y_c = swa(proj, swa_slopes, swa_sinks[i], batch=batch, seq=seq)
        merged, w_up2 = merge(h, r, w_gate_b, b_gate[i], y_a, y_b, y_c, w_branch_b.reshape(w_branch.shape[1:]),
                              casts=(plain(ffn2_w_up, i),))
        xf, h, r = mm_norm_res(merged, w_out_b, xf, mix_post_g[i], ffn2_pre_g[i], res_scale=1.0)

        last = i + 1 == depth
        casts = (tiled(ffn2_w_down, i),) + (() if last else (plain(ffn1_w_up, i + 1),))
        act, w_down2, *nxt = ffn_up(h, r, w_up2, casts=casts)
        xf, h, r = mm_norm_res(act, w_down2, xf, ffn2_post_g[i], None if last else ffn1_pre_g[i + 1],
                               res_scale=0.5)
        if not last:
            w_up1 = nxt[0]
    return xf.reshape(batch, seq, d)
```

```python
import functools
import math
from typing import NamedTuple, Optional, Tuple

import jax
import jax.numpy as jnp
from jax import lax
from jax.experimental import pallas as pl
from jax.experimental.pallas import tpu as pltpu

D_MODEL = 4096
DEPTH = 2
BRANCH_WIDTH = D_MODEL // 4
N_BRANCHES = 3
GMLP_CHUNK = 128
GMLP_GROUP_DIM = 128
GMLP_GROUPS = BRANCH_WIDTH // GMLP_GROUP_DIM
MOBA_HEAD_DIM = 128
MOBA_HEADS = BRANCH_WIDTH // MOBA_HEAD_DIM
MOBA_BLOCK = 256
MOBA_TOPK = 3
SWA_HEAD_DIM = 64
SWA_Q_HEADS = BRANCH_WIDTH // SWA_HEAD_DIM
SWA_KV_HEADS = SWA_Q_HEADS // 8
SWA_WINDOW = 128
SWA_KV_WIDTH = SWA_KV_HEADS * SWA_HEAD_DIM
IN_COLS = 6 * BRANCH_WIDTH + 2 * SWA_KV_WIDTH
D_FF = 2 * D_MODEL
N_ALIBI_HEADS = SWA_Q_HEADS + MOBA_HEADS
RMS_EPS = 1e-6
LN_EPS = 1e-5

LANES = 128
VMEM_LIMIT = 60 * 1024 * 1024
MM_TILE = (2048, 512)
MASKED = -1e30

F32 = jnp.float32
BF16 = jnp.bfloat16

COL_GMLP = 0
COL_MOBA_Q = 2 * BRANCH_WIDTH
COL_MOBA_K = 3 * BRANCH_WIDTH
COL_MOBA_V = 4 * BRANCH_WIDTH
COL_SWA_Q = 5 * BRANCH_WIDTH
COL_SWA_K = 6 * BRANCH_WIDTH
COL_SWA_V = 6 * BRANCH_WIDTH + SWA_KV_WIDTH


def _params(*semantics):
    return pltpu.CompilerParams(dimension_semantics=semantics, vmem_limit_bytes=VMEM_LIMIT)


def _rms_scale(x):
    return lax.rsqrt(jnp.mean(x * x, axis=-1, keepdims=True) + RMS_EPS)


def _dot_nt(a, b):
    return lax.dot_general(a, b, (((1,), (1,)), ((), ())), preferred_element_type=F32)


def _prenorm_kernel(x_ref, g_ref, h_ref, r_ref):
    x = x_ref[...]
    h_ref[...] = (x * g_ref[...]).astype(h_ref.dtype)
    r_ref[...] = _rms_scale(x)


def prenorm(x, g, *, tm=256):
    t, d = x.shape
    return pl.pallas_call(
        _prenorm_kernel,
        grid=(t // tm,),
        in_specs=[pl.BlockSpec((tm, d), lambda i: (i, 0)),
                  pl.BlockSpec((1, d), lambda i: (0, 0))],
        out_specs=[pl.BlockSpec((tm, d), lambda i: (i, 0)),
                   pl.BlockSpec((tm, 1), lambda i: (i, 0))],
        out_shape=[jax.ShapeDtypeStruct((t, d), BF16), jax.ShapeDtypeStruct((t, 1), F32)],
        compiler_params=_params("parallel"),
        name="prenorm",
    )(x, g.reshape(1, d))


class _SideCast(NamedTuple):
    src: jax.Array
    layer: int
    rows: int
    tile: Optional[Tuple[int, int]]


def _side_cast_plan(casts, steps, step_of):
    in_specs, out_specs, out_shapes = [], [], []
    for c in casts:
        _, nrow, ncol = c.src.shape
        nblk = nrow // c.rows
        assert nrow % c.rows == 0 and steps % nblk == 0
        blk_of = lambda *ids, hold=steps // nblk: step_of(*ids) // hold
        in_specs.append(pl.BlockSpec((None, c.rows, ncol), lambda *ids, f=blk_of, layer=c.layer: (layer, f(*ids), 0)))
        if c.tile is None:
            out_specs.append(pl.BlockSpec((c.rows, ncol), lambda *ids, f=blk_of: (f(*ids), 0)))
            out_shapes.append(jax.ShapeDtypeStruct((nrow, ncol), BF16))
        else:
            tk, tn = c.tile
            per = tk // c.rows
            assert tk % c.rows == 0 and nrow % tk == 0 and ncol % tn == 0
            out_specs.append(pl.BlockSpec((None, ncol // tn, c.rows, tn),
                                          lambda *ids, f=blk_of, per=per: (f(*ids) // per, 0, f(*ids) % per, 0)))
            out_shapes.append(jax.ShapeDtypeStruct((nrow // tk, ncol // tn, tk, tn), BF16))
    return in_specs, out_specs, out_shapes


def _side_cast_run(tiles, src_refs, dst_refs):
    for tile, src, dst in zip(tiles, src_refs, dst_refs, strict=True):
        if tile is None:
            dst[...] = src[...].astype(BF16)
        else:
            tn = tile[1]
            for j in range(dst.shape[0]):
                dst[j] = src[:, j * tn:(j + 1) * tn].astype(BF16)


def _tiled_bf16(w, tk, tn):
    kdim, n = w.shape
    return w.astype(BF16).reshape(kdim // tk, tk, n // tn, tn).transpose(0, 2, 1, 3)


def _ffn_up_kernel(*refs, tiles):
    n = len(tiles)
    h_ref, r_ref, wg_ref, wu_ref = refs[:4]
    o_ref = refs[4 + n]
    h = h_ref[...]
    r = r_ref[...]
    g = r * jnp.dot(h, wg_ref[...], preferred_element_type=F32)
    u = r * jnp.dot(h, wu_ref[...], preferred_element_type=F32)
    o_ref[...] = (g * jax.nn.sigmoid(g) * u).astype(o_ref.dtype)
    _side_cast_run(tiles, refs[4:4 + n], refs[5 + n:])


def ffn_up(h, r, w_up, *, casts=(), tm=1024, tn=512):
    t, d = h.shape
    f = w_up.shape[1] // 2
    ni, nj = t // tm, f // tn
    c_in, c_out, c_shapes = _side_cast_plan(casts, ni * nj, lambda i, j: i * nj + j)
    return pl.pallas_call(
        functools.partial(_ffn_up_kernel, tiles=tuple(c.tile for c in casts)),
        grid=(ni, nj),
        in_specs=[pl.BlockSpec((tm, d), lambda i, j: (i, 0)),
                  pl.BlockSpec((tm, 1), lambda i, j: (i, 0)),
                  pl.BlockSpec((d, tn), lambda i, j: (0, j)),
                  pl.BlockSpec((d, tn), lambda i, j: (0, j + nj))] + c_in,
        out_specs=[pl.BlockSpec((tm, tn), lambda i, j: (i, j))] + c_out,
        out_shape=[jax.ShapeDtypeStruct((t, f), BF16)] + c_shapes,
        compiler_params=_params("arbitrary", "arbitrary"),
        name="ffn_up",
    )(h, r, w_up, w_up, *[c.src for c in casts])


def _mm_norm_res_kernel(*refs, res_scale, nkk, nj, ni, rows, emit_h):
    if emit_h:
        a_ref, w_ref, x_hbm, pg_ref, ng_ref, xo_ref, ho_ref, ro_ref, y_ref, x_ref, sy_ref, ry_ref, sx_ref, x_sem = refs
    else:
        a_ref, w_ref, x_hbm, pg_ref, xo_ref, y_ref, x_ref, sy_ref, ry_ref, x_sem = refs
    b = pl.program_id(0)
    s = pl.program_id(1)
    n_mm = nkk * nj
    tm, d = y_ref.shape
    tn = w_ref.shape[1]

    def tile_cols(e):
        return pl.ds(pl.multiple_of(e * tn, tn), tn)

    def for_row_chunks(body):
        def step(c, carry):
            body(pl.ds(pl.multiple_of(c * rows, rows), rows))
            return carry
        lax.fori_loop(0, tm // rows, step, 0)

    def x_copy(block):
        return pltpu.make_async_copy(x_hbm.at[pl.ds(pl.multiple_of(block * tm, tm), tm), :], x_ref, x_sem)

    @pl.when((b < ni) & (s == nj))
    def _():
        x_copy(b).start()

    @pl.when((b > 0) & (s == 0))
    def _():
        x_copy(b - 1).wait()
        ry_ref[...] = lax.rsqrt(sy_ref[...] * (1.0 / d) + RMS_EPS)
        if emit_h:
            sx_ref[...] = jnp.zeros_like(sx_ref)

    def residual(r, cols):
        xn = x_ref[r, cols] + res_scale * (y_ref[r, cols] * ry_ref[r, :] * pg_ref[...])
        xo_ref[r, :] = xn
        if emit_h:
            ho_ref[r, :] = (xn * ng_ref[...]).astype(BF16)
            sx_ref[r, :] += jnp.sum(xn * xn, axis=-1, keepdims=True)

    first_k = s < nj
    last_k = s >= n_mm - nj

    @pl.when((b > 0) & (b < ni) & first_k)
    def _():
        cols = tile_cols(s)
        for c in range(tm // rows):
            residual(pl.ds(c * rows, rows), cols)
        y_ref[:, cols] = jnp.dot(a_ref[...], w_ref[...], preferred_element_type=F32)

    @pl.when((b == 0) & first_k)
    def _():
        y_ref[:, tile_cols(s)] = jnp.dot(a_ref[...], w_ref[...], preferred_element_type=F32)

    @pl.when((b == ni) & first_k)
    def _():
        cols = tile_cols(s)
        for_row_chunks(lambda r: residual(r, cols))

    if emit_h:
        @pl.when((b > 0) & (s == nj - 1))
        def _():
            ro_ref[...] = lax.rsqrt(sx_ref[...] * (1.0 / d) + RMS_EPS)

    @pl.when((b < ni) & jnp.logical_not(first_k) & jnp.logical_not(last_k))
    def _():
        y_ref[:, tile_cols(s % nj)] += jnp.dot(a_ref[...], w_ref[...], preferred_element_type=F32)

    @pl.when((b < ni) & last_k)
    def _():
        cols = tile_cols(s % nj)
        y = y_ref[:, cols] + jnp.dot(a_ref[...], w_ref[...], preferred_element_type=F32)
        y_ref[:, cols] = y
        sq = jnp.sum(y * y, axis=-1, keepdims=True)
        sy_ref[...] = jnp.where(s == n_mm - nj, sq, sy_ref[...] + sq)


def mm_norm_res(a, w_tiled, x, post_g, next_g=None, *, res_scale, tm=1024, rows=256):
    t, kdim = a.shape
    nkk, nj, tkk, tn = w_tiled.shape
    d = nj * tn
    assert kdim == nkk * tkk and nkk >= 2
    ni = t // tm
    n_mm = nkk * nj
    emit_h = next_g is not None
    mm_row = lambda b: jnp.minimum(b, ni - 1)
    mm_step = lambda b, s: jnp.where(b < ni, s, n_mm - 1)
    ep_row = lambda b: jnp.maximum(b - 1, 0)
    ep_tile = lambda b, s: jnp.where(b > 0, jnp.minimum(s, nj - 1), 0)
    tile_spec = pl.BlockSpec((tm, tn), lambda b, s: (ep_row(b), ep_tile(b, s)))
    gain_spec = pl.BlockSpec((1, tn), lambda b, s: (0, ep_tile(b, s)))
    operands = [a, w_tiled, x, post_g.reshape(1, d)]
    in_specs = [pl.BlockSpec((tm, tkk), lambda b, s: (mm_row(b), mm_step(b, s) // nj)),
                pl.BlockSpec((None, None, tkk, tn), lambda b, s: (mm_step(b, s) // nj, mm_step(b, s) % nj, 0, 0)),
                pl.BlockSpec(memory_space=pl.ANY), gain_spec]
    out_shape = [jax.ShapeDtypeStruct((t, d), F32)]
    out_specs = [tile_spec]
    row_stat = pltpu.VMEM((tm, 1), F32)
    scratch = [pltpu.VMEM((tm, d), F32), pltpu.VMEM((tm, d), F32), row_stat, row_stat]
    if emit_h:
        operands.append(next_g.reshape(1, d))
        in_specs.append(gain_spec)
        out_shape += [jax.ShapeDtypeStruct((t, d), BF16), jax.ShapeDtypeStruct((t, 1), F32)]
        out_specs += [tile_spec, pl.BlockSpec((tm, 1), lambda b, s: (ep_row(b), 0))]
        scratch.append(row_stat)
    scratch.append(pltpu.SemaphoreType.DMA(()))
    res = pl.pallas_call(
        functools.partial(_mm_norm_res_kernel, res_scale=res_scale, nkk=nkk, nj=nj, ni=ni, rows=rows,
                          emit_h=emit_h),
        grid=(ni + 1, n_mm),
        in_specs=in_specs,
        out_specs=out_specs,
        out_shape=out_shape,
        scratch_shapes=scratch,
        compiler_params=_params("arbitrary", "arbitrary"),
        name="mm_norm_res",
    )(*operands)
    return (res[0], res[1], res[2]) if emit_h else (res[0], None, None)


def _matmul_kernel(a_ref, r_ref, w_ref, o_ref):
    acc = jnp.dot(a_ref[...], w_ref[...], preferred_element_type=F32)
    o_ref[...] = (r_ref[...] * acc).astype(o_ref.dtype)


def matmul(a, r, w, *, tm=1024, tn=1280):
    t, kdim = a.shape
    n = w.shape[1]
    return pl.pallas_call(
        _matmul_kernel,
        grid=(t // tm, n // tn),
        in_specs=[pl.BlockSpec((tm, kdim), lambda i, j: (i, 0)),
                  pl.BlockSpec((tm, 1), lambda i, j: (i, 0)),
                  pl.BlockSpec((kdim, tn), lambda i, j: (0, j))],
        out_specs=pl.BlockSpec((tm, tn), lambda i, j: (i, j)),
        out_shape=jax.ShapeDtypeStruct((t, n), BF16),
        compiler_params=_params("parallel", "arbitrary"),
        name="in_proj",
    )(a, r, w)


def _gelu_exact(x):
    return 0.5 * x * (1.0 + lax.erf(x * (1.0 / math.sqrt(2.0))))


def _gmlp_kernel(z_ref, lng_ref, lnb_ref, ws_ref, bs_ref, o_ref, *, chunks):
    bw = BRANCH_WIDTH
    row = lax.broadcasted_iota(jnp.int32, (GMLP_CHUNK, GMLP_CHUNK), 0)
    col = lax.broadcasted_iota(jnp.int32, (GMLP_CHUNK, GMLP_CHUNK), 1)
    causal = row >= col
    for c in range(chunks):
        r = slice(c * GMLP_CHUNK, (c + 1) * GMLP_CHUNK)
        u = _gelu_exact(z_ref[r, :bw].astype(F32))
        v = _gelu_exact(z_ref[r, bw:].astype(F32))
        mu = jnp.mean(v, axis=-1, keepdims=True)
        vc = v - mu
        var = jnp.mean(vc * vc, axis=-1, keepdims=True)
        vn = (vc * lax.rsqrt(var + LN_EPS) * lng_ref[...] + lnb_ref[...]).astype(BF16)
        for g in range(GMLP_GROUPS):
            gs = slice(g * GMLP_GROUP_DIM, (g + 1) * GMLP_GROUP_DIM)
            w = jnp.where(causal, ws_ref[g], 0.0).astype(BF16)
            mixed = jnp.dot(w, vn[:, gs], preferred_element_type=F32) + bs_ref[:, g:g + 1]
            o_ref[r, gs] = (u[:, gs] * mixed).astype(o_ref.dtype)


def gmlp(proj, ln_g, ln_b, w_s, b_s, *, chunks=4):
    t = proj.shape[0]
    bw = BRANCH_WIDTH
    tm = chunks * GMLP_CHUNK
    return pl.pallas_call(
        functools.partial(_gmlp_kernel, chunks=chunks),
        grid=(t // tm,),
        in_specs=[pl.BlockSpec((tm, 2 * bw), lambda i: (i, 0)),
                  pl.BlockSpec((1, bw), lambda i: (0, 0)),
                  pl.BlockSpec((1, bw), lambda i: (0, 0)),
                  pl.BlockSpec((GMLP_GROUPS, GMLP_CHUNK, GMLP_CHUNK), lambda i: (0, 0, 0)),
                  pl.BlockSpec((GMLP_CHUNK, GMLP_GROUPS), lambda i: (0, 0))],
        out_specs=pl.BlockSpec((tm, bw), lambda i: (i, 0)),
        out_shape=jax.ShapeDtypeStruct((t, bw), BF16),
        compiler_params=_params("parallel"),
        name="gmlp",
    )(proj, ln_g.reshape(1, bw), ln_b.reshape(1, bw), w_s, b_s.T)


MOBA_MAX_BLOCKS = 32
MOBA_CHUNK_BLOCKS = 4
MOBA_Q_TILE_BLOCKS = 2
MOBA_PREP_ROWS = 512
LOG2_E = math.log2(math.e)
FEAT_SEL, FEAT_ONE, FEAT_BLK, FEAT_OFF, FEAT_END = 0, 32, 35, 38, 41


def _moba_key_features(seq):
    pos = jnp.arange(seq, dtype=jnp.int32)[:, None]
    kb, off = pos // MOBA_BLOCK, pos % MOBA_BLOCK
    lane = jnp.arange(LANES, dtype=jnp.int32)[None, :]
    feat = jnp.where(lane < FEAT_ONE, (lane == kb).astype(F32),
                     jnp.where(lane < FEAT_BLK, 1.0,
                               jnp.where(lane < FEAT_OFF, (kb * MOBA_BLOCK).astype(F32),
                                         jnp.where(lane < FEAT_END, off.astype(F32), 0.0))))
    return feat.astype(BF16)


def _split3(x):
    hi = x.astype(BF16).astype(F32)
    mid = (x - hi).astype(BF16).astype(F32)
    return hi, mid, x - hi - mid


def _moba_prepare(slope, q_ref, k_ref, kfeat_ref, kaug_ref, qaug_ref, kmh_ref, kml_ref, *, nb):
    blk = MOBA_BLOCK
    dh = MOBA_HEAD_DIM
    seq = q_ref.shape[0]
    prep = MOBA_PREP_ROWS
    nbp = MOBA_MAX_BLOCKS
    scale = dh ** -0.5 * LOG2_E
    slope = slope * LOG2_E

    kmh_ref[...] = jnp.zeros_like(kmh_ref)
    kml_ref[...] = jnp.zeros_like(kml_ref)
    for n in range(nb):
        rows = slice(n * blk, (n + 1) * blk)
        kn = k_ref[rows, :]
        kaug_ref[rows, :dh] = kn
        kaug_ref[rows, dh:] = kfeat_ref[rows, :]
        km = jnp.mean(kn.astype(F32), axis=0, keepdims=True)
        hi = km.astype(BF16)
        kmh_ref[n:n + 1, :] = hi
        kml_ref[n:n + 1, :] = (km - hi.astype(F32)).astype(BF16)

    blk_id = lax.broadcasted_iota(jnp.int32, (nbp, prep), 0)
    q_off = lax.broadcasted_iota(jnp.int32, (nbp, prep), 1)
    lane = lax.broadcasted_iota(jnp.int32, (prep, LANES), 1)
    row = lax.broadcasted_iota(jnp.int32, (prep, LANES), 0)
    part = jnp.where(lane < FEAT_BLK, lane - FEAT_ONE, jnp.where(lane < FEAT_OFF, lane - FEAT_BLK, lane - FEAT_OFF))
    pick3 = lambda parts: jnp.where(part == 0, parts[0], jnp.where(part == 1, parts[1], parts[2]))
    slope_feat = pick3(_split3(jnp.full((prep, LANES), slope, F32)))

    def rows_chunk(c, carry):
        row0 = pl.multiple_of(c * prep, prep)
        rows = pl.ds(row0, prep)
        q = q_ref[rows, :]
        gate = _dot_nt(kmh_ref[:nbp, :], q) + _dot_nt(kml_ref[:nbp, :], q)
        q_blk = (row0 + q_off) // blk
        gate = jnp.where(blk_id < q_blk, gate, -jnp.inf)
        sel_bias = jnp.where(blk_id == q_blk, 0.0, MASKED)
        for _ in range(MOBA_TOPK):
            best = jnp.max(gate, axis=0, keepdims=True)
            first = jnp.min(jnp.where(gate == best, blk_id, nbp), axis=0, keepdims=True)
            pick = (blk_id == first) & (best > -jnp.inf)
            sel_bias = jnp.where(pick, 0.0, sel_bias)
            gate = jnp.where(pick, -jnp.inf, gate)
        sel_rows = jnp.concatenate([sel_bias, jnp.zeros((LANES - nbp, prep), F32)], axis=0).T

        t_pos = (row0 + row).astype(F32)
        feat = jnp.where(lane < FEAT_ONE, sel_rows,
                         jnp.where(lane < FEAT_BLK, pick3(_split3(-slope * t_pos)),
                                   jnp.where(lane < FEAT_END, slope_feat, 0.0)))
        qaug_ref[rows, :dh] = (q.astype(F32) * scale).astype(BF16)
        qaug_ref[rows, dh:] = feat.astype(BF16)
        return carry

    lax.fori_loop(0, seq // prep, rows_chunk, 0)


def _moba_kernel(slopes_ref, q_ref, k_ref, v_ref, kfeat_ref, o_ref, kaug_ref, qaug_ref, kmh_ref, kml_ref, s_ref,
                 *, nb):
    blk = MOBA_BLOCK
    dh = MOBA_HEAD_DIM
    qt = MOBA_Q_TILE_BLOCKS * blk
    cb = MOBA_CHUNK_BLOCKS
    span = cb * blk
    h = pl.program_id(1)
    ti = pl.program_id(2)
    first_blk = ti * MOBA_Q_TILE_BLOCKS

    @pl.when(ti == 0)
    def _():
        _moba_prepare(slopes_ref[h], q_ref, k_ref, kfeat_ref, kaug_ref, qaug_ref, kmh_ref, kml_ref, nb=nb)

    q_aug = qaug_ref[pl.ds(pl.multiple_of(ti * qt, qt), qt), :]

    def chunk_rows(p):
        return pl.ds(pl.multiple_of(p * span, span), span)

    def scores(p):
        return _dot_nt(q_aug, kaug_ref[chunk_rows(p), :])

    def lanes_max(m, s):
        for c in range(0, s.shape[1], LANES):
            m = jnp.maximum(m, s[:, c:c + LANES])
        return m

    def pass1(p, mvec):
        s = scores(p)
        s_ref[p] = s
        return lanes_max(mvec, s)

    n_past = first_blk // cb
    mvec = lax.fori_loop(0, n_past, pass1, jnp.full((qt, LANES), -jnp.inf, F32))
    r_idx = lax.broadcasted_iota(jnp.int32, (blk, blk), 0)
    c_idx = lax.broadcasted_iota(jnp.int32, (blk, blk), 1)
    causal_bias = jnp.where(r_idx >= c_idx, 0.0, MASKED)
    own = first_blk % cb
    bias = jnp.concatenate(
        [jnp.concatenate([causal_bias * (own + r == c).astype(F32) for c in range(cb)], axis=1)
         for r in range(MOBA_Q_TILE_BLOCKS)], axis=0)
    s = scores(n_past) + bias
    s_ref[n_past] = s
    m = jnp.max(lanes_max(mvec, s), axis=-1, keepdims=True)

    def pass2(p, carry):
        l, acc = carry
        e = jnp.exp2(s_ref[p] - m)
        l = l + jnp.sum(e, axis=-1, keepdims=True)
        acc = acc + jnp.dot(e.astype(BF16), v_ref[chunk_rows(p), :], preferred_element_type=F32)
        return l, acc

    l, acc = lax.fori_loop(0, n_past + 1, pass2,
                           (jnp.zeros((qt, 1), F32), jnp.zeros((qt, dh), F32)))
    o_ref[...] = (acc / l).astype(o_ref.dtype)


def moba(proj, slopes, *, batch, seq):
    blk = MOBA_BLOCK
    nb = seq // blk
    qt = MOBA_Q_TILE_BLOCKS * blk
    nt = seq // qt
    assert nb % MOBA_CHUNK_BLOCKS == 0 and nb <= MOBA_MAX_BLOCKS and seq % MOBA_PREP_ROWS == 0
    assert MOBA_CHUNK_BLOCKS % MOBA_Q_TILE_BLOCKS == 0
    dh = MOBA_HEAD_DIM
    cq, ck, cv = COL_MOBA_Q // dh, COL_MOBA_K // dh, COL_MOBA_V // dh
    return pl.pallas_call(
        functools.partial(_moba_kernel, nb=nb),
        grid_spec=pltpu.PrefetchScalarGridSpec(
            num_scalar_prefetch=1,
            grid=(batch, MOBA_HEADS, nt),
            in_specs=[pl.BlockSpec((seq, dh), lambda b, h, i, s: (b, cq + h)),
                      pl.BlockSpec((seq, dh), lambda b, h, i, s: (b, ck + h)),
                      pl.BlockSpec((seq, dh), lambda b, h, i, s: (b, cv + h)),
                      pl.BlockSpec((seq, LANES), lambda b, h, i, s: (0, 0))],
            out_specs=pl.BlockSpec((qt, dh), lambda b, h, i, s: (b * nt + i, h)),
            scratch_shapes=[pltpu.VMEM((seq, dh + LANES), BF16), pltpu.VMEM((seq, dh + LANES), BF16),
                            pltpu.VMEM((LANES, dh), BF16), pltpu.VMEM((LANES, dh), BF16),
                            pltpu.VMEM((nb // MOBA_CHUNK_BLOCKS, qt, MOBA_CHUNK_BLOCKS * blk), F32)],
        ),
        out_shape=jax.ShapeDtypeStruct((batch * seq, BRANCH_WIDTH), BF16),
        compiler_params=_params("parallel", "parallel", "arbitrary"),
        name="moba",
    )(slopes, proj, proj, proj, _moba_key_features(seq))


def _swa_kernel(slopes_ref, sinks_ref, q_ref, kp_ref, kc_ref, vp_ref, vc_ref, o_ref):
    w = SWA_WINDOW
    dh = SWA_HEAD_DIM
    n = pl.program_id(1)
    scale = dh ** -0.5
    low2 = lax.broadcasted_iota(jnp.int32, (2 * w, LANES), 1) < dh
    t_idx = lax.broadcasted_iota(jnp.int32, (w, 4 * w), 0)
    col = lax.broadcasted_iota(jnp.int32, (w, 4 * w), 1)
    s_idx = col % w
    is_prev = (col % (2 * w)) < w
    second = col >= 2 * w
    dist = (t_idx - s_idx).astype(F32) + jnp.where(is_prev, float(w), 0.0)
    ok = (is_prev & (s_idx > t_idx) & (n > 0)) | (jnp.logical_not(is_prev) & (s_idx <= t_idx))

    def stacked(prev_ref, cur_ref):
        x = jnp.concatenate([prev_ref[...], cur_ref[...]], axis=0).astype(F32)
        xr = pltpu.roll(x, dh, axis=1)
        z = jnp.zeros_like(x)
        head0 = jnp.concatenate([jnp.where(low2, x, z), jnp.where(low2, z, xr)], axis=0)
        head1 = jnp.concatenate([jnp.where(low2, xr, z), jnp.where(low2, z, x)], axis=0)
        return head0.astype(BF16), head1.astype(BF16)

    k_st = stacked(kp_ref, kc_ref)
    v_st = stacked(vp_ref, vc_ref)
    pairs = SWA_Q_HEADS // 2
    group = SWA_Q_HEADS // SWA_KV_HEADS
    for pr in range(pairs):
        q = q_ref[:, pr * LANES:(pr + 1) * LANES]
        kh = (2 * pr) // group
        slope = jnp.where(second, slopes_ref[2 * pr + 1], slopes_ref[2 * pr])
        s = _dot_nt(q, k_st[kh]) * scale
        s = jnp.where(ok, s - slope * dist, MASKED)
        probs = []
        for half in range(2):
            sink = sinks_ref[2 * pr + half]
            sh = s[:, half * 2 * w:(half + 1) * 2 * w]
            m = jnp.maximum(jnp.max(sh, axis=-1, keepdims=True), sink)
            e = jnp.exp(sh - m)
            denom = jnp.sum(e, axis=-1, keepdims=True) + jnp.exp(sink - m)
            probs.append((e * (1.0 / denom)).astype(BF16))
        out = jnp.dot(jnp.concatenate(probs, axis=1), v_st[kh], preferred_element_type=F32)
        o_ref[:, pr * LANES:(pr + 1) * LANES] = out.astype(o_ref.dtype)


def swa(proj, slopes, sinks, *, batch, seq):
    w = SWA_WINDOW
    nblk = seq // w
    bw = BRANCH_WIDTH
    cq = COL_SWA_Q // bw
    ck = COL_SWA_K // SWA_KV_WIDTH
    cv = COL_SWA_V // SWA_KV_WIDTH
    prev = lambda b, n, *_: (b * nblk + jnp.maximum(n - 1, 0))
    cur = lambda b, n, *_: (b * nblk + n)
    return pl.pallas_call(
        _swa_kernel,
        grid_spec=pltpu.PrefetchScalarGridSpec(
            num_scalar_prefetch=2,
            grid=(batch, nblk),
            in_specs=[pl.BlockSpec((w, bw), lambda b, n, *_: (cur(b, n), cq)),
                      pl.BlockSpec((w, SWA_KV_WIDTH), lambda b, n, *_: (prev(b, n), ck)),
                      pl.BlockSpec((w, SWA_KV_WIDTH), lambda b, n, *_: (cur(b, n), ck)),
                      pl.BlockSpec((w, SWA_KV_WIDTH), lambda b, n, *_: (prev(b, n), cv)),
                      pl.BlockSpec((w, SWA_KV_WIDTH), lambda b, n, *_: (cur(b, n), cv))],
            out_specs=pl.BlockSpec((w, bw), lambda b, n, *_: (cur(b, n), 0)),
        ),
        out_shape=jax.ShapeDtypeStruct((batch * seq, bw), BF16),
        compiler_params=_params("parallel", "arbitrary"),
        name="swa",
    )(slopes, sinks, proj, proj, proj, proj, proj)


def _merge_kernel(*refs, tiles):
    n = len(tiles)
    h_ref, r_ref, wg_ref0, wg_ref1, wg_ref2, bg_ref, ya_ref, yb_ref, yc_ref, wb_ref = refs[:10]
    o_ref = refs[10 + n]
    h = h_ref[...]
    r = r_ref[...]
    merged = None
    for br, (wg_ref, y_ref) in enumerate(((wg_ref0, ya_ref), (wg_ref1, yb_ref), (wg_ref2, yc_ref))):
        logits = r * jnp.dot(h, wg_ref[...], preferred_element_type=F32) + bg_ref[br]
        branch = jnp.dot(y_ref[...], wb_ref[br], preferred_element_type=F32)
        term = jax.nn.sigmoid(logits) * branch
        merged = term if merged is None else merged + term
    o_ref[...] = merged.astype(o_ref.dtype)
    _side_cast_run(tiles, refs[10:10 + n], refs[11 + n:])


def merge(h, r, w_gate, b_gate, y_a, y_b, y_c, w_branch, *, casts=(), tm=1024, tn=256):
    t, d = h.shape
    bw = BRANCH_WIDTH
    ni, nj = t // tm, d // tn
    c_in, c_out, c_shapes = _side_cast_plan(casts, ni * nj, lambda i, j: i * nj + j)
    gate_spec = lambda n: pl.BlockSpec((d, tn), lambda i, j: (0, n * nj + j))
    y_spec = pl.BlockSpec((tm, bw), lambda i, j: (i, 0))
    return pl.pallas_call(
        functools.partial(_merge_kernel, tiles=tuple(c.tile for c in casts)),
        grid=(ni, nj),
        in_specs=[pl.BlockSpec((tm, d), lambda i, j: (i, 0)),
                  pl.BlockSpec((tm, 1), lambda i, j: (i, 0)),
                  gate_spec(0), gate_spec(1), gate_spec(2),
                  pl.BlockSpec((N_BRANCHES, 1, tn), lambda i, j: (0, 0, j)),
                  y_spec, y_spec, y_spec,
                  pl.BlockSpec((N_BRANCHES, bw, tn), lambda i, j: (0, 0, j))] + c_in,
        out_specs=[pl.BlockSpec((tm, tn), lambda i, j: (i, j))] + c_out,
        out_shape=[jax.ShapeDtypeStruct((t, d), BF16)] + c_shapes,
        compiler_params=_params("arbitrary", "arbitrary"),
        name="merge",
    )(h, r, w_gate, w_gate, w_gate, b_gate.reshape(N_BRANCHES, 1, d), y_a, y_b, y_c, w_branch,
      *[c.src for c in casts])


def _alibi_slopes():
    i = jnp.arange(1, N_ALIBI_HEADS + 1, dtype=F32)
    s = jnp.exp2(-8.0 * i / N_ALIBI_HEADS)
    return s[:SWA_Q_HEADS], s[SWA_Q_HEADS:]


def kernel(x, ffn1_pre_g, ffn1_w_up, ffn1_w_down, ffn1_post_g, mix_pre_g, w_in, gmlp_ln_g, gmlp_ln_b,
           gmlp_w_s, gmlp_b_s, swa_sinks, w_gate, b_gate, w_branch, w_out, mix_post_g, ffn2_pre_g,
           ffn2_w_up, ffn2_w_down, ffn2_post_g):
    batch, seq, d = x.shape
    depth = ffn1_pre_g.shape[0]
    swa_slopes, moba_slopes = _alibi_slopes()
    xf = x.reshape(batch * seq, d)
    h, r = prenorm(xf, ffn1_pre_g[0])
    steps = (batch * seq // 1024) * 16
    tiled = lambda w, i: _SideCast(w, i, w.shape[1] // steps, MM_TILE)
    plain = lambda w, i, nblk=steps: _SideCast(w, i, w.shape[1] // nblk, None)
    w_branch_rows = w_branch.reshape(depth, -1, d)
    w_up1 = ffn1_w_up[0].astype(BF16)
    for i in range(depth):
        act, w_down1, w_in_b, w_gate_b, w_branch_b, w_out_b = ffn_up(
            h, r, w_up1, casts=(tiled(ffn1_w_down, i), plain(w_in, i), plain(w_gate, i),
                                plain(w_branch_rows, i, steps // 4), tiled(w_out, i)))
        xf, h, r = mm_norm_res(act, w_down1, xf, ffn1_post_g[i], mix_pre_g[i], res_scale=0.5)

        proj = matmul(h, r, w_in_b)
        y_a = gmlp(proj, gmlp_ln_g[i], gmlp_ln_b[i], gmlp_w_s[i], gmlp_b_s[i])
        y_b = moba(proj, moba_slopes, batch=batch, seq=seq)
        y_c = swa(proj, swa_slopes, swa_sinks[i], batch=batch, seq=seq)
        merged, w_up2 = merge(h, r, w_gate_b, b_gate[i], y_a, y_b, y_c, w_branch_b.reshape(w_branch.shape[1:]),
                              casts=(plain(ffn2_w_up, i),))
        xf, h, r = mm_norm_res(merged, w_out_b, xf, mix_post_g[i], ffn2_pre_g[i], res_scale=1.0)

        last = i + 1 == depth
        casts = (tiled(ffn2_w_down, i),) + (() if last else (plain(ffn1_w_up, i + 1),))
        act, w_down2, *nxt = ffn_up(h, r, w_up2, casts=casts)
        xf, h, r = mm_norm_res(act, w_down2, xf, ffn2_post_g[i], None if last else ffn1_pre_g[i + 1],
                               res_scale=0.5)
        if not last:
            w_up1 = nxt[0]
    return xf.reshape(batch, seq, d)
```

```python
import functools
import math
from typing import NamedTuple, Optional, Tuple

import jax
import jax.numpy as jnp
from jax import lax
from jax.experimental import pallas as pl
from jax.experimental.pallas import tpu as pltpu

D_MODEL = 4096
DEPTH = 2
BRANCH_WIDTH = D_MODEL // 4
N_BRANCHES = 3
GMLP_CHUNK = 128
GMLP_GROUP_DIM = 128
GMLP_GROUPS = BRANCH_WIDTH // GMLP_GROUP_DIM
MOBA_HEAD_DIM = 128
MOBA_HEADS = BRANCH_WIDTH // MOBA_HEAD_DIM
MOBA_BLOCK = 256
MOBA_TOPK = 3
SWA_HEAD_DIM = 64
SWA_Q_HEADS = BRANCH_WIDTH // SWA_HEAD_DIM
SWA_KV_HEADS = SWA_Q_HEADS // 8
SWA_WINDOW = 128
SWA_KV_WIDTH = SWA_KV_HEADS * SWA_HEAD_DIM
IN_COLS = 6 * BRANCH_WIDTH + 2 * SWA_KV_WIDTH
D_FF = 2 * D_MODEL
N_ALIBI_HEADS = SWA_Q_HEADS + MOBA_HEADS
RMS_EPS = 1e-6
LN_EPS = 1e-5

LANES = 128
VMEM_LIMIT = 60 * 1024 * 1024
MM_TILE = (2048, 512)
MASKED = -1e30

F32 = jnp.float32
BF16 = jnp.bfloat16

COL_GMLP = 0
COL_MOBA_Q = 2 * BRANCH_WIDTH
COL_MOBA_K = 3 * BRANCH_WIDTH
COL_MOBA_V = 4 * BRANCH_WIDTH
COL_SWA_Q = 5 * BRANCH_WIDTH
COL_SWA_K = 6 * BRANCH_WIDTH
COL_SWA_V = 6 * BRANCH_WIDTH + SWA_KV_WIDTH


def _params(*semantics):
    return pltpu.CompilerParams(dimension_semantics=semantics, vmem_limit_bytes=VMEM_LIMIT)


def _rms_scale(x):
    return lax.rsqrt(jnp.mean(x * x, axis=-1, keepdims=True) + RMS_EPS)


def _dot_nt(a, b):
    return lax.dot_general(a, b, (((1,), (1,)), ((), ())), preferred_element_type=F32)


def _prenorm_kernel(x_ref, g_ref, h_ref, r_ref):
    x = x_ref[...]
    h_ref[...] = (x * g_ref[...]).astype(h_ref.dtype)
    r_ref[...] = _rms_scale(x)


def prenorm(x, g, *, tm=256):
    t, d = x.shape
    return pl.pallas_call(
        _prenorm_kernel,
        grid=(t // tm,),
        in_specs=[pl.BlockSpec((tm, d), lambda i: (i, 0)),
                  pl.BlockSpec((1, d), lambda i: (0, 0))],
        out_specs=[pl.BlockSpec((tm, d), lambda i: (i, 0)),
                   pl.BlockSpec((tm, 1), lambda i: (i, 0))],
        out_shape=[jax.ShapeDtypeStruct((t, d), BF16), jax.ShapeDtypeStruct((t, 1), F32)],
        compiler_params=_params("parallel"),
        name="prenorm",
    )(x, g.reshape(1, d))


class _SideCast(NamedTuple):
    src: jax.Array
    layer: int
    rows: int
    tile: Optional[Tuple[int, int]]


def _side_cast_plan(casts, steps, step_of):
    in_specs, out_specs, out_shapes = [], [], []
    for c in casts:
        _, nrow, ncol = c.src.shape
        nblk = nrow // c.rows
        assert nrow % c.rows == 0 and steps % nblk == 0
        blk_of = lambda *ids, hold=steps // nblk: step_of(*ids) // hold
        in_specs.append(pl.BlockSpec((None, c.rows, ncol), lambda *ids, f=blk_of, layer=c.layer: (layer, f(*ids), 0)))
        if c.tile is None:
            out_specs.append(pl.BlockSpec((c.rows, ncol), lambda *ids, f=blk_of: (f(*ids), 0)))
            out_shapes.append(jax.ShapeDtypeStruct((nrow, ncol), BF16))
        else:
            tk, tn = c.tile
            per = tk // c.rows
            assert tk % c.rows == 0 and nrow % tk == 0 and ncol % tn == 0
            out_specs.append(pl.BlockSpec((None, ncol // tn, c.rows, tn),
                                          lambda *ids, f=blk_of, per=per: (f(*ids) // per, 0, f(*ids) % per, 0)))
            out_shapes.append(jax.ShapeDtypeStruct((nrow // tk, ncol // tn, tk, tn), BF16))
    return in_specs, out_specs, out_shapes


def _side_cast_run(tiles, src_refs, dst_refs):
    for tile, src, dst in zip(tiles, src_refs, dst_refs, strict=True):
        if tile is None:
            dst[...] = src[...].astype(BF16)
        else:
            tn = tile[1]
            for j in range(dst.shape[0]):
                dst[j] = src[:, j * tn:(j + 1) * tn].astype(BF16)


def _tiled_bf16(w, tk, tn):
    kdim, n = w.shape
    return w.astype(BF16).reshape(kdim // tk, tk, n // tn, tn).transpose(0, 2, 1, 3)


def _ffn_up_kernel(*refs, tiles):
    n = len(tiles)
    h_ref, r_ref, wg_ref, wu_ref = refs[:4]
    o_ref = refs[4 + n]
    h = h_ref[...]
    r = r_ref[...]
    g = r * jnp.dot(h, wg_ref[...], preferred_element_type=F32)
    u = r * jnp.dot(h, wu_ref[...], preferred_element_type=F32)
    o_ref[...] = (g * jax.nn.sigmoid(g) * u).astype(o_ref.dtype)
    _side_cast_run(tiles, refs[4:4 + n], refs[5 + n:])


def ffn_up(h, r, w_up, *, casts=(), tm=1024, tn=512):
    t, d = h.shape
    f = w_up.shape[1] // 2
    ni, nj = t // tm, f // tn
    c_in, c_out, c_shapes = _side_cast_plan(casts, ni * nj, lambda i, j: i * nj + j)
    return pl.pallas_call(
        functools.partial(_ffn_up_kernel, tiles=tuple(c.tile for c in casts)),
        grid=(ni, nj),
        in_specs=[pl.BlockSpec((tm, d), lambda i, j: (i, 0)),
                  pl.BlockSpec((tm, 1), lambda i, j: (i, 0)),
                  pl.BlockSpec((d, tn), lambda i, j: (0, j)),
                  pl.BlockSpec((d, tn), lambda i, j: (0, j + nj))] + c_in,
        out_specs=[pl.BlockSpec((tm, tn), lambda i, j: (i, j))] + c_out,
        out_shape=[jax.ShapeDtypeStruct((t, f), BF16)] + c_shapes,
        compiler_params=_params("arbitrary", "arbitrary"),
        name="ffn_up",
    )(h, r, w_up, w_up, *[c.src for c in casts])


def _mm_norm_res_kernel(*refs, res_scale, nkk, nj, ni, rows, emit_h):
    if emit_h:
        a_ref, w_ref, x_ref, pg_ref, ng_ref, xo_ref, ho_ref, ro_ref, y_ref, sy_ref, ry_ref, sx_ref = refs
    else:
        a_ref, w_ref, x_ref, pg_ref, xo_ref, y_ref, sy_ref, ry_ref = refs
    b = pl.program_id(0)
    s = pl.program_id(1)
    n_mm = nkk * nj
    tm, d = y_ref.shape
    tn = w_ref.shape[1]

    def tile_cols(e, width=tn):
        return pl.ds(pl.multiple_of(e * tn, tn), width)

    def for_row_chunks(body):
        def step(c, carry):
            body(pl.ds(pl.multiple_of(c * rows, rows), rows))
            return carry
        lax.fori_loop(0, tm // rows, step, 0)

    @pl.when((b > 0) & (s == 0))
    def _():
        ry_ref[...] = lax.rsqrt(sy_ref[...] * (1.0 / d) + RMS_EPS)
        if emit_h:
            sx_ref[...] = jnp.zeros_like(sx_ref)

    def residual(r, cols):
        xn = x_ref[r, :] + res_scale * (y_ref[r, cols] * ry_ref[r, :] * pg_ref[...])
        xo_ref[r, :] = xn
        if emit_h:
            ho_ref[r, :] = (xn * ng_ref[...]).astype(BF16)
            sx_ref[r, :] += jnp.sum(xn * xn, axis=-1, keepdims=True)

    first_k = s < nj
    last_k = s >= n_mm - nj
    finishing = (b > 0) & first_k & (s % 2 == 0)

    @pl.when(finishing & (b < ni))
    def _():
        for c in range(tm // rows):
            residual(pl.ds(c * rows, rows), tile_cols(s, 2 * tn))
        y_ref[:, tile_cols(s)] = jnp.dot(a_ref[...], w_ref[...], preferred_element_type=F32)

    @pl.when(jnp.logical_not(finishing) & first_k & (b < ni))
    def _():
        y_ref[:, tile_cols(s)] = jnp.dot(a_ref[...], w_ref[...], preferred_element_type=F32)

    @pl.when(finishing & (b == ni))
    def _():
        for_row_chunks(lambda r: residual(r, tile_cols(s, 2 * tn)))

    if emit_h:
        @pl.when((b > 0) & (s == nj - 1))
        def _():
            ro_ref[...] = lax.rsqrt(sx_ref[...] * (1.0 / d) + RMS_EPS)

    @pl.when((b < ni) & jnp.logical_not(first_k) & jnp.logical_not(last_k))
    def _():
        y_ref[:, tile_cols(s % nj)] += jnp.dot(a_ref[...], w_ref[...], preferred_element_type=F32)

    @pl.when((b < ni) & last_k)
    def _():
        cols = tile_cols(s % nj)
        y = y_ref[:, cols] + jnp.dot(a_ref[...], w_ref[...], preferred_element_type=F32)
        y_ref[:, cols] = y
        sq = jnp.sum(y * y, axis=-1, keepdims=True)
        sy_ref[...] = jnp.where(s == n_mm - nj, sq, sy_ref[...] + sq)


def mm_norm_res(a, w_tiled, x, post_g, next_g=None, *, res_scale, tm=1024, rows=256):
    t, kdim = a.shape
    nkk, nj, tkk, tn = w_tiled.shape
    d = nj * tn
    assert kdim == nkk * tkk and nkk >= 2 and nj % 2 == 0
    ni = t // tm
    n_mm = nkk * nj
    emit_h = next_g is not None
    mm_row = lambda b: jnp.minimum(b, ni - 1)
    mm_step = lambda b, s: jnp.where(b < ni, s, n_mm - 1)
    ep_row = lambda b: jnp.maximum(b - 1, 0)
    ep_tile = lambda b, s: jnp.where(b > 0, jnp.minimum(s // 2, nj // 2 - 1), 0)
    tile_spec = pl.BlockSpec((tm, 2 * tn), lambda b, s: (ep_row(b), ep_tile(b, s)))
    gain_spec = pl.BlockSpec((1, 2 * tn), lambda b, s: (0, ep_tile(b, s)))
    operands = [a, w_tiled, x, post_g.reshape(1, d)]
    in_specs = [pl.BlockSpec((tm, tkk), lambda b, s: (mm_row(b), mm_step(b, s) // nj)),
                pl.BlockSpec((None, None, tkk, tn), lambda b, s: (mm_step(b, s) // nj, mm_step(b, s) % nj, 0, 0)),
                tile_spec, gain_spec]
    out_shape = [jax.ShapeDtypeStruct((t, d), F32)]
    out_specs = [tile_spec]
    row_stat = pltpu.VMEM((tm, 1), F32)
    scratch = [pltpu.VMEM((tm, d), F32), row_stat, row_stat]
    if emit_h:
        operands.append(next_g.reshape(1, d))
        in_specs.append(gain_spec)
        out_shape += [jax.ShapeDtypeStruct((t, d), BF16), jax.ShapeDtypeStruct((t, 1), F32)]
        out_specs += [tile_spec, pl.BlockSpec((tm, 1), lambda b, s: (ep_row(b), 0))]
        scratch.append(row_stat)
    res = pl.pallas_call(
        functools.partial(_mm_norm_res_kernel, res_scale=res_scale, nkk=nkk, nj=nj, ni=ni, rows=rows,
                          emit_h=emit_h),
        grid=(ni + 1, n_mm),
        in_specs=in_specs,
        out_specs=out_specs,
        out_shape=out_shape,
        scratch_shapes=scratch,
        compiler_params=_params("arbitrary", "arbitrary"),
        name="mm_norm_res",
    )(*operands)
    return (res[0], res[1], res[2]) if emit_h else (res[0], None, None)


def _matmul_kernel(a_ref, r_ref, w_ref, o_ref):
    acc = jnp.dot(a_ref[...], w_ref[...], preferred_element_type=F32)
    o_ref[...] = (r_ref[...] * acc).astype(o_ref.dtype)


def matmul(a, r, w, *, tm=1024, tn=1280):
    t, kdim = a.shape
    n = w.shape[1]
    return pl.pallas_call(
        _matmul_kernel,
        grid=(t // tm, n // tn),
        in_specs=[pl.BlockSpec((tm, kdim), lambda i, j: (i, 0)),
                  pl.BlockSpec((tm, 1), lambda i, j: (i, 0)),
                  pl.BlockSpec((kdim, tn), lambda i, j: (0, j))],
        out_specs=pl.BlockSpec((tm, tn), lambda i, j: (i, j)),
        out_shape=jax.ShapeDtypeStruct((t, n), BF16),
        compiler_params=_params("parallel", "arbitrary"),
        name="in_proj",
    )(a, r, w)


def _gelu_exact(x):
    return 0.5 * x * (1.0 + lax.erf(x * (1.0 / math.sqrt(2.0))))


def _gmlp_kernel(z_ref, lng_ref, lnb_ref, ws_ref, bs_ref, o_ref, *, chunks):
    bw = BRANCH_WIDTH
    row = lax.broadcasted_iota(jnp.int32, (GMLP_CHUNK, GMLP_CHUNK), 0)
    col = lax.broadcasted_iota(jnp.int32, (GMLP_CHUNK, GMLP_CHUNK), 1)
    causal = row >= col
    for c in range(chunks):
        r = slice(c * GMLP_CHUNK, (c + 1) * GMLP_CHUNK)
        u = _gelu_exact(z_ref[r, :bw].astype(F32))
        v = _gelu_exact(z_ref[r, bw:].astype(F32))
        mu = jnp.mean(v, axis=-1, keepdims=True)
        vc = v - mu
        var = jnp.mean(vc * vc, axis=-1, keepdims=True)
        vn = (vc * lax.rsqrt(var + LN_EPS) * lng_ref[...] + lnb_ref[...]).astype(BF16)
        for g in range(GMLP_GROUPS):
            gs = slice(g * GMLP_GROUP_DIM, (g + 1) * GMLP_GROUP_DIM)
            w = jnp.where(causal, ws_ref[g], 0.0).astype(BF16)
            mixed = jnp.dot(w, vn[:, gs], preferred_element_type=F32) + bs_ref[:, g:g + 1]
            o_ref[r, gs] = (u[:, gs] * mixed).astype(o_ref.dtype)


def gmlp(proj, ln_g, ln_b, w_s, b_s, *, chunks=4):
    t = proj.shape[0]
    bw = BRANCH_WIDTH
    tm = chunks * GMLP_CHUNK
    return pl.pallas_call(
        functools.partial(_gmlp_kernel, chunks=chunks),
        grid=(t // tm,),
        in_specs=[pl.BlockSpec((tm, 2 * bw), lambda i: (i, 0)),
                  pl.BlockSpec((1, bw), lambda i: (0, 0)),
                  pl.BlockSpec((1, bw), lambda i: (0, 0)),
                  pl.BlockSpec((GMLP_GROUPS, GMLP_CHUNK, GMLP_CHUNK), lambda i: (0, 0, 0)),
                  pl.BlockSpec((GMLP_CHUNK, GMLP_GROUPS), lambda i: (0, 0))],
        out_specs=pl.BlockSpec((tm, bw), lambda i: (i, 0)),
        out_shape=jax.ShapeDtypeStruct((t, bw), BF16),
        compiler_params=_params("parallel"),
        name="gmlp",
    )(proj, ln_g.reshape(1, bw), ln_b.reshape(1, bw), w_s, b_s.T)


MOBA_MAX_BLOCKS = 32
MOBA_CHUNK_BLOCKS = 4
MOBA_Q_TILE_BLOCKS = 2
MOBA_PREP_ROWS = 512
LOG2_E = math.log2(math.e)
FEAT_SEL, FEAT_ONE, FEAT_BLK, FEAT_OFF, FEAT_END = 0, 32, 35, 38, 41


def _moba_key_features(seq):
    pos = jnp.arange(seq, dtype=jnp.int32)[:, None]
    kb, off = pos // MOBA_BLOCK, pos % MOBA_BLOCK
    lane = jnp.arange(LANES, dtype=jnp.int32)[None, :]
    feat = jnp.where(lane < FEAT_ONE, (lane == kb).astype(F32),
                     jnp.where(lane < FEAT_BLK, 1.0,
                               jnp.where(lane < FEAT_OFF, (kb * MOBA_BLOCK).astype(F32),
                                         jnp.where(lane < FEAT_END, off.astype(F32), 0.0))))
    return feat.astype(BF16)


def _split3(x):
    hi = x.astype(BF16).astype(F32)
    mid = (x - hi).astype(BF16).astype(F32)
    return hi, mid, x - hi - mid


def _moba_prepare(slope, q_ref, k_ref, kfeat_ref, kaug_ref, qaug_ref, kmh_ref, kml_ref, *, nb):
    blk = MOBA_BLOCK
    dh = MOBA_HEAD_DIM
    seq = q_ref.shape[0]
    prep = MOBA_PREP_ROWS
    nbp = MOBA_MAX_BLOCKS
    scale = dh ** -0.5 * LOG2_E
    slope = slope * LOG2_E

    kmh_ref[...] = jnp.zeros_like(kmh_ref)
    kml_ref[...] = jnp.zeros_like(kml_ref)
    for n in range(nb):
        rows = slice(n * blk, (n + 1) * blk)
        kn = k_ref[rows, :]
        kaug_ref[rows, :dh] = kn
        kaug_ref[rows, dh:] = kfeat_ref[rows, :]
        km = jnp.mean(kn.astype(F32), axis=0, keepdims=True)
        hi = km.astype(BF16)
        kmh_ref[n:n + 1, :] = hi
        kml_ref[n:n + 1, :] = (km - hi.astype(F32)).astype(BF16)

    blk_id = lax.broadcasted_iota(jnp.int32, (nbp, prep), 0)
    q_off = lax.broadcasted_iota(jnp.int32, (nbp, prep), 1)
    lane = lax.broadcasted_iota(jnp.int32, (prep, LANES), 1)
    row = lax.broadcasted_iota(jnp.int32, (prep, LANES), 0)
    part = jnp.where(lane < FEAT_BLK, lane - FEAT_ONE, jnp.where(lane < FEAT_OFF, lane - FEAT_BLK, lane - FEAT_OFF))
    pick3 = lambda parts: jnp.where(part == 0, parts[0], jnp.where(part == 1, parts[1], parts[2]))
    slope_feat = pick3(_split3(jnp.full((prep, LANES), slope, F32)))

    def rows_chunk(c, carry):
        row0 = pl.multiple_of(c * prep, prep)
        rows = pl.ds(row0, prep)
        q = q_ref[rows, :]
        gate = _dot_nt(kmh_ref[:nbp, :], q) + _dot_nt(kml_ref[:nbp, :], q)
        q_blk = (row0 + q_off) // blk
        gate = jnp.where(blk_id < q_blk, gate, -jnp.inf)
        sel_bias = jnp.where(blk_id == q_blk, 0.0, MASKED)
        for _ in range(MOBA_TOPK):
            best = jnp.max(gate, axis=0, keepdims=True)
            first = jnp.min(jnp.where(gate == best, blk_id, nbp), axis=0, keepdims=True)
            pick = (blk_id == first) & (best > -jnp.inf)
            sel_bias = jnp.where(pick, 0.0, sel_bias)
            gate = jnp.where(pick, -jnp.inf, gate)
        sel_rows = jnp.concatenate([sel_bias, jnp.zeros((LANES - nbp, prep), F32)], axis=0).T

        t_pos = (row0 + row).astype(F32)
        feat = jnp.where(lane < FEAT_ONE, sel_rows,
                         jnp.where(lane < FEAT_BLK, pick3(_split3(-slope * t_pos)),
                                   jnp.where(lane < FEAT_END, slope_feat, 0.0)))
        qaug_ref[rows, :dh] = (q.astype(F32) * scale).astype(BF16)
        qaug_ref[rows, dh:] = feat.astype(BF16)
        return carry

    lax.fori_loop(0, seq // prep, rows_chunk, 0)


def _moba_kernel(slopes_ref, q_ref, k_ref, v_ref, kfeat_ref, o_ref, kaug_ref, qaug_ref, kmh_ref, kml_ref, s_ref,
                 *, nb):
    blk = MOBA_BLOCK
    dh = MOBA_HEAD_DIM
    qt = MOBA_Q_TILE_BLOCKS * blk
    cb = MOBA_CHUNK_BLOCKS
    span = cb * blk
    h = pl.program_id(1)
    ti = pl.program_id(2)
    first_blk = ti * MOBA_Q_TILE_BLOCKS

    @pl.when(ti == 0)
    def _():
        _moba_prepare(slopes_ref[h], q_ref, k_ref, kfeat_ref, kaug_ref, qaug_ref, kmh_ref, kml_ref, nb=nb)

    q_aug = qaug_ref[pl.ds(pl.multiple_of(ti * qt, qt), qt), :]

    def chunk_rows(p):
        return pl.ds(pl.multiple_of(p * span, span), span)

    def scores(p):
        return _dot_nt(q_aug, kaug_ref[chunk_rows(p), :])

    def lanes_max(m, s):
        for c in range(0, s.shape[1], LANES):
            m = jnp.maximum(m, s[:, c:c + LANES])
        return m

    def pass1(p, mvec):
        s = scores(p)
        s_ref[p] = s
        return lanes_max(mvec, s)

    def chunk_loop(count, body, init):
        carry = lax.fori_loop(0, count // 2, lambda i, c: body(2 * i + 1, body(2 * i, c)), init)
        return lax.fori_loop(count - count % 2, count, body, carry)

    n_past = first_blk // cb
    mvec = chunk_loop(n_past, pass1, jnp.full((qt, LANES), -jnp.inf, F32))
    r_idx = lax.broadcasted_iota(jnp.int32, (blk, blk), 0)
    c_idx = lax.broadcasted_iota(jnp.int32, (blk, blk), 1)
    causal_bias = jnp.where(r_idx >= c_idx, 0.0, MASKED)
    own = first_blk % cb
    bias = jnp.concatenate(
        [jnp.concatenate([causal_bias * (own + r == c).astype(F32) for c in range(cb)], axis=1)
         for r in range(MOBA_Q_TILE_BLOCKS)], axis=0)
    s = scores(n_past) + bias
    s_ref[n_past] = s
    m = jnp.max(lanes_max(mvec, s), axis=-1, keepdims=True)

    def pass2(p, carry):
        l, acc = carry
        e = jnp.exp2(s_ref[p] - m)
        l = l + jnp.sum(e, axis=-1, keepdims=True)
        acc = acc + jnp.dot(e.astype(BF16), v_ref[chunk_rows(p), :], preferred_element_type=F32)
        return l, acc

    l, acc = chunk_loop(n_past + 1, pass2, (jnp.zeros((qt, 1), F32), jnp.zeros((qt, dh), F32)))
    o_ref[...] = (acc / l).astype(o_ref.dtype)


def moba(proj, slopes, *, batch, seq):
    blk = MOBA_BLOCK
    nb = seq // blk
    qt = MOBA_Q_TILE_BLOCKS * blk
    nt = seq // qt
    assert nb % MOBA_CHUNK_BLOCKS == 0 and nb <= MOBA_MAX_BLOCKS and seq % MOBA_PREP_ROWS == 0
    assert MOBA_CHUNK_BLOCKS % MOBA_Q_TILE_BLOCKS == 0
    dh = MOBA_HEAD_DIM
    cq, ck, cv = COL_MOBA_Q // dh, COL_MOBA_K // dh, COL_MOBA_V // dh
    return pl.pallas_call(
        functools.partial(_moba_kernel, nb=nb),
        grid_spec=pltpu.PrefetchScalarGridSpec(
            num_scalar_prefetch=1,
            grid=(batch, MOBA_HEADS, nt),
            in_specs=[pl.BlockSpec((seq, dh), lambda b, h, i, s: (b, cq + h)),
                      pl.BlockSpec((seq, dh), lambda b, h, i, s: (b, ck + h)),
                      pl.BlockSpec((seq, dh), lambda b, h, i, s: (b, cv + h)),
                      pl.BlockSpec((seq, LANES), lambda b, h, i, s: (0, 0))],
            out_specs=pl.BlockSpec((qt, dh), lambda b, h, i, s: (b * nt + i, h)),
            scratch_shapes=[pltpu.VMEM((seq, dh + LANES), BF16), pltpu.VMEM((seq, dh + LANES), BF16),
                            pltpu.VMEM((LANES, dh), BF16), pltpu.VMEM((LANES, dh), BF16),
                            pltpu.VMEM((nb // MOBA_CHUNK_BLOCKS, qt, MOBA_CHUNK_BLOCKS * blk), F32)],
        ),
        out_shape=jax.ShapeDtypeStruct((batch * seq, BRANCH_WIDTH), BF16),
        compiler_params=_params("parallel", "parallel", "arbitrary"),
        name="moba",
    )(slopes, proj, proj, proj, _moba_key_features(seq))


def _swa_kernel(slopes_ref, sinks_ref, q_ref, kp_ref, kc_ref, vp_ref, vc_ref, o_ref):
    w = SWA_WINDOW
    dh = SWA_HEAD_DIM
    n = pl.program_id(1)
    scale = dh ** -0.5
    low2 = lax.broadcasted_iota(jnp.int32, (2 * w, LANES), 1) < dh
    t_idx = lax.broadcasted_iota(jnp.int32, (w, 4 * w), 0)
    col = lax.broadcasted_iota(jnp.int32, (w, 4 * w), 1)
    s_idx = col % w
    is_prev = (col % (2 * w)) < w
    second = col >= 2 * w
    dist = (t_idx - s_idx).astype(F32) + jnp.where(is_prev, float(w), 0.0)
    ok = (is_prev & (s_idx > t_idx) & (n > 0)) | (jnp.logical_not(is_prev) & (s_idx <= t_idx))

    def stacked(prev_ref, cur_ref):
        x = jnp.concatenate([prev_ref[...], cur_ref[...]], axis=0).astype(F32)
        xr = pltpu.roll(x, dh, axis=1)
        z = jnp.zeros_like(x)
        head0 = jnp.concatenate([jnp.where(low2, x, z), jnp.where(low2, z, xr)], axis=0)
        head1 = jnp.concatenate([jnp.where(low2, xr, z), jnp.where(low2, z, x)], axis=0)
        return head0.astype(BF16), head1.astype(BF16)

    k_st = stacked(kp_ref, kc_ref)
    v_st = stacked(vp_ref, vc_ref)
    pairs = SWA_Q_HEADS // 2
    group = SWA_Q_HEADS // SWA_KV_HEADS
    for pr in range(pairs):
        q = q_ref[:, pr * LANES:(pr + 1) * LANES]
        kh = (2 * pr) // group
        slope = jnp.where(second, slopes_ref[2 * pr + 1], slopes_ref[2 * pr])
        s = _dot_nt(q, k_st[kh]) * scale
        s = jnp.where(ok, s - slope * dist, MASKED)
        probs = []
        for half in range(2):
            sink = sinks_ref[2 * pr + half]
            sh = s[:, half * 2 * w:(half + 1) * 2 * w]
            m = jnp.maximum(jnp.max(sh, axis=-1, keepdims=True), sink)
            e = jnp.exp(sh - m)
            denom = jnp.sum(e, axis=-1, keepdims=True) + jnp.exp(sink - m)
            probs.append((e * (1.0 / denom)).astype(BF16))
        out = jnp.dot(jnp.concatenate(probs, axis=1), v_st[kh], preferred_element_type=F32)
        o_ref[:, pr * LANES:(pr + 1) * LANES] = out.astype(o_ref.dtype)


def swa(proj, slopes, sinks, *, batch, seq):
    w = SWA_WINDOW
    nblk = seq // w
    bw = BRANCH_WIDTH
    cq = COL_SWA_Q // bw
    ck = COL_SWA_K // SWA_KV_WIDTH
    cv = COL_SWA_V // SWA_KV_WIDTH
    prev = lambda b, n, *_: (b * nblk + jnp.maximum(n - 1, 0))
    cur = lambda b, n, *_: (b * nblk + n)
    return pl.pallas_call(
        _swa_kernel,
        grid_spec=pltpu.PrefetchScalarGridSpec(
            num_scalar_prefetch=2,
            grid=(batch, nblk),
            in_specs=[pl.BlockSpec((w, bw), lambda b, n, *_: (cur(b, n), cq)),
                      pl.BlockSpec((w, SWA_KV_WIDTH), lambda b, n, *_: (prev(b, n), ck)),
                      pl.BlockSpec((w, SWA_KV_WIDTH), lambda b, n, *_: (cur(b, n), ck)),
                      pl.BlockSpec((w, SWA_KV_WIDTH), lambda b, n, *_: (prev(b, n), cv)),
                      pl.BlockSpec((w, SWA_KV_WIDTH), lambda b, n, *_: (cur(b, n), cv))],
            out_specs=pl.BlockSpec((w, bw), lambda b, n, *_: (cur(b, n), 0)),
        ),
        out_shape=jax.ShapeDtypeStruct((batch * seq, bw), BF16),
        compiler_params=_params("parallel", "arbitrary"),
        name="swa",
    )(slopes, sinks, proj, proj, proj, proj, proj)


def _merge_kernel(*refs, tiles):
    n = len(tiles)
    h_ref, r_ref, wg_ref0, wg_ref1, wg_ref2, bg_ref, ya_ref, yb_ref, yc_ref, wb_ref = refs[:10]
    o_ref = refs[10 + n]
    h = h_ref[...]
    r = r_ref[...]
    merged = None
    for br, (wg_ref, y_ref) in enumerate(((wg_ref0, ya_ref), (wg_ref1, yb_ref), (wg_ref2, yc_ref))):
        logits = r * jnp.dot(h, wg_ref[...], preferred_element_type=F32) + bg_ref[br]
        branch = jnp.dot(y_ref[...], wb_ref[br], preferred_element_type=F32)
        term = jax.nn.sigmoid(logits) * branch
        merged = term if merged is None else merged + term
    o_ref[...] = merged.astype(o_ref.dtype)
    _side_cast_run(tiles, refs[10:10 + n], refs[11 + n:])


def merge(h, r, w_gate, b_gate, y_a, y_b, y_c, w_branch, *, casts=(), tm=1024, tn=256):
    t, d = h.shape
    bw = BRANCH_WIDTH
    ni, nj = t // tm, d // tn
    c_in, c_out, c_shapes = _side_cast_plan(casts, ni * nj, lambda i, j: i * nj + j)
    gate_spec = lambda n: pl.BlockSpec((d, tn), lambda i, j: (0, n * nj + j))
    y_spec = pl.BlockSpec((tm, bw), lambda i, j: (i, 0))
    return pl.pallas_call(
        functools.partial(_merge_kernel, tiles=tuple(c.tile for c in casts)),
        grid=(ni, nj),
        in_specs=[pl.BlockSpec((tm, d), lambda i, j: (i, 0)),
                  pl.BlockSpec((tm, 1), lambda i, j: (i, 0)),
                  gate_spec(0), gate_spec(1), gate_spec(2),
                  pl.BlockSpec((N_BRANCHES, 1, tn), lambda i, j: (0, 0, j)),
                  y_spec, y_spec, y_spec,
                  pl.BlockSpec((N_BRANCHES, bw, tn), lambda i, j: (0, 0, j))] + c_in,
        out_specs=[pl.BlockSpec((tm, tn), lambda i, j: (i, j))] + c_out,
        out_shape=[jax.ShapeDtypeStruct((t, d), BF16)] + c_shapes,
        compiler_params=_params("arbitrary", "arbitrary"),
        name="merge",
    )(h, r, w_gate, w_gate, w_gate, b_gate.reshape(N_BRANCHES, 1, d), y_a, y_b, y_c, w_branch,
      *[c.src for c in casts])


def _alibi_slopes():
    i = jnp.arange(1, N_ALIBI_HEADS + 1, dtype=F32)
    s = jnp.exp2(-8.0 * i / N_ALIBI_HEADS)
    return s[:SWA_Q_HEADS], s[SWA_Q_HEADS:]


def kernel(x, ffn1_pre_g, ffn1_w_up, ffn1_w_down, ffn1_post_g, mix_pre_g, w_in, gmlp_ln_g, gmlp_ln_b,
           gmlp_w_s, gmlp_b_s, swa_sinks, w_gate, b_gate, w_branch, w_out, mix_post_g, ffn2_pre_g,
           ffn2_w_up, ffn2_w_down, ffn2_post_g):
    batch, seq, d = x.shape
    depth = ffn1_pre_g.shape[0]
    swa_slopes, moba_slopes = _alibi_slopes()
    xf = x.reshape(batch * seq, d)
    h, r = prenorm(xf, ffn1_pre_g[0])
    steps = (batch * seq // 1024) * 16
    tiled = lambda w, i: _SideCast(w, i, w.shape[1] // steps, MM_TILE)
    plain = lambda w, i, nblk=steps: _SideCast(w, i, w.shape[1] // nblk, None)
    w_branch_rows = w_branch.reshape(depth, -1, d)
    w_up1 = ffn1_w_up[0].astype(BF16)
    for i in range(depth):
        act, w_down1, w_in_b, w_gate_b, w_branch_b, w_out_b = ffn_up(
            h, r, w_up1, casts=(tiled(ffn1_w_down, i), plain(w_in, i), plain(w_gate, i),
                                plain(w_branch_rows, i, steps // 4), tiled(w_out, i)))
        xf, h, r = mm_norm_res(act, w_down1, xf, ffn1_post_g[i], mix_pre_g[i], res_scale=0.5)

        proj = matmul(h, r, w_in_b)
        y_a = gmlp(proj, gmlp_ln_g[i], gmlp_ln_b[i], gmlp_w_s[i], gmlp_b_s[i])
        y_b = moba(proj, moba_slopes, batch=batch, seq=seq)
        y_c = swa(proj, swa_slopes, swa_sinks[i], batch=batch, seq=seq)
        merged, w_up2 = merge(h, r, w_gate_b, b_gate[i], y_a, y_b, y_c, w_branch_b.reshape(w_branch.shape[1:]),
                              casts=(plain(ffn2_w_up, i),))
        xf, h, r = mm_norm_res(merged, w_out_b, xf, mix_post_g[i], ffn2_pre_g[i], res_scale=1.0)

        last = i + 1 == depth
        casts = (tiled(ffn2_w_down, i),) + (() if last else (plain(ffn1_w_up, i + 1),))
        act, w_down2, *nxt = ffn_up(h, r, w_up2, casts=casts)
        xf, h, r = mm_norm_res(act, w_down2, xf, ffn2_post_g[i], None if last else ffn1_pre_g[i + 1],
                               res_scale=0.5)
        if not last:
            w_up1 = nxt[0]
    return xf.reshape(batch, seq, d)
```

```python
import functools
import math
from typing import NamedTuple, Optional, Tuple

import jax
import jax.numpy as jnp
from jax import lax
from jax.experimental import pallas as pl
from jax.experimental.pallas import tpu as pltpu

D_MODEL = 4096
DEPTH = 2
BRANCH_WIDTH = D_MODEL // 4
N_BRANCHES = 3
GMLP_CHUNK = 128
GMLP_GROUP_DIM = 128
GMLP_GROUPS = BRANCH_WIDTH // GMLP_GROUP_DIM
MOBA_HEAD_DIM = 128
MOBA_HEADS = BRANCH_WIDTH // MOBA_HEAD_DIM
MOBA_BLOCK = 256
MOBA_TOPK = 3
SWA_HEAD_DIM = 64
SWA_Q_HEADS = BRANCH_WIDTH // SWA_HEAD_DIM
SWA_KV_HEADS = SWA_Q_HEADS // 8
SWA_WINDOW = 128
SWA_KV_WIDTH = SWA_KV_HEADS * SWA_HEAD_DIM
IN_COLS = 6 * BRANCH_WIDTH + 2 * SWA_KV_WIDTH
D_FF = 2 * D_MODEL
N_ALIBI_HEADS = SWA_Q_HEADS + MOBA_HEADS
RMS_EPS = 1e-6
LN_EPS = 1e-5

LANES = 128
VMEM_LIMIT = 60 * 1024 * 1024
MM_TILE_N = 512
MASKED = -1e30

F32 = jnp.float32
BF16 = jnp.bfloat16

COL_GMLP = 0
COL_MOBA_Q = 2 * BRANCH_WIDTH
COL_MOBA_K = 3 * BRANCH_WIDTH
COL_MOBA_V = 4 * BRANCH_WIDTH
COL_SWA_Q = 5 * BRANCH_WIDTH
COL_SWA_K = 6 * BRANCH_WIDTH
COL_SWA_V = 6 * BRANCH_WIDTH + SWA_KV_WIDTH


def _params(*semantics):
    return pltpu.CompilerParams(dimension_semantics=semantics, vmem_limit_bytes=VMEM_LIMIT)


def _rms_scale(x):
    return lax.rsqrt(jnp.mean(x * x, axis=-1, keepdims=True) + RMS_EPS)


def _dot_nt(a, b):
    return lax.dot_general(a, b, (((1,), (1,)), ((), ())), preferred_element_type=F32)


def _prenorm_kernel(x_ref, g_ref, h_ref, r_ref):
    x = x_ref[...]
    h_ref[...] = (x * g_ref[...]).astype(h_ref.dtype)
    r_ref[...] = _rms_scale(x)


def prenorm(x, g, *, tm=256):
    t, d = x.shape
    return pl.pallas_call(
        _prenorm_kernel,
        grid=(t // tm,),
        in_specs=[pl.BlockSpec((tm, d), lambda i: (i, 0)),
                  pl.BlockSpec((1, d), lambda i: (0, 0))],
        out_specs=[pl.BlockSpec((tm, d), lambda i: (i, 0)),
                   pl.BlockSpec((tm, 1), lambda i: (i, 0))],
        out_shape=[jax.ShapeDtypeStruct((t, d), BF16), jax.ShapeDtypeStruct((t, 1), F32)],
        compiler_params=_params("parallel"),
        name="prenorm",
    )(x, g.reshape(1, d))


class _SideCast(NamedTuple):
    src: jax.Array
    layer: int
    rows: int
    tile: Optional[Tuple[int, int]]


def _side_cast_plan(casts, steps, step_of):
    in_specs, out_specs, out_shapes = [], [], []
    for c in casts:
        _, nrow, ncol = c.src.shape
        nblk = nrow // c.rows
        assert nrow % c.rows == 0 and steps % nblk == 0
        blk_of = lambda *ids, hold=steps // nblk: step_of(*ids) // hold
        in_specs.append(pl.BlockSpec((None, c.rows, ncol), lambda *ids, f=blk_of, layer=c.layer: (layer, f(*ids), 0)))
        if c.tile is None:
            out_specs.append(pl.BlockSpec((c.rows, ncol), lambda *ids, f=blk_of: (f(*ids), 0)))
            out_shapes.append(jax.ShapeDtypeStruct((nrow, ncol), BF16))
        else:
            tk, tn = c.tile
            per = tk // c.rows
            assert tk % c.rows == 0 and nrow % tk == 0 and ncol % tn == 0
            out_specs.append(pl.BlockSpec((None, ncol // tn, c.rows, tn),
                                          lambda *ids, f=blk_of, per=per: (f(*ids) // per, 0, f(*ids) % per, 0)))
            out_shapes.append(jax.ShapeDtypeStruct((nrow // tk, ncol // tn, tk, tn), BF16))
    return in_specs, out_specs, out_shapes


def _side_cast_run(tiles, src_refs, dst_refs):
    for tile, src, dst in zip(tiles, src_refs, dst_refs, strict=True):
        if tile is None:
            dst[...] = src[...].astype(BF16)
        else:
            tn = tile[1]
            for j in range(dst.shape[0]):
                dst[j] = src[:, j * tn:(j + 1) * tn].astype(BF16)


def _tiled_bf16(w, tk, tn):
    kdim, n = w.shape
    return w.astype(BF16).reshape(kdim // tk, tk, n // tn, tn).transpose(0, 2, 1, 3)


def _ffn_up_kernel(*refs, tiles):
    n = len(tiles)
    h_ref, r_ref, wg_ref, wu_ref = refs[:4]
    o_ref = refs[4 + n]
    h = h_ref[...]
    r = r_ref[...]
    g = r * jnp.dot(h, wg_ref[...], preferred_element_type=F32)
    u = r * jnp.dot(h, wu_ref[...], preferred_element_type=F32)
    o_ref[...] = (g * jax.nn.sigmoid(g) * u).astype(o_ref.dtype)
    _side_cast_run(tiles, refs[4:4 + n], refs[5 + n:])


def ffn_up(h, r, w_up, *, casts=(), tm=1024, tn=512):
    t, d = h.shape
    f = w_up.shape[1] // 2
    ni, nj = t // tm, f // tn
    c_in, c_out, c_shapes = _side_cast_plan(casts, ni * nj, lambda i, j: i * nj + j)
    return pl.pallas_call(
        functools.partial(_ffn_up_kernel, tiles=tuple(c.tile for c in casts)),
        grid=(ni, nj),
        in_specs=[pl.BlockSpec((tm, d), lambda i, j: (i, 0)),
                  pl.BlockSpec((tm, 1), lambda i, j: (i, 0)),
                  pl.BlockSpec((d, tn), lambda i, j: (0, j)),
                  pl.BlockSpec((d, tn), lambda i, j: (0, j + nj))] + c_in,
        out_specs=[pl.BlockSpec((tm, tn), lambda i, j: (i, j))] + c_out,
        out_shape=[jax.ShapeDtypeStruct((t, f), BF16)] + c_shapes,
        compiler_params=_params("arbitrary", "arbitrary"),
        name="ffn_up",
    )(h, r, w_up, w_up, *[c.src for c in casts])


def _mm_norm_res_kernel(*refs, res_scale, nkk, nj, ni, rows, emit_h):
    if emit_h:
        a_ref, w_ref, x_ref, pg_ref, ng_ref, xo_ref, ho_ref, ro_ref, y_ref, sy_ref, ry_ref, sx_ref = refs
    else:
        a_ref, w_ref, x_ref, pg_ref, xo_ref, y_ref, sy_ref, ry_ref = refs
    b = pl.program_id(0)
    s = pl.program_id(1)
    n_mm = nkk * nj
    tm, d = y_ref.shape
    tn = w_ref.shape[1]

    def tile_cols(e):
        return pl.ds(pl.multiple_of(e * tn, tn), tn)

    def for_row_chunks(body):
        def step(c, carry):
            body(pl.ds(pl.multiple_of(c * rows, rows), rows))
            return carry
        lax.fori_loop(0, tm // rows, step, 0)

    @pl.when((b > 0) & (s == 0))
    def _():
        ry_ref[...] = lax.rsqrt(sy_ref[...] * (1.0 / d) + RMS_EPS)
        if emit_h:
            sx_ref[...] = jnp.zeros_like(sx_ref)

    def residual(r, cols):
        xn = x_ref[r, :] + res_scale * (y_ref[r, cols] * ry_ref[r, :] * pg_ref[...])
        xo_ref[r, :] = xn
        if emit_h:
            ho_ref[r, :] = (xn * ng_ref[...]).astype(BF16)
            sx_ref[r, :] += jnp.sum(xn * xn, axis=-1, keepdims=True)

    first_k = s < nj
    last_k = s >= n_mm - nj

    @pl.when((b > 0) & (b < ni) & first_k)
    def _():
        cols = tile_cols(s)
        for c in range(tm // rows):
            residual(pl.ds(c * rows, rows), cols)
        y_ref[:, cols] = jnp.dot(a_ref[...], w_ref[...], preferred_element_type=F32)

    @pl.when((b == 0) & first_k)
    def _():
        y_ref[:, tile_cols(s)] = jnp.dot(a_ref[...], w_ref[...], preferred_element_type=F32)

    @pl.when((b == ni) & first_k)
    def _():
        cols = tile_cols(s)
        for_row_chunks(lambda r: residual(r, cols))

    if emit_h:
        @pl.when((b > 0) & (s == nj - 1))
        def _():
            ro_ref[...] = lax.rsqrt(sx_ref[...] * (1.0 / d) + RMS_EPS)

    @pl.when((b < ni) & jnp.logical_not(first_k) & jnp.logical_not(last_k))
    def _():
        y_ref[:, tile_cols(s % nj)] += jnp.dot(a_ref[...], w_ref[...], preferred_element_type=F32)

    @pl.when((b < ni) & last_k)
    def _():
        cols = tile_cols(s % nj)
        y = y_ref[:, cols] + jnp.dot(a_ref[...], w_ref[...], preferred_element_type=F32)
        y_ref[:, cols] = y
        sq = jnp.sum(y * y, axis=-1, keepdims=True)
        sy_ref[...] = jnp.where(s == n_mm - nj, sq, sy_ref[...] + sq)


def mm_norm_res(a, w_tiled, x, post_g, next_g=None, *, res_scale, tm=1024, rows=256):
    t, kdim = a.shape
    nkk, nj, tkk, tn = w_tiled.shape
    d = nj * tn
    assert kdim == nkk * tkk and nkk >= 2
    ni = t // tm
    n_mm = nkk * nj
    emit_h = next_g is not None
    mm_row = lambda b: jnp.minimum(b, ni - 1)
    mm_step = lambda b, s: jnp.where(b < ni, s, n_mm - 1)
    ep_row = lambda b: jnp.maximum(b - 1, 0)
    ep_tile = lambda b, s: jnp.where(b > 0, jnp.minimum(s, nj - 1), 0)
    tile_spec = pl.BlockSpec((tm, tn), lambda b, s: (ep_row(b), ep_tile(b, s)))
    gain_spec = pl.BlockSpec((1, tn), lambda b, s: (0, ep_tile(b, s)))
    operands = [a, w_tiled, x, post_g.reshape(1, d)]
    in_specs = [pl.BlockSpec((tm, tkk), lambda b, s: (mm_row(b), mm_step(b, s) // nj)),
                pl.BlockSpec((None, None, tkk, tn), lambda b, s: (mm_step(b, s) // nj, mm_step(b, s) % nj, 0, 0)),
                tile_spec, gain_spec]
    out_shape = [jax.ShapeDtypeStruct((t, d), F32)]
    out_specs = [tile_spec]
    row_stat = pltpu.VMEM((tm, 1), F32)
    scratch = [pltpu.VMEM((tm, d), F32), row_stat, row_stat]
    if emit_h:
        operands.append(next_g.reshape(1, d))
        in_specs.append(gain_spec)
        out_shape += [jax.ShapeDtypeStruct((t, d), BF16), jax.ShapeDtypeStruct((t, 1), F32)]
        out_specs += [tile_spec, pl.BlockSpec((tm, 1), lambda b, s: (ep_row(b), 0))]
        scratch.append(row_stat)
    res = pl.pallas_call(
        functools.partial(_mm_norm_res_kernel, res_scale=res_scale, nkk=nkk, nj=nj, ni=ni, rows=rows,
                          emit_h=emit_h),
        grid=(ni + 1, n_mm),
        in_specs=in_specs,
        out_specs=out_specs,
        out_shape=out_shape,
        scratch_shapes=scratch,
        compiler_params=_params("arbitrary", "arbitrary"),
        name="mm_norm_res",
    )(*operands)
    return (res[0], res[1], res[2]) if emit_h else (res[0], None, None)


def _matmul_kernel(a_ref, r_ref, w_ref, o_ref):
    acc = jnp.dot(a_ref[...], w_ref[...], preferred_element_type=F32)
    o_ref[...] = (r_ref[...] * acc).astype(o_ref.dtype)


def matmul(a, r, w, *, tm=1024, tn=1280):
    t, kdim = a.shape
    n = w.shape[1]
    return pl.pallas_call(
        _matmul_kernel,
        grid=(t // tm, n // tn),
        in_specs=[pl.BlockSpec((tm, kdim), lambda i, j: (i, 0)),
                  pl.BlockSpec((tm, 1), lambda i, j: (i, 0)),
                  pl.BlockSpec((kdim, tn), lambda i, j: (0, j))],
        out_specs=pl.BlockSpec((tm, tn), lambda i, j: (i, j)),
        out_shape=jax.ShapeDtypeStruct((t, n), BF16),
        compiler_params=_params("parallel", "arbitrary"),
        name="in_proj",
    )(a, r, w)


def _gelu_exact(x):
    return 0.5 * x * (1.0 + lax.erf(x * (1.0 / math.sqrt(2.0))))


def _gmlp_kernel(z_ref, lng_ref, lnb_ref, ws_ref, bs_ref, o_ref, *, chunks):
    bw = BRANCH_WIDTH
    row = lax.broadcasted_iota(jnp.int32, (GMLP_CHUNK, GMLP_CHUNK), 0)
    col = lax.broadcasted_iota(jnp.int32, (GMLP_CHUNK, GMLP_CHUNK), 1)
    causal = row >= col
    for c in range(chunks):
        r = slice(c * GMLP_CHUNK, (c + 1) * GMLP_CHUNK)
        u = _gelu_exact(z_ref[r, :bw].astype(F32))
        v = _gelu_exact(z_ref[r, bw:].astype(F32))
        mu = jnp.mean(v, axis=-1, keepdims=True)
        vc = v - mu
        var = jnp.mean(vc * vc, axis=-1, keepdims=True)
        vn = (vc * lax.rsqrt(var + LN_EPS) * lng_ref[...] + lnb_ref[...]).astype(BF16)
        for g in range(GMLP_GROUPS):
            gs = slice(g * GMLP_GROUP_DIM, (g + 1) * GMLP_GROUP_DIM)
            w = jnp.where(causal, ws_ref[g], 0.0).astype(BF16)
            mixed = jnp.dot(w, vn[:, gs], preferred_element_type=F32) + bs_ref[:, g:g + 1]
            o_ref[r, gs] = (u[:, gs] * mixed).astype(o_ref.dtype)


def gmlp(proj, ln_g, ln_b, w_s, b_s, *, chunks=4):
    t = proj.shape[0]
    bw = BRANCH_WIDTH
    tm = chunks * GMLP_CHUNK
    return pl.pallas_call(
        functools.partial(_gmlp_kernel, chunks=chunks),
        grid=(t // tm,),
        in_specs=[pl.BlockSpec((tm, 2 * bw), lambda i: (i, 0)),
                  pl.BlockSpec((1, bw), lambda i: (0, 0)),
                  pl.BlockSpec((1, bw), lambda i: (0, 0)),
                  pl.BlockSpec((GMLP_GROUPS, GMLP_CHUNK, GMLP_CHUNK), lambda i: (0, 0, 0)),
                  pl.BlockSpec((GMLP_CHUNK, GMLP_GROUPS), lambda i: (0, 0))],
        out_specs=pl.BlockSpec((tm, bw), lambda i: (i, 0)),
        out_shape=jax.ShapeDtypeStruct((t, bw), BF16),
        compiler_params=_params("parallel"),
        name="gmlp",
    )(proj, ln_g.reshape(1, bw), ln_b.reshape(1, bw), w_s, b_s.T)


MOBA_MAX_BLOCKS = 32
MOBA_CHUNK_BLOCKS = 4
MOBA_Q_TILE_BLOCKS = 2
MOBA_PREP_ROWS = 512
LOG2_E = math.log2(math.e)
FEAT_SEL, FEAT_ONE, FEAT_BLK, FEAT_OFF, FEAT_END = 0, 32, 35, 38, 41


def _moba_key_features(seq):
    pos = jnp.arange(seq, dtype=jnp.int32)[:, None]
    kb, off = pos // MOBA_BLOCK, pos % MOBA_BLOCK
    lane = jnp.arange(LANES, dtype=jnp.int32)[None, :]
    feat = jnp.where(lane < FEAT_ONE, (lane == kb).astype(F32),
                     jnp.where(lane < FEAT_BLK, 1.0,
                               jnp.where(lane < FEAT_OFF, (kb * MOBA_BLOCK).astype(F32),
                                         jnp.where(lane < FEAT_END, off.astype(F32), 0.0))))
    return feat.astype(BF16)


def _split3(x):
    hi = x.astype(BF16).astype(F32)
    mid = (x - hi).astype(BF16).astype(F32)
    return hi, mid, x - hi - mid


def _moba_prepare(slope, q_ref, k_ref, kfeat_ref, kaug_ref, qaug_ref, kmh_ref, kml_ref, *, nb):
    blk = MOBA_BLOCK
    dh = MOBA_HEAD_DIM
    seq = q_ref.shape[0]
    prep = MOBA_PREP_ROWS
    nbp = MOBA_MAX_BLOCKS
    scale = dh ** -0.5 * LOG2_E
    slope = slope * LOG2_E

    kmh_ref[...] = jnp.zeros_like(kmh_ref)
    kml_ref[...] = jnp.zeros_like(kml_ref)
    for n in range(nb):
        rows = slice(n * blk, (n + 1) * blk)
        kn = k_ref[rows, :]
        kaug_ref[rows, :dh] = kn
        kaug_ref[rows, dh:] = kfeat_ref[rows, :]
        km = jnp.mean(kn.astype(F32), axis=0, keepdims=True)
        hi = km.astype(BF16)
        kmh_ref[n:n + 1, :] = hi
        kml_ref[n:n + 1, :] = (km - hi.astype(F32)).astype(BF16)

    blk_id = lax.broadcasted_iota(jnp.int32, (nbp, prep), 0)
    q_off = lax.broadcasted_iota(jnp.int32, (nbp, prep), 1)
    lane = lax.broadcasted_iota(jnp.int32, (prep, LANES), 1)
    row = lax.broadcasted_iota(jnp.int32, (prep, LANES), 0)
    part = jnp.where(lane < FEAT_BLK, lane - FEAT_ONE, jnp.where(lane < FEAT_OFF, lane - FEAT_BLK, lane - FEAT_OFF))
    pick3 = lambda parts: jnp.where(part == 0, parts[0], jnp.where(part == 1, parts[1], parts[2]))
    slope_feat = pick3(_split3(jnp.full((prep, LANES), slope, F32)))

    def rows_chunk(c, carry):
        row0 = pl.multiple_of(c * prep, prep)
        rows = pl.ds(row0, prep)
        q = q_ref[rows, :]
        gate = _dot_nt(kmh_ref[:nbp, :], q) + _dot_nt(kml_ref[:nbp, :], q)
        q_blk = (row0 + q_off) // blk
        gate = jnp.where(blk_id < q_blk, gate, -jnp.inf)
        sel_bias = jnp.where(blk_id == q_blk, 0.0, MASKED)
        for _ in range(MOBA_TOPK):
            best = jnp.max(gate, axis=0, keepdims=True)
            first = jnp.min(jnp.where(gate == best, blk_id, nbp), axis=0, keepdims=True)
            pick = (blk_id == first) & (best > -jnp.inf)
            sel_bias = jnp.where(pick, 0.0, sel_bias)
            gate = jnp.where(pick, -jnp.inf, gate)
        sel_rows = jnp.concatenate([sel_bias, jnp.zeros((LANES - nbp, prep), F32)], axis=0).T

        t_pos = (row0 + row).astype(F32)
        feat = jnp.where(lane < FEAT_ONE, sel_rows,
                         jnp.where(lane < FEAT_BLK, pick3(_split3(-slope * t_pos)),
                                   jnp.where(lane < FEAT_END, slope_feat, 0.0)))
        qaug_ref[rows, :dh] = (q.astype(F32) * scale).astype(BF16)
        qaug_ref[rows, dh:] = feat.astype(BF16)
        return carry

    lax.fori_loop(0, seq // prep, rows_chunk, 0)


def _moba_kernel(slopes_ref, q_ref, k_ref, v_ref, kfeat_ref, o_ref, kaug_ref, qaug_ref, kmh_ref, kml_ref, s_ref,
                 *, nb):
    blk = MOBA_BLOCK
    dh = MOBA_HEAD_DIM
    qt = MOBA_Q_TILE_BLOCKS * blk
    cb = MOBA_CHUNK_BLOCKS
    span = cb * blk
    h = pl.program_id(1)
    ti = pl.program_id(2)
    first_blk = ti * MOBA_Q_TILE_BLOCKS

    @pl.when(ti == 0)
    def _():
        _moba_prepare(slopes_ref[h], q_ref, k_ref, kfeat_ref, kaug_ref, qaug_ref, kmh_ref, kml_ref, nb=nb)

    q_aug = qaug_ref[pl.ds(pl.multiple_of(ti * qt, qt), qt), :]

    def chunk_rows(p):
        return pl.ds(pl.multiple_of(p * span, span), span)

    def scores(p):
        return _dot_nt(q_aug, kaug_ref[chunk_rows(p), :])

    def lanes_max(m, s):
        for c in range(0, s.shape[1], LANES):
            m = jnp.maximum(m, s[:, c:c + LANES])
        return m

    def pass1(p, mvec):
        s = scores(p)
        s_ref[p] = s
        return lanes_max(mvec, s)

    def chunk_loop(count, body, init):
        carry = lax.fori_loop(0, count // 2, lambda i, c: body(2 * i + 1, body(2 * i, c)), init)
        return lax.fori_loop(count - count % 2, count, body, carry)

    n_past = first_blk // cb
    mvec = chunk_loop(n_past, pass1, jnp.full((qt, LANES), -jnp.inf, F32))
    r_idx = lax.broadcasted_iota(jnp.int32, (blk, blk), 0)
    c_idx = lax.broadcasted_iota(jnp.int32, (blk, blk), 1)
    causal_bias = jnp.where(r_idx >= c_idx, 0.0, MASKED)
    own = first_blk % cb
    bias = jnp.concatenate(
        [jnp.concatenate([causal_bias * (own + r == c).astype(F32) for c in range(cb)], axis=1)
         for r in range(MOBA_Q_TILE_BLOCKS)], axis=0)
    s = scores(n_past) + bias
    s_ref[n_past] = s
    m = jnp.max(lanes_max(mvec, s), axis=-1, keepdims=True)

    def pass2(p, carry):
        l, acc = carry
        e = jnp.exp2(s_ref[p] - m)
        l = l + jnp.sum(e, axis=-1, keepdims=True)
        acc = acc + jnp.dot(e.astype(BF16), v_ref[chunk_rows(p), :], preferred_element_type=F32)
        return l, acc

    l, acc = chunk_loop(n_past + 1, pass2, (jnp.zeros((qt, 1), F32), jnp.zeros((qt, dh), F32)))
    o_ref[...] = (acc / l).astype(o_ref.dtype)


def moba(proj, slopes, *, batch, seq):
    blk = MOBA_BLOCK
    nb = seq // blk
    qt = MOBA_Q_TILE_BLOCKS * blk
    nt = seq // qt
    assert nb % MOBA_CHUNK_BLOCKS == 0 and nb <= MOBA_MAX_BLOCKS and seq % MOBA_PREP_ROWS == 0
    assert MOBA_CHUNK_BLOCKS % MOBA_Q_TILE_BLOCKS == 0
    dh = MOBA_HEAD_DIM
    cq, ck, cv = COL_MOBA_Q // dh, COL_MOBA_K // dh, COL_MOBA_V // dh
    return pl.pallas_call(
        functools.partial(_moba_kernel, nb=nb),
        grid_spec=pltpu.PrefetchScalarGridSpec(
            num_scalar_prefetch=1,
            grid=(batch, MOBA_HEADS, nt),
            in_specs=[pl.BlockSpec((seq, dh), lambda b, h, i, s: (b, cq + h)),
                      pl.BlockSpec((seq, dh), lambda b, h, i, s: (b, ck + h)),
                      pl.BlockSpec((seq, dh), lambda b, h, i, s: (b, cv + h)),
                      pl.BlockSpec((seq, LANES), lambda b, h, i, s: (0, 0))],
            out_specs=pl.BlockSpec((qt, dh), lambda b, h, i, s: (b * nt + i, h)),
            scratch_shapes=[pltpu.VMEM((seq, dh + LANES), BF16), pltpu.VMEM((seq, dh + LANES), BF16),
                            pltpu.VMEM((LANES, dh), BF16), pltpu.VMEM((LANES, dh), BF16),
                            pltpu.VMEM((nb // MOBA_CHUNK_BLOCKS, qt, MOBA_CHUNK_BLOCKS * blk), F32)],
        ),
        out_shape=jax.ShapeDtypeStruct((batch * seq, BRANCH_WIDTH), BF16),
        compiler_params=_params("parallel", "parallel", "arbitrary"),
        name="moba",
    )(slopes, proj, proj, proj, _moba_key_features(seq))


def _swa_kernel(slopes_ref, sinks_ref, q_ref, kp_ref, kc_ref, vp_ref, vc_ref, o_ref):
    w = SWA_WINDOW
    dh = SWA_HEAD_DIM
    n = pl.program_id(1)
    scale = dh ** -0.5
    low2 = lax.broadcasted_iota(jnp.int32, (2 * w, LANES), 1) < dh
    t_idx = lax.broadcasted_iota(jnp.int32, (w, 4 * w), 0)
    col = lax.broadcasted_iota(jnp.int32, (w, 4 * w), 1)
    s_idx = col % w
    is_prev = (col % (2 * w)) < w
    second = col >= 2 * w
    dist = (t_idx - s_idx).astype(F32) + jnp.where(is_prev, float(w), 0.0)
    ok = (is_prev & (s_idx > t_idx) & (n > 0)) | (jnp.logical_not(is_prev) & (s_idx <= t_idx))

    def stacked(prev_ref, cur_ref):
        x = jnp.concatenate([prev_ref[...], cur_ref[...]], axis=0).astype(F32)
        xr = pltpu.roll(x, dh, axis=1)
        z = jnp.zeros_like(x)
        head0 = jnp.concatenate([jnp.where(low2, x, z), jnp.where(low2, z, xr)], axis=0)
        head1 = jnp.concatenate([jnp.where(low2, xr, z), jnp.where(low2, z, x)], axis=0)
        return head0.astype(BF16), head1.astype(BF16)

    k_st = stacked(kp_ref, kc_ref)
    v_st = stacked(vp_ref, vc_ref)
    pairs = SWA_Q_HEADS // 2
    group = SWA_Q_HEADS // SWA_KV_HEADS
    for pr in range(pairs):
        q = q_ref[:, pr * LANES:(pr + 1) * LANES]
        kh = (2 * pr) // group
        slope = jnp.where(second, slopes_ref[2 * pr + 1], slopes_ref[2 * pr])
        s = _dot_nt(q, k_st[kh]) * scale
        s = jnp.where(ok, s - slope * dist, MASKED)
        probs = []
        for half in range(2):
            sink = sinks_ref[2 * pr + half]
            sh = s[:, half * 2 * w:(half + 1) * 2 * w]
            m = jnp.maximum(jnp.max(sh, axis=-1, keepdims=True), sink)
            e = jnp.exp(sh - m)
            denom = jnp.sum(e, axis=-1, keepdims=True) + jnp.exp(sink - m)
            probs.append((e * (1.0 / denom)).astype(BF16))
        out = jnp.dot(jnp.concatenate(probs, axis=1), v_st[kh], preferred_element_type=F32)
        o_ref[:, pr * LANES:(pr + 1) * LANES] = out.astype(o_ref.dtype)


def swa(proj, slopes, sinks, *, batch, seq):
    w = SWA_WINDOW
    nblk = seq // w
    bw = BRANCH_WIDTH
    cq = COL_SWA_Q // bw
    ck = COL_SWA_K // SWA_KV_WIDTH
    cv = COL_SWA_V // SWA_KV_WIDTH
    prev = lambda b, n, *_: (b * nblk + jnp.maximum(n - 1, 0))
    cur = lambda b, n, *_: (b * nblk + n)
    return pl.pallas_call(
        _swa_kernel,
        grid_spec=pltpu.PrefetchScalarGridSpec(
            num_scalar_prefetch=2,
            grid=(batch, nblk),
            in_specs=[pl.BlockSpec((w, bw), lambda b, n, *_: (cur(b, n), cq)),
                      pl.BlockSpec((w, SWA_KV_WIDTH), lambda b, n, *_: (prev(b, n), ck)),
                      pl.BlockSpec((w, SWA_KV_WIDTH), lambda b, n, *_: (cur(b, n), ck)),
                      pl.BlockSpec((w, SWA_KV_WIDTH), lambda b, n, *_: (prev(b, n), cv)),
                      pl.BlockSpec((w, SWA_KV_WIDTH), lambda b, n, *_: (cur(b, n), cv))],
            out_specs=pl.BlockSpec((w, bw), lambda b, n, *_: (cur(b, n), 0)),
        ),
        out_shape=jax.ShapeDtypeStruct((batch * seq, bw), BF16),
        compiler_params=_params("parallel", "arbitrary"),
        name="swa",
    )(slopes, sinks, proj, proj, proj, proj, proj)


def _merge_kernel(*refs, tiles):
    n = len(tiles)
    h_ref, r_ref, wg_ref0, wg_ref1, wg_ref2, bg_ref, ya_ref, yb_ref, yc_ref, wb_ref = refs[:10]
    o_ref = refs[10 + n]
    h = h_ref[...]
    r = r_ref[...]
    merged = None
    for br, (wg_ref, y_ref) in enumerate(((wg_ref0, ya_ref), (wg_ref1, yb_ref), (wg_ref2, yc_ref))):
        logits = r * jnp.dot(h, wg_ref[...], preferred_element_type=F32) + bg_ref[br]
        branch = jnp.dot(y_ref[...], wb_ref[br], preferred_element_type=F32)
        term = jax.nn.sigmoid(logits) * branch
        merged = term if merged is None else merged + term
    o_ref[...] = merged.astype(o_ref.dtype)
    _side_cast_run(tiles, refs[10:10 + n], refs[11 + n:])


def merge(h, r, w_gate, b_gate, y_a, y_b, y_c, w_branch, *, casts=(), tm=1024, tn=256):
    t, d = h.shape
    bw = BRANCH_WIDTH
    ni, nj = t // tm, d // tn
    c_in, c_out, c_shapes = _side_cast_plan(casts, ni * nj, lambda i, j: i * nj + j)
    gate_spec = lambda n: pl.BlockSpec((d, tn), lambda i, j: (0, n * nj + j))
    y_spec = pl.BlockSpec((tm, bw), lambda i, j: (i, 0))
    return pl.pallas_call(
        functools.partial(_merge_kernel, tiles=tuple(c.tile for c in casts)),
        grid=(ni, nj),
        in_specs=[pl.BlockSpec((tm, d), lambda i, j: (i, 0)),
                  pl.BlockSpec((tm, 1), lambda i, j: (i, 0)),
                  gate_spec(0), gate_spec(1), gate_spec(2),
                  pl.BlockSpec((N_BRANCHES, 1, tn), lambda i, j: (0, 0, j)),
                  y_spec, y_spec, y_spec,
                  pl.BlockSpec((N_BRANCHES, bw, tn), lambda i, j: (0, 0, j))] + c_in,
        out_specs=[pl.BlockSpec((tm, tn), lambda i, j: (i, j))] + c_out,
        out_shape=[jax.ShapeDtypeStruct((t, d), BF16)] + c_shapes,
        compiler_params=_params("arbitrary", "arbitrary"),
        name="merge",
    )(h, r, w_gate, w_gate, w_gate, b_gate.reshape(N_BRANCHES, 1, d), y_a, y_b, y_c, w_branch,
      *[c.src for c in casts])


def _alibi_slopes():
    i = jnp.arange(1, N_ALIBI_HEADS + 1, dtype=F32)
    s = jnp.exp2(-8.0 * i / N_ALIBI_HEADS)
    return s[:SWA_Q_HEADS], s[SWA_Q_HEADS:]


def kernel(x, ffn1_pre_g, ffn1_w_up, ffn1_w_down, ffn1_post_g, mix_pre_g, w_in, gmlp_ln_g, gmlp_ln_b,
           gmlp_w_s, gmlp_b_s, swa_sinks, w_gate, b_gate, w_branch, w_out, mix_post_g, ffn2_pre_g,
           ffn2_w_up, ffn2_w_down, ffn2_post_g):
    batch, seq, d = x.shape
    depth = ffn1_pre_g.shape[0]
    swa_slopes, moba_slopes = _alibi_slopes()
    xf = x.reshape(batch * seq, d)
    h, r = prenorm(xf, ffn1_pre_g[0])
    steps = (batch * seq // 1024) * 16
    tiled = lambda w, i: _SideCast(w, i, w.shape[1] // steps, (w.shape[1] // 2, MM_TILE_N))
    plain = lambda w, i, nblk=steps: _SideCast(w, i, w.shape[1] // nblk, None)
    w_branch_rows = w_branch.reshape(depth, -1, d)
    w_up1 = ffn1_w_up[0].astype(BF16)
    for i in range(depth):
        act, w_down1, w_in_b, w_gate_b, w_branch_b, w_out_b = ffn_up(
            h, r, w_up1, casts=(tiled(ffn1_w_down, i), plain(w_in, i), plain(w_gate, i),
                                plain(w_branch_rows, i, steps // 4), tiled(w_out, i)))
        xf, h, r = mm_norm_res(act, w_down1, xf, ffn1_post_g[i], mix_pre_g[i], res_scale=0.5)

        proj = matmul(h, r, w_in_b)
        y_a = gmlp(proj, gmlp_ln_g[i], gmlp_ln_b[i], gmlp_w_s[i], gmlp_b_s[i])
        y_b = moba(proj, moba_slopes, batch=batch, seq=seq)
        y_c = swa(proj, swa_slopes, swa_sinks[i], batch=batch, seq=seq)
        merged, w_up2 = merge(h, r, w_gate_b, b_gate[i], y_a, y_b, y_c, w_branch_b.reshape(w_branch.shape[1:]),
                              casts=(plain(ffn2_w_up, i),))
        xf, h, r = mm_norm_res(merged, w_out_b, xf, mix_post_g[i], ffn2_pre_g[i], res_scale=1.0)

        last = i + 1 == depth
        casts = (tiled(ffn2_w_down, i),) + (() if last else (plain(ffn1_w_up, i + 1),))
        act, w_down2, *nxt = ffn_up(h, r, w_up2, casts=casts)
        xf, h, r = mm_norm_res(act, w_down2, xf, ffn2_post_g[i], None if last else ffn1_pre_g[i + 1],
                               res_scale=0.5)
        if not last:
            w_up1 = nxt[0]
    return xf.reshape(batch, seq, d)
```

```python
import functools
import math
from typing import NamedTuple, Optional, Tuple

import jax
import jax.numpy as jnp
from jax import lax
from jax.experimental import pallas as pl
from jax.experimental.pallas import tpu as pltpu

D_MODEL = 4096
DEPTH = 2
BRANCH_WIDTH = D_MODEL // 4
N_BRANCHES = 3
GMLP_CHUNK = 128
GMLP_GROUP_DIM = 128
GMLP_GROUPS = BRANCH_WIDTH // GMLP_GROUP_DIM
MOBA_HEAD_DIM = 128
MOBA_HEADS = BRANCH_WIDTH // MOBA_HEAD_DIM
MOBA_BLOCK = 256
MOBA_TOPK = 3
SWA_HEAD_DIM = 64
SWA_Q_HEADS = BRANCH_WIDTH // SWA_HEAD_DIM
SWA_KV_HEADS = SWA_Q_HEADS // 8
SWA_WINDOW = 128
SWA_KV_WIDTH = SWA_KV_HEADS * SWA_HEAD_DIM
IN_COLS = 6 * BRANCH_WIDTH + 2 * SWA_KV_WIDTH
D_FF = 2 * D_MODEL
N_ALIBI_HEADS = SWA_Q_HEADS + MOBA_HEADS
RMS_EPS = 1e-6
LN_EPS = 1e-5

LANES = 128
VMEM_LIMIT = 60 * 1024 * 1024
MM_TILE = (4096, 512)
MASKED = -1e30

F32 = jnp.float32
BF16 = jnp.bfloat16

COL_GMLP = 0
COL_MOBA_Q = 2 * BRANCH_WIDTH
COL_MOBA_K = 3 * BRANCH_WIDTH
COL_MOBA_V = 4 * BRANCH_WIDTH
COL_SWA_Q = 5 * BRANCH_WIDTH
COL_SWA_K = 6 * BRANCH_WIDTH
COL_SWA_V = 6 * BRANCH_WIDTH + SWA_KV_WIDTH


def _params(*semantics):
    return pltpu.CompilerParams(dimension_semantics=semantics, vmem_limit_bytes=VMEM_LIMIT)


def _rms_scale(x):
    return lax.rsqrt(jnp.mean(x * x, axis=-1, keepdims=True) + RMS_EPS)


def _dot_nt(a, b):
    return lax.dot_general(a, b, (((1,), (1,)), ((), ())), preferred_element_type=F32)


def _prenorm_kernel(x_ref, g_ref, h_ref, r_ref):
    x = x_ref[...]
    h_ref[...] = (x * g_ref[...]).astype(h_ref.dtype)
    r_ref[...] = _rms_scale(x)


def prenorm(x, g, *, tm=256):
    t, d = x.shape
    return pl.pallas_call(
        _prenorm_kernel,
        grid=(t // tm,),
        in_specs=[pl.BlockSpec((tm, d), lambda i: (i, 0)),
                  pl.BlockSpec((1, d), lambda i: (0, 0))],
        out_specs=[pl.BlockSpec((tm, d), lambda i: (i, 0)),
                   pl.BlockSpec((tm, 1), lambda i: (i, 0))],
        out_shape=[jax.ShapeDtypeStruct((t, d), BF16), jax.ShapeDtypeStruct((t, 1), F32)],
        compiler_params=_params("parallel"),
        name="prenorm",
    )(x, g.reshape(1, d))


class _SideCast(NamedTuple):
    src: jax.Array
    layer: int
    rows: int
    tile: Optional[Tuple[int, int]]


def _side_cast_plan(casts, steps, step_of):
    in_specs, out_specs, out_shapes = [], [], []
    for c in casts:
        _, nrow, ncol = c.src.shape
        nblk = nrow // c.rows
        assert nrow % c.rows == 0 and steps % nblk == 0
        blk_of = lambda *ids, hold=steps // nblk: step_of(*ids) // hold
        in_specs.append(pl.BlockSpec((None, c.rows, ncol), lambda *ids, f=blk_of, layer=c.layer: (layer, f(*ids), 0)))
        if c.tile is None:
            out_specs.append(pl.BlockSpec((c.rows, ncol), lambda *ids, f=blk_of: (f(*ids), 0)))
            out_shapes.append(jax.ShapeDtypeStruct((nrow, ncol), BF16))
        else:
            tk, tn = c.tile
            per = tk // c.rows
            assert tk % c.rows == 0 and nrow % tk == 0 and ncol % tn == 0
            out_specs.append(pl.BlockSpec((None, ncol // tn, c.rows, tn),
                                          lambda *ids, f=blk_of, per=per: (f(*ids) // per, 0, f(*ids) % per, 0)))
            out_shapes.append(jax.ShapeDtypeStruct((nrow // tk, ncol // tn, tk, tn), BF16))
    return in_specs, out_specs, out_shapes


def _side_cast_run(tiles, src_refs, dst_refs):
    for tile, src, dst in zip(tiles, src_refs, dst_refs, strict=True):
        if tile is None:
            dst[...] = src[...].astype(BF16)
        else:
            tn = tile[1]
            for j in range(dst.shape[0]):
                dst[j] = src[:, j * tn:(j + 1) * tn].astype(BF16)


def _tiled_bf16(w, tk, tn):
    kdim, n = w.shape
    return w.astype(BF16).reshape(kdim // tk, tk, n // tn, tn).transpose(0, 2, 1, 3)


def _ffn_up_kernel(*refs, tiles):
    n = len(tiles)
    h_ref, r_ref, wg_ref, wu_ref = refs[:4]
    o_ref = refs[4 + n]
    h = h_ref[...]
    r = r_ref[...]
    g = r * jnp.dot(h, wg_ref[...], preferred_element_type=F32)
    u = r * jnp.dot(h, wu_ref[...], preferred_element_type=F32)
    o_ref[...] = (g * jax.nn.sigmoid(g) * u).astype(o_ref.dtype)
    _side_cast_run(tiles, refs[4:4 + n], refs[5 + n:])


def ffn_up(h, r, w_up, *, casts=(), tm=1024, tn=512):
    t, d = h.shape
    f = w_up.shape[1] // 2
    ni, nj = t // tm, f // tn
    c_in, c_out, c_shapes = _side_cast_plan(casts, ni * nj, lambda i, j: i * nj + j)
    return pl.pallas_call(
        functools.partial(_ffn_up_kernel, tiles=tuple(c.tile for c in casts)),
        grid=(ni, nj),
        in_specs=[pl.BlockSpec((tm, d), lambda i, j: (i, 0)),
                  pl.BlockSpec((tm, 1), lambda i, j: (i, 0)),
                  pl.BlockSpec((d, tn), lambda i, j: (0, j)),
                  pl.BlockSpec((d, tn), lambda i, j: (0, j + nj))] + c_in,
        out_specs=[pl.BlockSpec((tm, tn), lambda i, j: (i, j))] + c_out,
        out_shape=[jax.ShapeDtypeStruct((t, f), BF16)] + c_shapes,
        compiler_params=_params("arbitrary", "arbitrary"),
        name="ffn_up",
    )(h, r, w_up, w_up, *[c.src for c in casts])


def _mm_norm_res_kernel(*refs, res_scale, nkk, nj, ni, rows, emit_h):
    if emit_h:
        a_ref, w_ref, x_ref, pg_ref, ng_ref, xo_ref, ho_ref, ro_ref, y_ref, sy_ref, ry_ref, sx_ref = refs
    else:
        a_ref, w_ref, x_ref, pg_ref, xo_ref, y_ref, sy_ref, ry_ref = refs
    b = pl.program_id(0)
    s = pl.program_id(1)
    n_mm = nkk * nj
    tm, d = y_ref.shape
    tn = w_ref.shape[1]

    def tile_cols(e):
        return pl.ds(pl.multiple_of(e * tn, tn), tn)

    def for_row_chunks(body):
        def step(c, carry):
            body(pl.ds(pl.multiple_of(c * rows, rows), rows))
            return carry
        lax.fori_loop(0, tm // rows, step, 0)

    @pl.when((b > 0) & (s == 0))
    def _():
        ry_ref[...] = lax.rsqrt(sy_ref[...] * (1.0 / d) + RMS_EPS)
        if emit_h:
            sx_ref[...] = jnp.zeros_like(sx_ref)

    def residual(r, cols):
        xn = x_ref[r, :] + res_scale * (y_ref[r, cols] * ry_ref[r, :] * pg_ref[...])
        xo_ref[r, :] = xn
        if emit_h:
            ho_ref[r, :] = (xn * ng_ref[...]).astype(BF16)
            sx_ref[r, :] += jnp.sum(xn * xn, axis=-1, keepdims=True)

    first_k = s < nj
    last_k = s >= n_mm - nj

    def matmul_step(first, last):
        cols = tile_cols(s % nj)
        y = jnp.dot(a_ref[...], w_ref[...], preferred_element_type=F32)
        if not first:
            y = y_ref[:, cols] + y
        y_ref[:, cols] = y
        if last:
            sq = jnp.sum(y * y, axis=-1, keepdims=True)
            sy_ref[...] = jnp.where(s == n_mm - nj, sq, sy_ref[...] + sq)

    @pl.when((b > 0) & (b < ni) & first_k)
    def _():
        for c in range(tm // rows):
            residual(pl.ds(c * rows, rows), tile_cols(s))
        matmul_step(True, nkk == 1)

    @pl.when((b == 0) & first_k)
    def _():
        matmul_step(True, nkk == 1)

    @pl.when((b == ni) & first_k)
    def _():
        for_row_chunks(lambda r: residual(r, tile_cols(s)))

    if emit_h:
        @pl.when((b > 0) & (s == nj - 1))
        def _():
            ro_ref[...] = lax.rsqrt(sx_ref[...] * (1.0 / d) + RMS_EPS)

    if nkk > 2:
        @pl.when((b < ni) & jnp.logical_not(first_k) & jnp.logical_not(last_k))
        def _():
            matmul_step(False, False)

    if nkk > 1:
        @pl.when((b < ni) & last_k)
        def _():
            matmul_step(False, True)


def mm_norm_res(a, w_tiled, x, post_g, next_g=None, *, res_scale, tm=1024, rows=256):
    t, kdim = a.shape
    nkk, nj, tkk, tn = w_tiled.shape
    d = nj * tn
    assert kdim == nkk * tkk
    ni = t // tm
    n_mm = nkk * nj
    emit_h = next_g is not None
    mm_row = lambda b: jnp.minimum(b, ni - 1)
    mm_step = lambda b, s: jnp.where(b < ni, s, n_mm - 1)
    ep_row = lambda b: jnp.maximum(b - 1, 0)
    ep_tile = lambda b, s: jnp.where(b > 0, jnp.minimum(s, nj - 1), 0)
    tile_spec = pl.BlockSpec((tm, tn), lambda b, s: (ep_row(b), ep_tile(b, s)))
    gain_spec = pl.BlockSpec((1, tn), lambda b, s: (0, ep_tile(b, s)))
    operands = [a, w_tiled, x, post_g.reshape(1, d)]
    in_specs = [pl.BlockSpec((tm, tkk), lambda b, s: (mm_row(b), mm_step(b, s) // nj)),
                pl.BlockSpec((None, None, tkk, tn), lambda b, s: (mm_step(b, s) // nj, mm_step(b, s) % nj, 0, 0)),
                tile_spec, gain_spec]
    out_shape = [jax.ShapeDtypeStruct((t, d), F32)]
    out_specs = [tile_spec]
    row_stat = pltpu.VMEM((tm, 1), F32)
    scratch = [pltpu.VMEM((tm, d), F32), row_stat, row_stat]
    if emit_h:
        operands.append(next_g.reshape(1, d))
        in_specs.append(gain_spec)
        out_shape += [jax.ShapeDtypeStruct((t, d), BF16), jax.ShapeDtypeStruct((t, 1), F32)]
        out_specs += [tile_spec, pl.BlockSpec((tm, 1), lambda b, s: (ep_row(b), 0))]
        scratch.append(row_stat)
    res = pl.pallas_call(
        functools.partial(_mm_norm_res_kernel, res_scale=res_scale, nkk=nkk, nj=nj, ni=ni, rows=rows,
                          emit_h=emit_h),
        grid=(ni + 1, n_mm),
        in_specs=in_specs,
        out_specs=out_specs,
        out_shape=out_shape,
        scratch_shapes=scratch,
        compiler_params=_params("arbitrary", "arbitrary"),
        name="mm_norm_res",
    )(*operands)
    return (res[0], res[1], res[2]) if emit_h else (res[0], None, None)


def _matmul_kernel(a_ref, r_ref, w_ref, o_ref):
    acc = jnp.dot(a_ref[...], w_ref[...], preferred_element_type=F32)
    o_ref[...] = (r_ref[...] * acc).astype(o_ref.dtype)


def matmul(a, r, w, *, tm=1024, tn=1280):
    t, kdim = a.shape
    n = w.shape[1]
    return pl.pallas_call(
        _matmul_kernel,
        grid=(t // tm, n // tn),
        in_specs=[pl.BlockSpec((tm, kdim), lambda i, j: (i, 0)),
                  pl.BlockSpec((tm, 1), lambda i, j: (i, 0)),
                  pl.BlockSpec((kdim, tn), lambda i, j: (0, j))],
        out_specs=pl.BlockSpec((tm, tn), lambda i, j: (i, j)),
        out_shape=jax.ShapeDtypeStruct((t, n), BF16),
        compiler_params=_params("parallel", "arbitrary"),
        name="in_proj",
    )(a, r, w)


def _gelu_exact(x):
    return 0.5 * x * (1.0 + lax.erf(x * (1.0 / math.sqrt(2.0))))


def _gmlp_kernel(z_ref, lng_ref, lnb_ref, ws_ref, bs_ref, o_ref, *, chunks):
    bw = BRANCH_WIDTH
    row = lax.broadcasted_iota(jnp.int32, (GMLP_CHUNK, GMLP_CHUNK), 0)
    col = lax.broadcasted_iota(jnp.int32, (GMLP_CHUNK, GMLP_CHUNK), 1)
    causal = row >= col
    for c in range(chunks):
        r = slice(c * GMLP_CHUNK, (c + 1) * GMLP_CHUNK)
        u = _gelu_exact(z_ref[r, :bw].astype(F32))
        v = _gelu_exact(z_ref[r, bw:].astype(F32))
        mu = jnp.mean(v, axis=-1, keepdims=True)
        vc = v - mu
        var = jnp.mean(vc * vc, axis=-1, keepdims=True)
        vn = (vc * lax.rsqrt(var + LN_EPS) * lng_ref[...] + lnb_ref[...]).astype(BF16)
        for g in range(GMLP_GROUPS):
            gs = slice(g * GMLP_GROUP_DIM, (g + 1) * GMLP_GROUP_DIM)
            w = jnp.where(causal, ws_ref[g], 0.0).astype(BF16)
            mixed = jnp.dot(w, vn[:, gs], preferred_element_type=F32) + bs_ref[:, g:g + 1]
            o_ref[r, gs] = (u[:, gs] * mixed).astype(o_ref.dtype)


def gmlp(proj, ln_g, ln_b, w_s, b_s, *, chunks=4):
    t = proj.shape[0]
    bw = BRANCH_WIDTH
    tm = chunks * GMLP_CHUNK
    return pl.pallas_call(
        functools.partial(_gmlp_kernel, chunks=chunks),
        grid=(t // tm,),
        in_specs=[pl.BlockSpec((tm, 2 * bw), lambda i: (i, 0)),
                  pl.BlockSpec((1, bw), lambda i: (0, 0)),
                  pl.BlockSpec((1, bw), lambda i: (0, 0)),
                  pl.BlockSpec((GMLP_GROUPS, GMLP_CHUNK, GMLP_CHUNK), lambda i: (0, 0, 0)),
                  pl.BlockSpec((GMLP_CHUNK, GMLP_GROUPS), lambda i: (0, 0))],
        out_specs=pl.BlockSpec((tm, bw), lambda i: (i, 0)),
        out_shape=jax.ShapeDtypeStruct((t, bw), BF16),
        compiler_params=_params("parallel"),
        name="gmlp",
    )(proj, ln_g.reshape(1, bw), ln_b.reshape(1, bw), w_s, b_s.T)


MOBA_MAX_BLOCKS = 32
MOBA_CHUNK_BLOCKS = 4
MOBA_Q_TILE_BLOCKS = 2
MOBA_PREP_ROWS = 512
LOG2_E = math.log2(math.e)
FEAT_SEL, FEAT_ONE, FEAT_BLK, FEAT_OFF, FEAT_END = 0, 32, 35, 38, 41


def _moba_key_features(seq):
    pos = jnp.arange(seq, dtype=jnp.int32)[:, None]
    kb, off = pos // MOBA_BLOCK, pos % MOBA_BLOCK
    lane = jnp.arange(LANES, dtype=jnp.int32)[None, :]
    feat = jnp.where(lane < FEAT_ONE, (lane == kb).astype(F32),
                     jnp.where(lane < FEAT_BLK, 1.0,
                               jnp.where(lane < FEAT_OFF, (kb * MOBA_BLOCK).astype(F32),
                                         jnp.where(lane < FEAT_END, off.astype(F32), 0.0))))
    return feat.astype(BF16)


def _split3(x):
    hi = x.astype(BF16).astype(F32)
    mid = (x - hi).astype(BF16).astype(F32)
    return hi, mid, x - hi - mid


def _moba_prepare(slope, q_ref, k_ref, kfeat_ref, kaug_ref, qaug_ref, kmh_ref, kml_ref, *, nb):
    blk = MOBA_BLOCK
    dh = MOBA_HEAD_DIM
    seq = q_ref.shape[0]
    prep = MOBA_PREP_ROWS
    nbp = MOBA_MAX_BLOCKS
    scale = dh ** -0.5 * LOG2_E
    slope = slope * LOG2_E

    kmh_ref[...] = jnp.zeros_like(kmh_ref)
    kml_ref[...] = jnp.zeros_like(kml_ref)
    for n in range(nb):
        rows = slice(n * blk, (n + 1) * blk)
        kn = k_ref[rows, :]
        kaug_ref[rows, :dh] = kn
        kaug_ref[rows, dh:] = kfeat_ref[rows, :]
        km = jnp.mean(kn.astype(F32), axis=0, keepdims=True)
        hi = km.astype(BF16)
        kmh_ref[n:n + 1, :] = hi
        kml_ref[n:n + 1, :] = (km - hi.astype(F32)).astype(BF16)

    blk_id = lax.broadcasted_iota(jnp.int32, (nbp, prep), 0)
    q_off = lax.broadcasted_iota(jnp.int32, (nbp, prep), 1)
    lane = lax.broadcasted_iota(jnp.int32, (prep, LANES), 1)
    row = lax.broadcasted_iota(jnp.int32, (prep, LANES), 0)
    part = jnp.where(lane < FEAT_BLK, lane - FEAT_ONE, jnp.where(lane < FEAT_OFF, lane - FEAT_BLK, lane - FEAT_OFF))
    pick3 = lambda parts: jnp.where(part == 0, parts[0], jnp.where(part == 1, parts[1], parts[2]))
    slope_feat = pick3(_split3(jnp.full((prep, LANES), slope, F32)))

    def rows_chunk(c, carry):
        row0 = pl.multiple_of(c * prep, prep)
        rows = pl.ds(row0, prep)
        q = q_ref[rows, :]
        gate = _dot_nt(kmh_ref[:nbp, :], q) + _dot_nt(kml_ref[:nbp, :], q)
        q_blk = (row0 + q_off) // blk
        gate = jnp.where(blk_id < q_blk, gate, -jnp.inf)
        sel_bias = jnp.where(blk_id == q_blk, 0.0, MASKED)
        for _ in range(MOBA_TOPK):
            best = jnp.max(gate, axis=0, keepdims=True)
            first = jnp.min(jnp.where(gate == best, blk_id, nbp), axis=0, keepdims=True)
            pick = (blk_id == first) & (best > -jnp.inf)
            sel_bias = jnp.where(pick, 0.0, sel_bias)
            gate = jnp.where(pick, -jnp.inf, gate)
        sel_rows = jnp.concatenate([sel_bias, jnp.zeros((LANES - nbp, prep), F32)], axis=0).T

        t_pos = (row0 + row).astype(F32)
        feat = jnp.where(lane < FEAT_ONE, sel_rows,
                         jnp.where(lane < FEAT_BLK, pick3(_split3(-slope * t_pos)),
                                   jnp.where(lane < FEAT_END, slope_feat, 0.0)))
        qaug_ref[rows, :dh] = (q.astype(F32) * scale).astype(BF16)
        qaug_ref[rows, dh:] = feat.astype(BF16)
        return carry

    lax.fori_loop(0, seq // prep, rows_chunk, 0)


def _moba_kernel(slopes_ref, q_ref, k_ref, v_ref, kfeat_ref, o_ref, kaug_ref, qaug_ref, kmh_ref, kml_ref, s_ref,
                 *, nb):
    blk = MOBA_BLOCK
    dh = MOBA_HEAD_DIM
    qt = MOBA_Q_TILE_BLOCKS * blk
    cb = MOBA_CHUNK_BLOCKS
    span = cb * blk
    h = pl.program_id(1)
    ti = pl.program_id(2)
    first_blk = ti * MOBA_Q_TILE_BLOCKS

    @pl.when(ti == 0)
    def _():
        _moba_prepare(slopes_ref[h], q_ref, k_ref, kfeat_ref, kaug_ref, qaug_ref, kmh_ref, kml_ref, nb=nb)

    q_aug = qaug_ref[pl.ds(pl.multiple_of(ti * qt, qt), qt), :]

    def chunk_rows(p):
        return pl.ds(pl.multiple_of(p * span, span), span)

    def scores(p):
        return _dot_nt(q_aug, kaug_ref[chunk_rows(p), :])

    def lanes_max(m, s):
        for c in range(0, s.shape[1], LANES):
            m = jnp.maximum(m, s[:, c:c + LANES])
        return m

    def pass1(p, mvec):
        s = scores(p)
        s_ref[p] = s
        return lanes_max(mvec, s)

    def chunk_loop(count, body, init):
        carry = lax.fori_loop(0, count // 2, lambda i, c: body(2 * i + 1, body(2 * i, c)), init)
        return lax.fori_loop(count - count % 2, count, body, carry)

    n_past = first_blk // cb
    mvec = chunk_loop(n_past, pass1, jnp.full((qt, LANES), -jnp.inf, F32))
    r_idx = lax.broadcasted_iota(jnp.int32, (blk, blk), 0)
    c_idx = lax.broadcasted_iota(jnp.int32, (blk, blk), 1)
    causal_bias = jnp.where(r_idx >= c_idx, 0.0, MASKED)
    own = first_blk % cb
    bias = jnp.concatenate(
        [jnp.concatenate([causal_bias * (own + r == c).astype(F32) for c in range(cb)], axis=1)
         for r in range(MOBA_Q_TILE_BLOCKS)], axis=0)
    s = scores(n_past) + bias
    s_ref[n_past] = s
    m = jnp.max(lanes_max(mvec, s), axis=-1, keepdims=True)

    def pass2(p, carry):
        l, acc = carry
        e = jnp.exp2(s_ref[p] - m)
        l = l + jnp.sum(e, axis=-1, keepdims=True)
        acc = acc + jnp.dot(e.astype(BF16), v_ref[chunk_rows(p), :], preferred_element_type=F32)
        return l, acc

    l, acc = chunk_loop(n_past + 1, pass2, (jnp.zeros((qt, 1), F32), jnp.zeros((qt, dh), F32)))
    o_ref[...] = (acc / l).astype(o_ref.dtype)


def moba(proj, slopes, *, batch, seq):
    blk = MOBA_BLOCK
    nb = seq // blk
    qt = MOBA_Q_TILE_BLOCKS * blk
    nt = seq // qt
    assert nb % MOBA_CHUNK_BLOCKS == 0 and nb <= MOBA_MAX_BLOCKS and seq % MOBA_PREP_ROWS == 0
    assert MOBA_CHUNK_BLOCKS % MOBA_Q_TILE_BLOCKS == 0
    dh = MOBA_HEAD_DIM
    cq, ck, cv = COL_MOBA_Q // dh, COL_MOBA_K // dh, COL_MOBA_V // dh
    return pl.pallas_call(
        functools.partial(_moba_kernel, nb=nb),
        grid_spec=pltpu.PrefetchScalarGridSpec(
            num_scalar_prefetch=1,
            grid=(batch, MOBA_HEADS, nt),
            in_specs=[pl.BlockSpec((seq, dh), lambda b, h, i, s: (b, cq + h)),
                      pl.BlockSpec((seq, dh), lambda b, h, i, s: (b, ck + h)),
                      pl.BlockSpec((seq, dh), lambda b, h, i, s: (b, cv + h)),
                      pl.BlockSpec((seq, LANES), lambda b, h, i, s: (0, 0))],
            out_specs=pl.BlockSpec((qt, dh), lambda b, h, i, s: (b * nt + i, h)),
            scratch_shapes=[pltpu.VMEM((seq, dh + LANES), BF16), pltpu.VMEM((seq, dh + LANES), BF16),
                            pltpu.VMEM((LANES, dh), BF16), pltpu.VMEM((LANES, dh), BF16),
                            pltpu.VMEM((nb // MOBA_CHUNK_BLOCKS, qt, MOBA_CHUNK_BLOCKS * blk), F32)],
        ),
        out_shape=jax.ShapeDtypeStruct((batch * seq, BRANCH_WIDTH), BF16),
        compiler_params=_params("parallel", "parallel", "arbitrary"),
        name="moba",
    )(slopes, proj, proj, proj, _moba_key_features(seq))


SWA_STEP_BLOCKS = 2


def _swa_kernel(slopes_ref, sinks_ref, q_ref, *refs):
    nsub = SWA_STEP_BLOCKS
    k_refs, v_refs, o_ref = refs[:nsub + 1], refs[nsub + 1:2 * nsub + 2], refs[2 * nsub + 2]
    w = SWA_WINDOW
    dh = SWA_HEAD_DIM
    n = pl.program_id(1)
    scale = dh ** -0.5
    low2 = lax.broadcasted_iota(jnp.int32, (2 * w, LANES), 1) < dh
    t_idx = lax.broadcasted_iota(jnp.int32, (w, 4 * w), 0)
    col = lax.broadcasted_iota(jnp.int32, (w, 4 * w), 1)
    s_idx = col % w
    is_prev = (col % (2 * w)) < w
    second = col >= 2 * w
    dist = (t_idx - s_idx).astype(F32) + jnp.where(is_prev, float(w), 0.0)
    in_window = (is_prev & (s_idx > t_idx)) | (jnp.logical_not(is_prev) & (s_idx <= t_idx))
    in_window_first = in_window & (jnp.logical_not(is_prev) | (n > 0))

    def stacked(prev_ref, cur_ref):
        x = jnp.concatenate([prev_ref[...], cur_ref[...]], axis=0).astype(F32)
        xr = pltpu.roll(x, dh, axis=1)
        z = jnp.zeros_like(x)
        head0 = jnp.concatenate([jnp.where(low2, x, z), jnp.where(low2, z, xr)], axis=0)
        head1 = jnp.concatenate([jnp.where(low2, xr, z), jnp.where(low2, z, x)], axis=0)
        return head0.astype(BF16), head1.astype(BF16)

    pairs = SWA_Q_HEADS // 2
    group = SWA_Q_HEADS // SWA_KV_HEADS
    for sub in range(nsub):
        rows = slice(sub * w, (sub + 1) * w)
        ok = in_window_first if sub == 0 else in_window
        k_st = stacked(k_refs[sub], k_refs[sub + 1])
        v_st = stacked(v_refs[sub], v_refs[sub + 1])
        for pr in range(pairs):
            q = q_ref[rows, pr * LANES:(pr + 1) * LANES]
            kh = (2 * pr) // group
            slope = jnp.where(second, slopes_ref[2 * pr + 1], slopes_ref[2 * pr])
            s = _dot_nt(q, k_st[kh]) * scale
            s = jnp.where(ok, s - slope * dist, MASKED)
            probs = []
            for half in range(2):
                sink = sinks_ref[2 * pr + half]
                sh = s[:, half * 2 * w:(half + 1) * 2 * w]
                m = jnp.maximum(jnp.max(sh, axis=-1, keepdims=True), sink)
                e = jnp.exp(sh - m)
                denom = jnp.sum(e, axis=-1, keepdims=True) + jnp.exp(sink - m)
                probs.append((e * (1.0 / denom)).astype(BF16))
            out = jnp.dot(jnp.concatenate(probs, axis=1), v_st[kh], preferred_element_type=F32)
            o_ref[rows, pr * LANES:(pr + 1) * LANES] = out.astype(o_ref.dtype)


def swa(proj, slopes, sinks, *, batch, seq):
    w = SWA_WINDOW
    nsub = SWA_STEP_BLOCKS
    nblk = seq // w
    nstep = nblk // nsub
    assert nblk % nsub == 0
    bw = BRANCH_WIDTH
    cq = COL_SWA_Q // bw
    ck = COL_SWA_K // SWA_KV_WIDTH
    cv = COL_SWA_V // SWA_KV_WIDTH

    def kv_spec(col_blk, j):
        return pl.BlockSpec((w, SWA_KV_WIDTH),
                            lambda b, n, *_: (b * nblk + jnp.maximum(n * nsub + j - 1, 0), col_blk))

    return pl.pallas_call(
        _swa_kernel,
        grid_spec=pltpu.PrefetchScalarGridSpec(
            num_scalar_prefetch=2,
            grid=(batch, nstep),
            in_specs=([pl.BlockSpec((nsub * w, bw), lambda b, n, *_: (b * nstep + n, cq))]
                      + [kv_spec(ck, j) for j in range(nsub + 1)]
                      + [kv_spec(cv, j) for j in range(nsub + 1)]),
            out_specs=pl.BlockSpec((nsub * w, bw), lambda b, n, *_: (b * nstep + n, 0)),
        ),
        out_shape=jax.ShapeDtypeStruct((batch * seq, bw), BF16),
        compiler_params=_params("parallel", "arbitrary"),
        name="swa",
    )(slopes, sinks, *([proj] * (2 * nsub + 3)))


def _merge_kernel(*refs, tiles):
    n = len(tiles)
    h_ref, r_ref, wg_ref0, wg_ref1, wg_ref2, bg_ref, ya_ref, yb_ref, yc_ref, wb_ref = refs[:10]
    o_ref = refs[10 + n]
    h = h_ref[...]
    r = r_ref[...]
    merged = None
    for br, (wg_ref, y_ref) in enumerate(((wg_ref0, ya_ref), (wg_ref1, yb_ref), (wg_ref2, yc_ref))):
        logits = r * jnp.dot(h, wg_ref[...], preferred_element_type=F32) + bg_ref[br]
        branch = jnp.dot(y_ref[...], wb_ref[br], preferred_element_type=F32)
        term = jax.nn.sigmoid(logits) * branch
        merged = term if merged is None else merged + term
    o_ref[...] = merged.astype(o_ref.dtype)
    _side_cast_run(tiles, refs[10:10 + n], refs[11 + n:])


def merge(h, r, w_gate, b_gate, y_a, y_b, y_c, w_branch, *, casts=(), tm=1024, tn=256):
    t, d = h.shape
    bw = BRANCH_WIDTH
    ni, nj = t // tm, d // tn
    c_in, c_out, c_shapes = _side_cast_plan(casts, ni * nj, lambda i, j: i * nj + j)
    gate_spec = lambda n: pl.BlockSpec((d, tn), lambda i, j: (0, n * nj + j))
    y_spec = pl.BlockSpec((tm, bw), lambda i, j: (i, 0))
    return pl.pallas_call(
        functools.partial(_merge_kernel, tiles=tuple(c.tile for c in casts)),
        grid=(ni, nj),
        in_specs=[pl.BlockSpec((tm, d), lambda i, j: (i, 0)),
                  pl.BlockSpec((tm, 1), lambda i, j: (i, 0)),
                  gate_spec(0), gate_spec(1), gate_spec(2),
                  pl.BlockSpec((N_BRANCHES, 1, tn), lambda i, j: (0, 0, j)),
                  y_spec, y_spec, y_spec,
                  pl.BlockSpec((N_BRANCHES, bw, tn), lambda i, j: (0, 0, j))] + c_in,
        out_specs=[pl.BlockSpec((tm, tn), lambda i, j: (i, j))] + c_out,
        out_shape=[jax.ShapeDtypeStruct((t, d), BF16)] + c_shapes,
        compiler_params=_params("arbitrary", "arbitrary"),
        name="merge",
    )(h, r, w_gate, w_gate, w_gate, b_gate.reshape(N_BRANCHES, 1, d), y_a, y_b, y_c, w_branch,
      *[c.src for c in casts])


def _alibi_slopes():
    i = jnp.arange(1, N_ALIBI_HEADS + 1, dtype=F32)
    s = jnp.exp2(-8.0 * i / N_ALIBI_HEADS)
    return s[:SWA_Q_HEADS], s[SWA_Q_HEADS:]


def kernel(x, ffn1_pre_g, ffn1_w_up, ffn1_w_down, ffn1_post_g, mix_pre_g, w_in, gmlp_ln_g, gmlp_ln_b,
           gmlp_w_s, gmlp_b_s, swa_sinks, w_gate, b_gate, w_branch, w_out, mix_post_g, ffn2_pre_g,
           ffn2_w_up, ffn2_w_down, ffn2_post_g):
    batch, seq, d = x.shape
    depth = ffn1_pre_g.shape[0]
    swa_slopes, moba_slopes = _alibi_slopes()
    xf = x.reshape(batch * seq, d)
    h, r = prenorm(xf, ffn1_pre_g[0])
    steps = (batch * seq // 1024) * 16
    tiled = lambda w, i: _SideCast(w, i, w.shape[1] // steps, MM_TILE)
    plain = lambda w, i, nblk=steps: _SideCast(w, i, w.shape[1] // nblk, None)
    w_branch_rows = w_branch.reshape(depth, -1, d)
    w_up1 = ffn1_w_up[0].astype(BF16)
    for i in range(depth):
        act, w_down1, w_in_b, w_gate_b, w_branch_b, w_out_b = ffn_up(
            h, r, w_up1, casts=(tiled(ffn1_w_down, i), plain(w_in, i), plain(w_gate, i),
                                plain(w_branch_rows, i, steps // 4), tiled(w_out, i)))
        xf, h, r = mm_norm_res(act, w_down1, xf, ffn1_post_g[i], mix_pre_g[i], res_scale=0.5)

        proj = matmul(h, r, w_in_b)
        y_a = gmlp(proj, gmlp_ln_g[i], gmlp_ln_b[i], gmlp_w_s[i], gmlp_b_s[i])
        y_b = moba(proj, moba_slopes, batch=batch, seq=seq)
        y_c = swa(proj, swa_slopes, swa_sinks[i], batch=batch, seq=seq)
        merged, w_up2 = merge(h, r, w_gate_b, b_gate[i], y_a, y_b, y_c, w_branch_b.reshape(w_branch.shape[1:]),
                              casts=(plain(ffn2_w_up, i),))
        xf, h, r = mm_norm_res(merged, w_out_b, xf, mix_post_g[i], ffn2_pre_g[i], res_scale=1.0)

        last = i + 1 == depth
        casts = (tiled(ffn2_w_down, i),) + (() if last else (plain(ffn1_w_up, i + 1),))
        act, w_down2, *nxt = ffn_up(h, r, w_up2, casts=casts)
        xf, h, r = mm_norm_res(act, w_down2, xf, ffn2_post_g[i], None if last else ffn1_pre_g[i + 1],
                               res_scale=0.5)
        if not last:
            w_up1 = nxt[0]
    return xf.reshape(batch, seq, d)
```

```python
import functools
import math
from typing import NamedTuple, Optional, Tuple

import jax
import jax.numpy as jnp
from jax import lax
from jax.experimental import pallas as pl
from jax.experimental.pallas import tpu as pltpu

D_MODEL = 4096
DEPTH = 2
BRANCH_WIDTH = D_MODEL // 4
N_BRANCHES = 3
GMLP_CHUNK = 128
GMLP_GROUP_DIM = 128
GMLP_GROUPS = BRANCH_WIDTH // GMLP_GROUP_DIM
MOBA_HEAD_DIM = 128
MOBA_HEADS = BRANCH_WIDTH // MOBA_HEAD_DIM
MOBA_BLOCK = 256
MOBA_TOPK = 3
SWA_HEAD_DIM = 64
SWA_Q_HEADS = BRANCH_WIDTH // SWA_HEAD_DIM
SWA_KV_HEADS = SWA_Q_HEADS // 8
SWA_WINDOW = 128
SWA_KV_WIDTH = SWA_KV_HEADS * SWA_HEAD_DIM
IN_COLS = 6 * BRANCH_WIDTH + 2 * SWA_KV_WIDTH
D_FF = 2 * D_MODEL
N_ALIBI_HEADS = SWA_Q_HEADS + MOBA_HEADS
RMS_EPS = 1e-6
LN_EPS = 1e-5

LANES = 128
BF16_SUBLANES = 16
VMEM_LIMIT = 60 * 1024 * 1024
MM_TILE = (4096, 512)
MASKED = -1e30

F32 = jnp.float32
BF16 = jnp.bfloat16

COL_GMLP = 0
COL_MOBA_Q = 2 * BRANCH_WIDTH
COL_MOBA_K = 3 * BRANCH_WIDTH
COL_MOBA_V = 4 * BRANCH_WIDTH
COL_SWA_Q = 5 * BRANCH_WIDTH
COL_SWA_K = 6 * BRANCH_WIDTH
COL_SWA_V = 6 * BRANCH_WIDTH + SWA_KV_WIDTH


def _params(*semantics):
    return pltpu.CompilerParams(dimension_semantics=semantics, vmem_limit_bytes=VMEM_LIMIT)


def _rms_scale(x):
    return lax.rsqrt(jnp.mean(x * x, axis=-1, keepdims=True) + RMS_EPS)


def _dot_nt(a, b):
    return lax.dot_general(a, b, (((1,), (1,)), ((), ())), preferred_element_type=F32)


def _prenorm_kernel(x_ref, g_ref, h_ref, r_ref):
    x = x_ref[...]
    h_ref[...] = (x * g_ref[...]).astype(h_ref.dtype)
    r_ref[...] = _rms_scale(x)


def prenorm(x, g, *, tm=256):
    t, d = x.shape
    return pl.pallas_call(
        _prenorm_kernel,
        grid=(t // tm,),
        in_specs=[pl.BlockSpec((tm, d), lambda i: (i, 0)),
                  pl.BlockSpec((1, d), lambda i: (0, 0))],
        out_specs=[pl.BlockSpec((tm, d), lambda i: (i, 0)),
                   pl.BlockSpec((tm, 1), lambda i: (i, 0))],
        out_shape=[jax.ShapeDtypeStruct((t, d), BF16), jax.ShapeDtypeStruct((t, 1), F32)],
        compiler_params=_params("parallel"),
        name="prenorm",
    )(x, g.reshape(1, d))


class _SideCast(NamedTuple):
    src: jax.Array
    layer: int
    tile: Optional[Tuple[int, int]]


def _side_cast_plan(casts, steps, step_of):
    in_specs, out_specs, out_shapes = [], [], []
    for c in casts:
        _, nrow, ncol = c.src.shape
        nblk = steps
        while nrow % nblk or (nrow // nblk) % BF16_SUBLANES:
            assert nblk % 2 == 0, (nrow, steps)
            nblk //= 2
        rows = nrow // nblk
        blk_of = lambda *ids, hold=steps // nblk: step_of(*ids) // hold
        in_specs.append(pl.BlockSpec((None, rows, ncol), lambda *ids, f=blk_of, layer=c.layer: (layer, f(*ids), 0)))
        if c.tile is None:
            out_specs.append(pl.BlockSpec((rows, ncol), lambda *ids, f=blk_of: (f(*ids), 0)))
            out_shapes.append(jax.ShapeDtypeStruct((nrow, ncol), BF16))
        else:
            tk, tn = c.tile
            per = tk // rows
            assert tk % rows == 0 and nrow % tk == 0 and ncol % tn == 0
            out_specs.append(pl.BlockSpec((None, ncol // tn, rows, tn),
                                          lambda *ids, f=blk_of, per=per: (f(*ids) // per, 0, f(*ids) % per, 0)))
            out_shapes.append(jax.ShapeDtypeStruct((nrow // tk, ncol // tn, tk, tn), BF16))
    return in_specs, out_specs, out_shapes


def _side_cast_run(tiles, src_refs, dst_refs):
    for tile, src, dst in zip(tiles, src_refs, dst_refs, strict=True):
        if tile is None:
            dst[...] = src[...].astype(BF16)
        else:
            tn = tile[1]
            for j in range(dst.shape[0]):
                dst[j] = src[:, j * tn:(j + 1) * tn].astype(BF16)


def _ffn_up_kernel(*refs, tiles):
    n = len(tiles)
    h_ref, r_ref, wg_ref, wu_ref = refs[:4]
    o_ref = refs[4 + n]
    h = h_ref[...]
    r = r_ref[...]
    g = r * jnp.dot(h, wg_ref[...], preferred_element_type=F32)
    u = r * jnp.dot(h, wu_ref[...], preferred_element_type=F32)
    o_ref[...] = (g * jax.nn.sigmoid(g) * u).astype(o_ref.dtype)
    _side_cast_run(tiles, refs[4:4 + n], refs[5 + n:])


def ffn_up(h, r, w_up, *, casts=(), tm=1024, tn=512):
    t, d = h.shape
    f = w_up.shape[1] // 2
    ni, nj = t // tm, f // tn
    c_in, c_out, c_shapes = _side_cast_plan(casts, ni * nj, lambda i, j: i * nj + j)
    return pl.pallas_call(
        functools.partial(_ffn_up_kernel, tiles=tuple(c.tile for c in casts)),
        grid=(ni, nj),
        in_specs=[pl.BlockSpec((tm, d), lambda i, j: (i, 0)),
                  pl.BlockSpec((tm, 1), lambda i, j: (i, 0)),
                  pl.BlockSpec((d, tn), lambda i, j: (0, j)),
                  pl.BlockSpec((d, tn), lambda i, j: (0, j + nj))] + c_in,
        out_specs=[pl.BlockSpec((tm, tn), lambda i, j: (i, j))] + c_out,
        out_shape=[jax.ShapeDtypeStruct((t, f), BF16)] + c_shapes,
        compiler_params=_params("arbitrary", "arbitrary"),
        name="ffn_up",
    )(h, r, w_up, w_up, *[c.src for c in casts])


def _mm_norm_res_kernel(*refs, res_scale, nkk, nj, ni, rows, emit_h):
    if emit_h:
        a_ref, w_ref, x_ref, pg_ref, ng_ref, xo_ref, ho_ref, ro_ref, y_ref, sy_ref, ry_ref, sx_ref = refs
    else:
        a_ref, w_ref, x_ref, pg_ref, xo_ref, y_ref, sy_ref, ry_ref = refs
    b = pl.program_id(0)
    s = pl.program_id(1)
    n_mm = nkk * nj
    tm, d = y_ref.shape
    tn = w_ref.shape[1]

    def tile_cols(e):
        return pl.ds(pl.multiple_of(e * tn, tn), tn)

    def for_row_chunks(body):
        def step(c, carry):
            body(pl.ds(pl.multiple_of(c * rows, rows), rows))
            return carry
        lax.fori_loop(0, tm // rows, step, 0)

    @pl.when((b > 0) & (s == 0))
    def _():
        ry_ref[...] = lax.rsqrt(sy_ref[...] * (1.0 / d) + RMS_EPS)
        if emit_h:
            sx_ref[...] = jnp.zeros_like(sx_ref)

    def residual(r, cols):
        xn = x_ref[r, :] + res_scale * (y_ref[r, cols] * ry_ref[r, :] * pg_ref[...])
        xo_ref[r, :] = xn
        if emit_h:
            ho_ref[r, :] = (xn * ng_ref[...]).astype(BF16)
            sx_ref[r, :] += jnp.sum(xn * xn, axis=-1, keepdims=True)

    first_k = s < nj
    last_k = s >= n_mm - nj

    def matmul_step(first, last):
        cols = tile_cols(s % nj)
        y = jnp.dot(a_ref[...], w_ref[...], preferred_element_type=F32)
        if not first:
            y = y_ref[:, cols] + y
        y_ref[:, cols] = y
        if last:
            sq = jnp.sum(y * y, axis=-1, keepdims=True)
            sy_ref[...] = jnp.where(s == n_mm - nj, sq, sy_ref[...] + sq)

    @pl.when((b > 0) & (b < ni) & first_k)
    def _():
        for c in range(tm // rows):
            residual(pl.ds(c * rows, rows), tile_cols(s))
        matmul_step(True, nkk == 1)

    @pl.when((b == 0) & first_k)
    def _():
        matmul_step(True, nkk == 1)

    @pl.when((b == ni) & first_k)
    def _():
        for_row_chunks(lambda r: residual(r, tile_cols(s)))

    if emit_h:
        @pl.when((b > 0) & (s == nj - 1))
        def _():
            ro_ref[...] = lax.rsqrt(sx_ref[...] * (1.0 / d) + RMS_EPS)

    if nkk > 2:
        @pl.when((b < ni) & jnp.logical_not(first_k) & jnp.logical_not(last_k))
        def _():
            matmul_step(False, False)

    if nkk > 1:
        @pl.when((b < ni) & last_k)
        def _():
            matmul_step(False, True)


def mm_norm_res(a, w_tiled, x, post_g, next_g=None, *, res_scale, tm=1024, rows=256):
    t, kdim = a.shape
    nkk, nj, tkk, tn = w_tiled.shape
    d = nj * tn
    assert kdim == nkk * tkk
    ni = t // tm
    n_mm = nkk * nj
    emit_h = next_g is not None
    mm_row = lambda b: jnp.minimum(b, ni - 1)
    mm_step = lambda b, s: jnp.where(b < ni, s, n_mm - 1)
    ep_row = lambda b: jnp.maximum(b - 1, 0)
    ep_tile = lambda b, s: jnp.where(b > 0, jnp.minimum(s, nj - 1), 0)
    tile_spec = pl.BlockSpec((tm, tn), lambda b, s: (ep_row(b), ep_tile(b, s)))
    gain_spec = pl.BlockSpec((1, tn), lambda b, s: (0, ep_tile(b, s)))
    operands = [a, w_tiled, x, post_g.reshape(1, d)]
    in_specs = [pl.BlockSpec((tm, tkk), lambda b, s: (mm_row(b), mm_step(b, s) // nj)),
                pl.BlockSpec((None, None, tkk, tn), lambda b, s: (mm_step(b, s) // nj, mm_step(b, s) % nj, 0, 0)),
                tile_spec, gain_spec]
    out_shape = [jax.ShapeDtypeStruct((t, d), F32)]
    out_specs = [tile_spec]
    row_stat = pltpu.VMEM((tm, 1), F32)
    scratch = [pltpu.VMEM((tm, d), F32), row_stat, row_stat]
    if emit_h:
        operands.append(next_g.reshape(1, d))
        in_specs.append(gain_spec)
        out_shape += [jax.ShapeDtypeStruct((t, d), BF16), jax.ShapeDtypeStruct((t, 1), F32)]
        out_specs += [tile_spec, pl.BlockSpec((tm, 1), lambda b, s: (ep_row(b), 0))]
        scratch.append(row_stat)
    res = pl.pallas_call(
        functools.partial(_mm_norm_res_kernel, res_scale=res_scale, nkk=nkk, nj=nj, ni=ni, rows=rows,
                          emit_h=emit_h),
        grid=(ni + 1, n_mm),
        in_specs=in_specs,
        out_specs=out_specs,
        out_shape=out_shape,
        scratch_shapes=scratch,
        compiler_params=_params("arbitrary", "arbitrary"),
        name="mm_norm_res",
    )(*operands)
    return (res[0], res[1], res[2]) if emit_h else (res[0], None, None)


def _matmul_kernel(a_ref, r_ref, w_ref, o_ref):
    acc = jnp.dot(a_ref[...], w_ref[...], preferred_element_type=F32)
    o_ref[...] = (r_ref[...] * acc).astype(o_ref.dtype)


def matmul(a, r, w, *, tm=1024, tn=1280):
    t, kdim = a.shape
    n = w.shape[1]
    return pl.pallas_call(
        _matmul_kernel,
        grid=(t // tm, n // tn),
        in_specs=[pl.BlockSpec((tm, kdim), lambda i, j: (i, 0)),
                  pl.BlockSpec((tm, 1), lambda i, j: (i, 0)),
                  pl.BlockSpec((kdim, tn), lambda i, j: (0, j))],
        out_specs=pl.BlockSpec((tm, tn), lambda i, j: (i, j)),
        out_shape=jax.ShapeDtypeStruct((t, n), BF16),
        compiler_params=_params("parallel", "arbitrary"),
        name="in_proj",
    )(a, r, w)


def _gelu_exact(x):
    return 0.5 * x * (1.0 + lax.erf(x * (1.0 / math.sqrt(2.0))))


def _gmlp_kernel(z_ref, lng_ref, lnb_ref, ws_ref, bs_ref, o_ref, *, chunks):
    bw = BRANCH_WIDTH
    row = lax.broadcasted_iota(jnp.int32, (GMLP_CHUNK, GMLP_CHUNK), 0)
    col = lax.broadcasted_iota(jnp.int32, (GMLP_CHUNK, GMLP_CHUNK), 1)
    causal = row >= col
    for c in range(chunks):
        r = slice(c * GMLP_CHUNK, (c + 1) * GMLP_CHUNK)
        u = _gelu_exact(z_ref[r, :bw].astype(F32))
        v = _gelu_exact(z_ref[r, bw:].astype(F32))
        mu = jnp.mean(v, axis=-1, keepdims=True)
        vc = v - mu
        var = jnp.mean(vc * vc, axis=-1, keepdims=True)
        vn = (vc * lax.rsqrt(var + LN_EPS) * lng_ref[...] + lnb_ref[...]).astype(BF16)
        for g in range(GMLP_GROUPS):
            gs = slice(g * GMLP_GROUP_DIM, (g + 1) * GMLP_GROUP_DIM)
            w = jnp.where(causal, ws_ref[g], 0.0).astype(BF16)
            mixed = jnp.dot(w, vn[:, gs], preferred_element_type=F32) + bs_ref[:, g:g + 1]
            o_ref[r, gs] = (u[:, gs] * mixed).astype(o_ref.dtype)


def gmlp(proj, ln_g, ln_b, w_s, b_s, *, chunks=4):
    t = proj.shape[0]
    bw = BRANCH_WIDTH
    tm = chunks * GMLP_CHUNK
    return pl.pallas_call(
        functools.partial(_gmlp_kernel, chunks=chunks),
        grid=(t // tm,),
        in_specs=[pl.BlockSpec((tm, 2 * bw), lambda i: (i, 0)),
                  pl.BlockSpec((1, bw), lambda i: (0, 0)),
                  pl.BlockSpec((1, bw), lambda i: (0, 0)),
                  pl.BlockSpec((GMLP_GROUPS, GMLP_CHUNK, GMLP_CHUNK), lambda i: (0, 0, 0)),
                  pl.BlockSpec((GMLP_CHUNK, GMLP_GROUPS), lambda i: (0, 0))],
        out_specs=pl.BlockSpec((tm, bw), lambda i: (i, 0)),
        out_shape=jax.ShapeDtypeStruct((t, bw), BF16),
        compiler_params=_params("parallel"),
        name="gmlp",
    )(proj, ln_g.reshape(1, bw), ln_b.reshape(1, bw), w_s, b_s.T)


MOBA_MAX_BLOCKS = 32
MOBA_CHUNK_BLOCKS = 4
MOBA_Q_TILE_BLOCKS = 2
MOBA_PREP_ROWS = 512
LOG2_E = math.log2(math.e)
FEAT_SEL, FEAT_ONE, FEAT_BLK, FEAT_OFF, FEAT_END = 0, 32, 35, 38, 41


def _moba_key_features(seq):
    pos = jnp.arange(seq, dtype=jnp.int32)[:, None]
    kb, off = pos // MOBA_BLOCK, pos % MOBA_BLOCK
    lane = jnp.arange(LANES, dtype=jnp.int32)[None, :]
    feat = jnp.where(lane < FEAT_ONE, (lane == kb).astype(F32),
                     jnp.where(lane < FEAT_BLK, 1.0,
                               jnp.where(lane < FEAT_OFF, (kb * MOBA_BLOCK).astype(F32),
                                         jnp.where(lane < FEAT_END, off.astype(F32), 0.0))))
    return feat.astype(BF16)


def _split3(x):
    hi = x.astype(BF16).astype(F32)
    mid = (x - hi).astype(BF16).astype(F32)
    return hi, mid, x - hi - mid


def _moba_prepare(slope, q_ref, k_ref, kfeat_ref, kaug_ref, qaug_ref, kmh_ref, kml_ref, *, nb):
    blk = MOBA_BLOCK
    dh = MOBA_HEAD_DIM
    seq = q_ref.shape[0]
    prep = MOBA_PREP_ROWS
    nbp = MOBA_MAX_BLOCKS
    scale = dh ** -0.5 * LOG2_E
    slope = slope * LOG2_E

    kmh_ref[...] = jnp.zeros_like(kmh_ref)
    kml_ref[...] = jnp.zeros_like(kml_ref)
    for n in range(nb):
        rows = slice(n * blk, (n + 1) * blk)
        kn = k_ref[rows, :]
        kaug_ref[rows, :dh] = kn
        kaug_ref[rows, dh:] = kfeat_ref[rows, :]
        km = jnp.mean(kn.astype(F32), axis=0, keepdims=True)
        hi = km.astype(BF16)
        kmh_ref[n:n + 1, :] = hi
        kml_ref[n:n + 1, :] = (km - hi.astype(F32)).astype(BF16)

    blk_id = lax.broadcasted_iota(jnp.int32, (nbp, prep), 0)
    q_off = lax.broadcasted_iota(jnp.int32, (nbp, prep), 1)
    lane = lax.broadcasted_iota(jnp.int32, (prep, LANES), 1)
    row = lax.broadcasted_iota(jnp.int32, (prep, LANES), 0)
    part = jnp.where(lane < FEAT_BLK, lane - FEAT_ONE, jnp.where(lane < FEAT_OFF, lane - FEAT_BLK, lane - FEAT_OFF))
    pick3 = lambda parts: jnp.where(part == 0, parts[0], jnp.where(part == 1, parts[1], parts[2]))
    slope_feat = pick3(_split3(jnp.full((prep, LANES), slope, F32)))

    def rows_chunk(c, carry):
        row0 = pl.multiple_of(c * prep, prep)
        rows = pl.ds(row0, prep)
        q = q_ref[rows, :]
        gate = _dot_nt(kmh_ref[:nbp, :], q) + _dot_nt(kml_ref[:nbp, :], q)
        q_blk = (row0 + q_off) // blk
        gate = jnp.where(blk_id < q_blk, gate, -jnp.inf)
        sel_bias = jnp.where(blk_id == q_blk, 0.0, MASKED)
        for _ in range(MOBA_TOPK):
            best = jnp.max(gate, axis=0, keepdims=True)
            first = jnp.min(jnp.where(gate == best, blk_id, nbp), axis=0, keepdims=True)
            pick = (blk_id == first) & (best > -jnp.inf)
            sel_bias = jnp.where(pick, 0.0, sel_bias)
            gate = jnp.where(pick, -jnp.inf, gate)
        sel_rows = jnp.concatenate([sel_bias, jnp.zeros((LANES - nbp, prep), F32)], axis=0).T

        t_pos = (row0 + row).astype(F32)
        feat = jnp.where(lane < FEAT_ONE, sel_rows,
                         jnp.where(lane < FEAT_BLK, pick3(_split3(-slope * t_pos)),
                                   jnp.where(lane < FEAT_END, slope_feat, 0.0)))
        qaug_ref[rows, :dh] = (q.astype(F32) * scale).astype(BF16)
        qaug_ref[rows, dh:] = feat.astype(BF16)
        return carry

    lax.fori_loop(0, seq // prep, rows_chunk, 0)


def _moba_kernel(slopes_ref, q_ref, k_ref, v_ref, kfeat_ref, o_ref, kaug_ref, qaug_ref, kmh_ref, kml_ref, s_ref,
                 *, nb):
    blk = MOBA_BLOCK
    dh = MOBA_HEAD_DIM
    qt = MOBA_Q_TILE_BLOCKS * blk
    cb = MOBA_CHUNK_BLOCKS
    span = cb * blk
    h = pl.program_id(1)
    ti = pl.program_id(2)
    first_blk = ti * MOBA_Q_TILE_BLOCKS

    @pl.when(ti == 0)
    def _():
        _moba_prepare(slopes_ref[h], q_ref, k_ref, kfeat_ref, kaug_ref, qaug_ref, kmh_ref, kml_ref, nb=nb)

    q_aug = qaug_ref[pl.ds(pl.multiple_of(ti * qt, qt), qt), :]

    def chunk_rows(p):
        return pl.ds(pl.multiple_of(p * span, span), span)

    def scores(p):
        return _dot_nt(q_aug, kaug_ref[chunk_rows(p), :])

    def lanes_max(m, s):
        for c in range(0, s.shape[1], LANES):
            m = jnp.maximum(m, s[:, c:c + LANES])
        return m

    def pass1(p, mvec):
        s = scores(p)
        s_ref[p] = s
        return lanes_max(mvec, s)

    def chunk_loop(count, body, init):
        carry = lax.fori_loop(0, count // 2, lambda i, c: body(2 * i + 1, body(2 * i, c)), init)
        return lax.fori_loop(count - count % 2, count, body, carry)

    n_past = first_blk // cb
    mvec = chunk_loop(n_past, pass1, jnp.full((qt, LANES), -jnp.inf, F32))
    r_idx = lax.broadcasted_iota(jnp.int32, (blk, blk), 0)
    c_idx = lax.broadcasted_iota(jnp.int32, (blk, blk), 1)
    causal_bias = jnp.where(r_idx >= c_idx, 0.0, MASKED)
    own = first_blk % cb
    bias = jnp.concatenate(
        [jnp.concatenate([causal_bias * (own + r == c).astype(F32) for c in range(cb)], axis=1)
         for r in range(MOBA_Q_TILE_BLOCKS)], axis=0)
    s = scores(n_past) + bias
    s_ref[n_past] = s
    m = jnp.max(lanes_max(mvec, s), axis=-1, keepdims=True)

    def pass2(p, carry):
        l, acc = carry
        e = jnp.exp2(s_ref[p] - m)
        l = l + jnp.sum(e, axis=-1, keepdims=True)
        acc = acc + jnp.dot(e.astype(BF16), v_ref[chunk_rows(p), :], preferred_element_type=F32)
        return l, acc

    l, acc = chunk_loop(n_past + 1, pass2, (jnp.zeros((qt, 1), F32), jnp.zeros((qt, dh), F32)))
    o_ref[...] = (acc / l).astype(o_ref.dtype)


def moba(proj, slopes, *, batch, seq):
    blk = MOBA_BLOCK
    nb = seq // blk
    qt = MOBA_Q_TILE_BLOCKS * blk
    nt = seq // qt
    assert nb % MOBA_CHUNK_BLOCKS == 0 and nb <= MOBA_MAX_BLOCKS and seq % MOBA_PREP_ROWS == 0
    assert MOBA_CHUNK_BLOCKS % MOBA_Q_TILE_BLOCKS == 0
    dh = MOBA_HEAD_DIM
    cq, ck, cv = COL_MOBA_Q // dh, COL_MOBA_K // dh, COL_MOBA_V // dh
    return pl.pallas_call(
        functools.partial(_moba_kernel, nb=nb),
        grid_spec=pltpu.PrefetchScalarGridSpec(
            num_scalar_prefetch=1,
            grid=(batch, MOBA_HEADS, nt),
            in_specs=[pl.BlockSpec((seq, dh), lambda b, h, i, s: (b, cq + h)),
                      pl.BlockSpec((seq, dh), lambda b, h, i, s: (b, ck + h)),
                      pl.BlockSpec((seq, dh), lambda b, h, i, s: (b, cv + h)),
                      pl.BlockSpec((seq, LANES), lambda b, h, i, s: (0, 0))],
            out_specs=pl.BlockSpec((qt, dh), lambda b, h, i, s: (b * nt + i, h)),
            scratch_shapes=[pltpu.VMEM((seq, dh + LANES), BF16), pltpu.VMEM((seq, dh + LANES), BF16),
                            pltpu.VMEM((LANES, dh), BF16), pltpu.VMEM((LANES, dh), BF16),
                            pltpu.VMEM((nb // MOBA_CHUNK_BLOCKS, qt, MOBA_CHUNK_BLOCKS * blk), F32)],
        ),
        out_shape=jax.ShapeDtypeStruct((batch * seq, BRANCH_WIDTH), BF16),
        compiler_params=_params("parallel", "parallel", "arbitrary"),
        name="moba",
    )(slopes, proj, proj, proj, _moba_key_features(seq))


SWA_STEP_BLOCKS = 2


def _swa_kernel(slopes_ref, sinks_ref, q_ref, *refs):
    nsub = SWA_STEP_BLOCKS
    k_refs, v_refs, o_ref = refs[:nsub + 1], refs[nsub + 1:2 * nsub + 2], refs[2 * nsub + 2]
    w = SWA_WINDOW
    dh = SWA_HEAD_DIM
    n = pl.program_id(1)
    scale = dh ** -0.5
    low2 = lax.broadcasted_iota(jnp.int32, (2 * w, LANES), 1) < dh
    t_idx = lax.broadcasted_iota(jnp.int32, (w, 4 * w), 0)
    col = lax.broadcasted_iota(jnp.int32, (w, 4 * w), 1)
    s_idx = col % w
    is_prev = (col % (2 * w)) < w
    second = col >= 2 * w
    dist = (t_idx - s_idx).astype(F32) + jnp.where(is_prev, float(w), 0.0)
    in_window = (is_prev & (s_idx > t_idx)) | (jnp.logical_not(is_prev) & (s_idx <= t_idx))
    in_window_first = in_window & (jnp.logical_not(is_prev) | (n > 0))

    def stacked(prev_ref, cur_ref):
        x = jnp.concatenate([prev_ref[...], cur_ref[...]], axis=0).astype(F32)
        xr = pltpu.roll(x, dh, axis=1)
        z = jnp.zeros_like(x)
        head0 = jnp.concatenate([jnp.where(low2, x, z), jnp.where(low2, z, xr)], axis=0)
        head1 = jnp.concatenate([jnp.where(low2, xr, z), jnp.where(low2, z, x)], axis=0)
        return head0.astype(BF16), head1.astype(BF16)

    pairs = SWA_Q_HEADS // 2
    group = SWA_Q_HEADS // SWA_KV_HEADS
    for sub in range(nsub):
        rows = slice(sub * w, (sub + 1) * w)
        ok = in_window_first if sub == 0 else in_window
        k_st = stacked(k_refs[sub], k_refs[sub + 1])
        v_st = stacked(v_refs[sub], v_refs[sub + 1])
        for pr in range(pairs):
            q = q_ref[rows, pr * LANES:(pr + 1) * LANES]
            kh = (2 * pr) // group
            slope = jnp.where(second, slopes_ref[2 * pr + 1], slopes_ref[2 * pr])
            s = _dot_nt(q, k_st[kh]) * scale
            s = jnp.where(ok, s - slope * dist, MASKED)
            probs = []
            for half in range(2):
                sink = sinks_ref[2 * pr + half]
                sh = s[:, half * 2 * w:(half + 1) * 2 * w]
                m = jnp.maximum(jnp.max(sh, axis=-1, keepdims=True), sink)
                e = jnp.exp(sh - m)
                denom = jnp.sum(e, axis=-1, keepdims=True) + jnp.exp(sink - m)
                probs.append((e * (1.0 / denom)).astype(BF16))
            out = jnp.dot(jnp.concatenate(probs, axis=1), v_st[kh], preferred_element_type=F32)
            o_ref[rows, pr * LANES:(pr + 1) * LANES] = out.astype(o_ref.dtype)


def swa(proj, slopes, sinks, *, batch, seq):
    w = SWA_WINDOW
    nsub = SWA_STEP_BLOCKS
    nblk = seq // w
    nstep = nblk // nsub
    assert nblk % nsub == 0
    bw = BRANCH_WIDTH
    cq = COL_SWA_Q // bw
    ck = COL_SWA_K // SWA_KV_WIDTH
    cv = COL_SWA_V // SWA_KV_WIDTH

    def kv_spec(col_blk, j):
        return pl.BlockSpec((w, SWA_KV_WIDTH),
                            lambda b, n, *_: (b * nblk + jnp.maximum(n * nsub + j - 1, 0), col_blk))

    return pl.pallas_call(
        _swa_kernel,
        grid_spec=pltpu.PrefetchScalarGridSpec(
            num_scalar_prefetch=2,
            grid=(batch, nstep),
            in_specs=([pl.BlockSpec((nsub * w, bw), lambda b, n, *_: (b * nstep + n, cq))]
                      + [kv_spec(ck, j) for j in range(nsub + 1)]
                      + [kv_spec(cv, j) for j in range(nsub + 1)]),
            out_specs=pl.BlockSpec((nsub * w, bw), lambda b, n, *_: (b * nstep + n, 0)),
        ),
        out_shape=jax.ShapeDtypeStruct((batch * seq, bw), BF16),
        compiler_params=_params("parallel", "arbitrary"),
        name="swa",
    )(slopes, sinks, *([proj] * (2 * nsub + 3)))


def _merge_kernel(*refs, tiles):
    n = len(tiles)
    h_ref, r_ref, wg_ref0, wg_ref1, wg_ref2, bg_ref, ya_ref, yb_ref, yc_ref, wb_ref = refs[:10]
    o_ref = refs[10 + n]
    h = h_ref[...]
    r = r_ref[...]
    merged = None
    for br, (wg_ref, y_ref) in enumerate(((wg_ref0, ya_ref), (wg_ref1, yb_ref), (wg_ref2, yc_ref))):
        logits = r * jnp.dot(h, wg_ref[...], preferred_element_type=F32) + bg_ref[br]
        branch = jnp.dot(y_ref[...], wb_ref[br], preferred_element_type=F32)
        term = jax.nn.sigmoid(logits) * branch
        merged = term if merged is None else merged + term
    o_ref[...] = merged.astype(o_ref.dtype)
    _side_cast_run(tiles, refs[10:10 + n], refs[11 + n:])


def merge(h, r, w_gate, b_gate, y_a, y_b, y_c, w_branch, *, casts=(), tm=1024, tn=256):
    t, d = h.shape
    bw = BRANCH_WIDTH
    ni, nj = t // tm, d // tn
    c_in, c_out, c_shapes = _side_cast_plan(casts, ni * nj, lambda i, j: i * nj + j)
    gate_spec = lambda n: pl.BlockSpec((d, tn), lambda i, j: (0, n * nj + j))
    y_spec = pl.BlockSpec((tm, bw), lambda i, j: (i, 0))
    return pl.pallas_call(
        functools.partial(_merge_kernel, tiles=tuple(c.tile for c in casts)),
        grid=(ni, nj),
        in_specs=[pl.BlockSpec((tm, d), lambda i, j: (i, 0)),
                  pl.BlockSpec((tm, 1), lambda i, j: (i, 0)),
                  gate_spec(0), gate_spec(1), gate_spec(2),
                  pl.BlockSpec((N_BRANCHES, 1, tn), lambda i, j: (0, 0, j)),
                  y_spec, y_spec, y_spec,
                  pl.BlockSpec((N_BRANCHES, bw, tn), lambda i, j: (0, 0, j))] + c_in,
        out_specs=[pl.BlockSpec((tm, tn), lambda i, j: (i, j))] + c_out,
        out_shape=[jax.ShapeDtypeStruct((t, d), BF16)] + c_shapes,
        compiler_params=_params("arbitrary", "arbitrary"),
        name="merge",
    )(h, r, w_gate, w_gate, w_gate, b_gate.reshape(N_BRANCHES, 1, d), y_a, y_b, y_c, w_branch,
      *[c.src for c in casts])


def _alibi_slopes():
    i = jnp.arange(1, N_ALIBI_HEADS + 1, dtype=F32)
    s = jnp.exp2(-8.0 * i / N_ALIBI_HEADS)
    return s[:SWA_Q_HEADS], s[SWA_Q_HEADS:]


def kernel(x, ffn1_pre_g, ffn1_w_up, ffn1_w_down, ffn1_post_g, mix_pre_g, w_in, gmlp_ln_g, gmlp_ln_b,
           gmlp_w_s, gmlp_b_s, swa_sinks, w_gate, b_gate, w_branch, w_out, mix_post_g, ffn2_pre_g,
           ffn2_w_up, ffn2_w_down, ffn2_post_g):
    batch, seq, d = x.shape
    depth = ffn1_pre_g.shape[0]
    swa_slopes, moba_slopes = _alibi_slopes()
    xf = x.reshape(batch * seq, d)
    h, r = prenorm(xf, ffn1_pre_g[0])
    tiled = lambda w, i: _SideCast(w, i, MM_TILE)
    plain = lambda w, i: _SideCast(w, i, None)
    w_branch_rows = w_branch.reshape(depth, -1, d)
    w_up1 = ffn1_w_up[0].astype(BF16)
    for i in range(depth):
        act, w_down1, w_in_b, w_gate_b, w_branch_b, w_out_b = ffn_up(
            h, r, w_up1, casts=(tiled(ffn1_w_down, i), plain(w_in, i), plain(w_gate, i),
                                plain(w_branch_rows, i), tiled(w_out, i)))
        xf, h, r = mm_norm_res(act, w_down1, xf, ffn1_post_g[i], mix_pre_g[i], res_scale=0.5)

        proj = matmul(h, r, w_in_b)
        y_a = gmlp(proj, gmlp_ln_g[i], gmlp_ln_b[i], gmlp_w_s[i], gmlp_b_s[i])
        y_b = moba(proj, moba_slopes, batch=batch, seq=seq)
        y_c = swa(proj, swa_slopes, swa_sinks[i], batch=batch, seq=seq)
        merged, w_up2 = merge(h, r, w_gate_b, b_gate[i], y_a, y_b, y_c, w_branch_b.reshape(w_branch.shape[1:]),
                              casts=(plain(ffn2_w_up, i),))
        xf, h, r = mm_norm_res(merged, w_out_b, xf, mix_post_g[i], ffn2_pre_g[i], res_scale=1.0)

        last = i + 1 == depth
        casts = (tiled(ffn2_w_down, i),) + (() if last else (plain(ffn1_w_up, i + 1),))
        act, w_down2, *nxt = ffn_up(h, r, w_up2, casts=casts)
        xf, h, r = mm_norm_res(act, w_down2, xf, ffn2_post_g[i], None if last else ffn1_pre_g[i + 1],
                               res_scale=0.5)
        if not last:
            w_up1 = nxt[0]
    return xf.reshape(batch, seq, d)
```

```python
import functools
import math
from typing import NamedTuple, Optional, Tuple

import jax
import jax.numpy as jnp
from jax import lax
from jax.experimental import pallas as pl
from jax.experimental.pallas import tpu as pltpu

D_MODEL = 4096
DEPTH = 2
BRANCH_WIDTH = D_MODEL // 4
N_BRANCHES = 3
GMLP_CHUNK = 128
GMLP_GROUP_DIM = 128
GMLP_GROUPS = BRANCH_WIDTH // GMLP_GROUP_DIM
MOBA_HEAD_DIM = 128
MOBA_HEADS = BRANCH_WIDTH // MOBA_HEAD_DIM
MOBA_BLOCK = 256
MOBA_TOPK = 3
SWA_HEAD_DIM = 64
SWA_Q_HEADS = BRANCH_WIDTH // SWA_HEAD_DIM
SWA_KV_HEADS = SWA_Q_HEADS // 8
SWA_WINDOW = 128
SWA_KV_WIDTH = SWA_KV_HEADS * SWA_HEAD_DIM
IN_COLS = 6 * BRANCH_WIDTH + 2 * SWA_KV_WIDTH
D_FF = 2 * D_MODEL
N_ALIBI_HEADS = SWA_Q_HEADS + MOBA_HEADS
RMS_EPS = 1e-6
LN_EPS = 1e-5

LANES = 128
BF16_SUBLANES = 16
VMEM_LIMIT = 60 * 1024 * 1024
MM_TILE = (4096, 512)
A_LEAD_STEPS = 2
MASKED = -1e30

F32 = jnp.float32
BF16 = jnp.bfloat16

COL_GMLP = 0
COL_MOBA_Q = 2 * BRANCH_WIDTH
COL_MOBA_K = 3 * BRANCH_WIDTH
COL_MOBA_V = 4 * BRANCH_WIDTH
COL_SWA_Q = 5 * BRANCH_WIDTH
COL_SWA_K = 6 * BRANCH_WIDTH
COL_SWA_V = 6 * BRANCH_WIDTH + SWA_KV_WIDTH


def _params(*semantics):
    return pltpu.CompilerParams(dimension_semantics=semantics, vmem_limit_bytes=VMEM_LIMIT)


def _rms_scale(x):
    return lax.rsqrt(jnp.mean(x * x, axis=-1, keepdims=True) + RMS_EPS)


def _dot_nt(a, b):
    return lax.dot_general(a, b, (((1,), (1,)), ((), ())), preferred_element_type=F32)


def _prenorm_kernel(x_ref, g_ref, h_ref, r_ref):
    x = x_ref[...]
    h_ref[...] = (x * g_ref[...]).astype(h_ref.dtype)
    r_ref[...] = _rms_scale(x)


def prenorm(x, g, *, tm=256):
    t, d = x.shape
    return pl.pallas_call(
        _prenorm_kernel,
        grid=(t // tm,),
        in_specs=[pl.BlockSpec((tm, d), lambda i: (i, 0)),
                  pl.BlockSpec((1, d), lambda i: (0, 0))],
        out_specs=[pl.BlockSpec((tm, d), lambda i: (i, 0)),
                   pl.BlockSpec((tm, 1), lambda i: (i, 0))],
        out_shape=[jax.ShapeDtypeStruct((t, d), BF16), jax.ShapeDtypeStruct((t, 1), F32)],
        compiler_params=_params("parallel"),
        name="prenorm",
    )(x, g.reshape(1, d))


class _SideCast(NamedTuple):
    src: jax.Array
    layer: int
    tile: Optional[Tuple[int, int]]


def _side_cast_plan(casts, steps, step_of):
    in_specs, out_specs, out_shapes = [], [], []
    for c in casts:
        _, nrow, ncol = c.src.shape
        nblk = steps
        while nrow % nblk or (nrow // nblk) % BF16_SUBLANES:
            assert nblk % 2 == 0, (nrow, steps)
            nblk //= 2
        rows = nrow // nblk
        blk_of = lambda *ids, hold=steps // nblk: step_of(*ids) // hold
        in_specs.append(pl.BlockSpec((None, rows, ncol), lambda *ids, f=blk_of, layer=c.layer: (layer, f(*ids), 0)))
        if c.tile is None:
            out_specs.append(pl.BlockSpec((rows, ncol), lambda *ids, f=blk_of: (f(*ids), 0)))
            out_shapes.append(jax.ShapeDtypeStruct((nrow, ncol), BF16))
        else:
            tk, tn = c.tile
            per = tk // rows
            assert tk % rows == 0 and nrow % tk == 0 and ncol % tn == 0
            out_specs.append(pl.BlockSpec((None, ncol // tn, rows, tn),
                                          lambda *ids, f=blk_of, per=per: (f(*ids) // per, 0, f(*ids) % per, 0)))
            out_shapes.append(jax.ShapeDtypeStruct((nrow // tk, ncol // tn, tk, tn), BF16))
    return in_specs, out_specs, out_shapes


def _side_cast_run(tiles, src_refs, dst_refs):
    for tile, src, dst in zip(tiles, src_refs, dst_refs, strict=True):
        if tile is None:
            dst[...] = src[...].astype(BF16)
        else:
            tn = tile[1]
            for j in range(dst.shape[0]):
                dst[j] = src[:, j * tn:(j + 1) * tn].astype(BF16)


def _ffn_up_kernel(*refs, tiles):
    n = len(tiles)
    h_ref, r_ref, wg_ref, wu_ref = refs[:4]
    o_ref = refs[4 + n]
    h = h_ref[...]
    r = r_ref[...]
    g = r * jnp.dot(h, wg_ref[...], preferred_element_type=F32)
    u = r * jnp.dot(h, wu_ref[...], preferred_element_type=F32)
    o_ref[...] = (g * jax.nn.sigmoid(g) * u).astype(o_ref.dtype)
    _side_cast_run(tiles, refs[4:4 + n], refs[5 + n:])


def ffn_up(h, r, w_up, *, casts=(), tm=1024, tn=512):
    t, d = h.shape
    f = w_up.shape[1] // 2
    ni, nj = t // tm, f // tn
    c_in, c_out, c_shapes = _side_cast_plan(casts, ni * nj, lambda i, j: i * nj + j)
    return pl.pallas_call(
        functools.partial(_ffn_up_kernel, tiles=tuple(c.tile for c in casts)),
        grid=(ni, nj),
        in_specs=[pl.BlockSpec((tm, d), lambda i, j: (i, 0)),
                  pl.BlockSpec((tm, 1), lambda i, j: (i, 0)),
                  pl.BlockSpec((d, tn), lambda i, j: (0, j)),
                  pl.BlockSpec((d, tn), lambda i, j: (0, j + nj))] + c_in,
        out_specs=[pl.BlockSpec((tm, tn), lambda i, j: (i, j))] + c_out,
        out_shape=[jax.ShapeDtypeStruct((t, f), BF16)] + c_shapes,
        compiler_params=_params("arbitrary", "arbitrary"),
        name="ffn_up",
    )(h, r, w_up, w_up, *[c.src for c in casts])


def _mm_norm_res_kernel(*refs, res_scale, nkk, nj, ni, rows, emit_h):
    if emit_h:
        a_hbm, w_ref, x_ref, pg_ref, ng_ref, xo_ref, ho_ref, ro_ref, y_ref, a_buf, sy_ref, ry_ref, sx_ref, a_sem = refs
    else:
        a_hbm, w_ref, x_ref, pg_ref, xo_ref, y_ref, a_buf, sy_ref, ry_ref, a_sem = refs
    b = pl.program_id(0)
    s = pl.program_id(1)
    n_mm = nkk * nj
    tm, d = y_ref.shape
    tk, tn = w_ref.shape
    kk = s // nj
    q = b * nkk + kk
    a_ref = a_buf.at[q % 2]

    def a_copy(blk, part, slot):
        src = a_hbm.at[pl.ds(pl.multiple_of(blk * tm, tm), tm), pl.ds(pl.multiple_of(part * tk, tk), tk)]
        return pltpu.make_async_copy(src, a_buf.at[slot], a_sem.at[slot])

    @pl.when((b == 0) & (s == 0))
    def _():
        a_copy(0, 0, 0).start()

    @pl.when((b < ni) & (s % nj == 0))
    def _():
        a_copy(b, kk, q % 2).wait()

    last_tile = (b == ni - 1) & (kk == nkk - 1)

    @pl.when((b < ni) & (s % nj == A_LEAD_STEPS) & jnp.logical_not(last_tile))
    def _():
        wrap = kk == nkk - 1
        a_copy(jnp.where(wrap, b + 1, b), jnp.where(wrap, 0, kk + 1), (q + 1) % 2).start()

    def tile_cols(e):
        return pl.ds(pl.multiple_of(e * tn, tn), tn)

    def for_row_chunks(body):
        def step(c, carry):
            body(pl.ds(pl.multiple_of(c * rows, rows), rows))
            return carry
        lax.fori_loop(0, tm // rows, step, 0)

    @pl.when((b > 0) & (s == 0))
    def _():
        ry_ref[...] = lax.rsqrt(sy_ref[...] * (1.0 / d) + RMS_EPS)
        if emit_h:
            sx_ref[...] = jnp.zeros_like(sx_ref)

    def residual(r, cols):
        xn = x_ref[r, :] + res_scale * (y_ref[r, cols] * ry_ref[r, :] * pg_ref[...])
        xo_ref[r, :] = xn
        if emit_h:
            ho_ref[r, :] = (xn * ng_ref[...]).astype(BF16)
            sx_ref[r, :] += jnp.sum(xn * xn, axis=-1, keepdims=True)

    first_k = s < nj
    last_k = s >= n_mm - nj

    def matmul_step(first, last):
        cols = tile_cols(s % nj)
        y = jnp.dot(a_ref[...], w_ref[...], preferred_element_type=F32)
        if not first:
            y = y_ref[:, cols] + y
        y_ref[:, cols] = y
        if last:
            sq = jnp.sum(y * y, axis=-1, keepdims=True)
            sy_ref[...] = jnp.where(s == n_mm - nj, sq, sy_ref[...] + sq)

    @pl.when((b > 0) & (b < ni) & first_k)
    def _():
        for c in range(tm // rows):
            residual(pl.ds(c * rows, rows), tile_cols(s))
        matmul_step(True, nkk == 1)

    @pl.when((b == 0) & first_k)
    def _():
        matmul_step(True, nkk == 1)

    @pl.when((b == ni) & first_k)
    def _():
        for_row_chunks(lambda r: residual(r, tile_cols(s)))

    if emit_h:
        @pl.when((b > 0) & (s == nj - 1))
        def _():
            ro_ref[...] = lax.rsqrt(sx_ref[...] * (1.0 / d) + RMS_EPS)

    if nkk > 2:
        @pl.when((b < ni) & jnp.logical_not(first_k) & jnp.logical_not(last_k))
        def _():
            matmul_step(False, False)

    if nkk > 1:
        @pl.when((b < ni) & last_k)
        def _():
            matmul_step(False, True)


def mm_norm_res(a, w_tiled, x, post_g, next_g=None, *, res_scale, tm=1024, rows=256):
    t, kdim = a.shape
    nkk, nj, tkk, tn = w_tiled.shape
    d = nj * tn
    assert kdim == nkk * tkk
    ni = t // tm
    n_mm = nkk * nj
    emit_h = next_g is not None
    mm_step = lambda b, s: jnp.where(b < ni, s, n_mm - 1)
    ep_row = lambda b: jnp.maximum(b - 1, 0)
    ep_tile = lambda b, s: jnp.where(b > 0, jnp.minimum(s, nj - 1), 0)
    tile_spec = pl.BlockSpec((tm, tn), lambda b, s: (ep_row(b), ep_tile(b, s)))
    gain_spec = pl.BlockSpec((1, tn), lambda b, s: (0, ep_tile(b, s)))
    operands = [a, w_tiled, x, post_g.reshape(1, d)]
    in_specs = [pl.BlockSpec(memory_space=pl.ANY),
                pl.BlockSpec((None, None, tkk, tn), lambda b, s: (mm_step(b, s) // nj, mm_step(b, s) % nj, 0, 0)),
                tile_spec, gain_spec]
    out_shape = [jax.ShapeDtypeStruct((t, d), F32)]
    out_specs = [tile_spec]
    row_stat = pltpu.VMEM((tm, 1), F32)
    scratch = [pltpu.VMEM((tm, d), F32), pltpu.VMEM((2, tm, tkk), BF16), row_stat, row_stat]
    if emit_h:
        operands.append(next_g.reshape(1, d))
        in_specs.append(gain_spec)
        out_shape += [jax.ShapeDtypeStruct((t, d), BF16), jax.ShapeDtypeStruct((t, 1), F32)]
        out_specs += [tile_spec, pl.BlockSpec((tm, 1), lambda b, s: (ep_row(b), 0))]
        scratch.append(row_stat)
    scratch.append(pltpu.SemaphoreType.DMA((2,)))
    assert nj > A_LEAD_STEPS
    res = pl.pallas_call(
        functools.partial(_mm_norm_res_kernel, res_scale=res_scale, nkk=nkk, nj=nj, ni=ni, rows=rows,
                          emit_h=emit_h),
        grid=(ni + 1, n_mm),
        in_specs=in_specs,
        out_specs=out_specs,
        out_shape=out_shape,
        scratch_shapes=scratch,
        compiler_params=_params("arbitrary", "arbitrary"),
        name="mm_norm_res",
    )(*operands)
    return (res[0], res[1], res[2]) if emit_h else (res[0], None, None)


def _matmul_kernel(a_ref, r_ref, w_ref, o_ref):
    acc = jnp.dot(a_ref[...], w_ref[...], preferred_element_type=F32)
    o_ref[...] = (r_ref[...] * acc).astype(o_ref.dtype)


def matmul(a, r, w, *, tm=1024, tn=1280):
    t, kdim = a.shape
    n = w.shape[1]
    return pl.pallas_call(
        _matmul_kernel,
        grid=(t // tm, n // tn),
        in_specs=[pl.BlockSpec((tm, kdim), lambda i, j: (i, 0)),
                  pl.BlockSpec((tm, 1), lambda i, j: (i, 0)),
                  pl.BlockSpec((kdim, tn), lambda i, j: (0, j))],
        out_specs=pl.BlockSpec((tm, tn), lambda i, j: (i, j)),
        out_shape=jax.ShapeDtypeStruct((t, n), BF16),
        compiler_params=_params("parallel", "arbitrary"),
        name="in_proj",
    )(a, r, w)


def _gelu_exact(x):
    return 0.5 * x * (1.0 + lax.erf(x * (1.0 / math.sqrt(2.0))))


def _gmlp_kernel(z_ref, lng_ref, lnb_ref, ws_ref, bs_ref, o_ref, *, chunks):
    bw = BRANCH_WIDTH
    row = lax.broadcasted_iota(jnp.int32, (GMLP_CHUNK, GMLP_CHUNK), 0)
    col = lax.broadcasted_iota(jnp.int32, (GMLP_CHUNK, GMLP_CHUNK), 1)
    causal = row >= col
    for c in range(chunks):
        r = slice(c * GMLP_CHUNK, (c + 1) * GMLP_CHUNK)
        u = _gelu_exact(z_ref[r, :bw].astype(F32))
        v = _gelu_exact(z_ref[r, bw:].astype(F32))
        mu = jnp.mean(v, axis=-1, keepdims=True)
        vc = v - mu
        var = jnp.mean(vc * vc, axis=-1, keepdims=True)
        vn = (vc * lax.rsqrt(var + LN_EPS) * lng_ref[...] + lnb_ref[...]).astype(BF16)
        for g in range(GMLP_GROUPS):
            gs = slice(g * GMLP_GROUP_DIM, (g + 1) * GMLP_GROUP_DIM)
            w = jnp.where(causal, ws_ref[g], 0.0).astype(BF16)
            mixed = jnp.dot(w, vn[:, gs], preferred_element_type=F32) + bs_ref[:, g:g + 1]
            o_ref[r, gs] = (u[:, gs] * mixed).astype(o_ref.dtype)


def gmlp(proj, ln_g, ln_b, w_s, b_s, *, chunks=4):
    t = proj.shape[0]
    bw = BRANCH_WIDTH
    tm = chunks * GMLP_CHUNK
    return pl.pallas_call(
        functools.partial(_gmlp_kernel, chunks=chunks),
        grid=(t // tm,),
        in_specs=[pl.BlockSpec((tm, 2 * bw), lambda i: (i, 0)),
                  pl.BlockSpec((1, bw), lambda i: (0, 0)),
                  pl.BlockSpec((1, bw), lambda i: (0, 0)),
                  pl.BlockSpec((GMLP_GROUPS, GMLP_CHUNK, GMLP_CHUNK), lambda i: (0, 0, 0)),
                  pl.BlockSpec((GMLP_CHUNK, GMLP_GROUPS), lambda i: (0, 0))],
        out_specs=pl.BlockSpec((tm, bw), lambda i: (i, 0)),
        out_shape=jax.ShapeDtypeStruct((t, bw), BF16),
        compiler_params=_params("parallel"),
        name="gmlp",
    )(proj, ln_g.reshape(1, bw), ln_b.reshape(1, bw), w_s, b_s.T)


MOBA_MAX_BLOCKS = 32
MOBA_CHUNK_BLOCKS = 4
MOBA_Q_TILE_BLOCKS = 2
MOBA_PREP_ROWS = 512
LOG2_E = math.log2(math.e)
FEAT_SEL, FEAT_ONE, FEAT_BLK, FEAT_OFF, FEAT_END = 0, 32, 35, 38, 41


def _moba_key_features(seq):
    pos = jnp.arange(seq, dtype=jnp.int32)[:, None]
    kb, off = pos // MOBA_BLOCK, pos % MOBA_BLOCK
    lane = jnp.arange(LANES, dtype=jnp.int32)[None, :]
    feat = jnp.where(lane < FEAT_ONE, (lane == kb).astype(F32),
                     jnp.where(lane < FEAT_BLK, 1.0,
                               jnp.where(lane < FEAT_OFF, (kb * MOBA_BLOCK).astype(F32),
                                         jnp.where(lane < FEAT_END, off.astype(F32), 0.0))))
    return feat.astype(BF16)


def _split3(x):
    hi = x.astype(BF16).astype(F32)
    mid = (x - hi).astype(BF16).astype(F32)
    return hi, mid, x - hi - mid


def _moba_prepare(slope, q_ref, k_ref, kfeat_ref, kaug_ref, qaug_ref, kmh_ref, kml_ref, *, nb):
    blk = MOBA_BLOCK
    dh = MOBA_HEAD_DIM
    seq = q_ref.shape[0]
    prep = MOBA_PREP_ROWS
    nbp = MOBA_MAX_BLOCKS
    scale = dh ** -0.5 * LOG2_E
    slope = slope * LOG2_E

    kmh_ref[...] = jnp.zeros_like(kmh_ref)
    kml_ref[...] = jnp.zeros_like(kml_ref)
    for n in range(nb):
        rows = slice(n * blk, (n + 1) * blk)
        kn = k_ref[rows, :]
        kaug_ref[rows, :dh] = kn
        kaug_ref[rows, dh:] = kfeat_ref[rows, :]
        km = jnp.mean(kn.astype(F32), axis=0, keepdims=True)
        hi = km.astype(BF16)
        kmh_ref[n:n + 1, :] = hi
        kml_ref[n:n + 1, :] = (km - hi.astype(F32)).astype(BF16)

    blk_id = lax.broadcasted_iota(jnp.int32, (nbp, prep), 0)
    q_off = lax.broadcasted_iota(jnp.int32, (nbp, prep), 1)
    lane = lax.broadcasted_iota(jnp.int32, (prep, LANES), 1)
    row = lax.broadcasted_iota(jnp.int32, (prep, LANES), 0)
    part = jnp.where(lane < FEAT_BLK, lane - FEAT_ONE, jnp.where(lane < FEAT_OFF, lane - FEAT_BLK, lane - FEAT_OFF))
    pick3 = lambda parts: jnp.where(part == 0, parts[0], jnp.where(part == 1, parts[1], parts[2]))
    slope_feat = pick3(_split3(jnp.full((prep, LANES), slope, F32)))

    def rows_chunk(c, carry):
        row0 = pl.multiple_of(c * prep, prep)
        rows = pl.ds(row0, prep)
        q = q_ref[rows, :]
        gate = _dot_nt(kmh_ref[:nbp, :], q) + _dot_nt(kml_ref[:nbp, :], q)
        q_blk = (row0 + q_off) // blk
        gate = jnp.where(blk_id < q_blk, gate, -jnp.inf)
        sel_bias = jnp.where(blk_id == q_blk, 0.0, MASKED)
        for _ in range(MOBA_TOPK):
            best = jnp.max(gate, axis=0, keepdims=True)
            first = jnp.min(jnp.where(gate == best, blk_id, nbp), axis=0, keepdims=True)
            pick = (blk_id == first) & (best > -jnp.inf)
            sel_bias = jnp.where(pick, 0.0, sel_bias)
            gate = jnp.where(pick, -jnp.inf, gate)
        sel_rows = jnp.concatenate([sel_bias, jnp.zeros((LANES - nbp, prep), F32)], axis=0).T

        t_pos = (row0 + row).astype(F32)
        feat = jnp.where(lane < FEAT_ONE, sel_rows,
                         jnp.where(lane < FEAT_BLK, pick3(_split3(-slope * t_pos)),
                                   jnp.where(lane < FEAT_END, slope_feat, 0.0)))
        qaug_ref[rows, :dh] = (q.astype(F32) * scale).astype(BF16)
        qaug_ref[rows, dh:] = feat.astype(BF16)
        return carry

    lax.fori_loop(0, seq // prep, rows_chunk, 0)


def _moba_kernel(slopes_ref, q_ref, k_ref, v_ref, kfeat_ref, o_ref, kaug_ref, qaug_ref, kmh_ref, kml_ref, s_ref,
                 *, nb):
    blk = MOBA_BLOCK
    dh = MOBA_HEAD_DIM
    qt = MOBA_Q_TILE_BLOCKS * blk
    cb = MOBA_CHUNK_BLOCKS
    span = cb * blk
    h = pl.program_id(1)
    ti = pl.program_id(2)
    first_blk = ti * MOBA_Q_TILE_BLOCKS

    @pl.when(ti == 0)
    def _():
        _moba_prepare(slopes_ref[h], q_ref, k_ref, kfeat_ref, kaug_ref, qaug_ref, kmh_ref, kml_ref, nb=nb)

    q_aug = qaug_ref[pl.ds(pl.multiple_of(ti * qt, qt), qt), :]

    def chunk_rows(p):
        return pl.ds(pl.multiple_of(p * span, span), span)

    def scores(p):
        return _dot_nt(q_aug, kaug_ref[chunk_rows(p), :])

    def lanes_max(m, s):
        for c in range(0, s.shape[1], LANES):
            m = jnp.maximum(m, s[:, c:c + LANES])
        return m

    def pass1(p, mvec):
        s = scores(p)
        s_ref[p] = s
        return lanes_max(mvec, s)

    def chunk_loop(count, body, init):
        carry = lax.fori_loop(0, count // 2, lambda i, c: body(2 * i + 1, body(2 * i, c)), init)
        return lax.fori_loop(count - count % 2, count, body, carry)

    n_past = first_blk // cb
    mvec = chunk_loop(n_past, pass1, jnp.full((qt, LANES), -jnp.inf, F32))
    r_idx = lax.broadcasted_iota(jnp.int32, (blk, blk), 0)
    c_idx = lax.broadcasted_iota(jnp.int32, (blk, blk), 1)
    causal_bias = jnp.where(r_idx >= c_idx, 0.0, MASKED)
    own = first_blk % cb
    bias = jnp.concatenate(
        [jnp.concatenate([causal_bias * (own + r == c).astype(F32) for c in range(cb)], axis=1)
         for r in range(MOBA_Q_TILE_BLOCKS)], axis=0)
    s = scores(n_past) + bias
    s_ref[n_past] = s
    m = jnp.max(lanes_max(mvec, s), axis=-1, keepdims=True)

    def pass2(p, carry):
        l, acc = carry
        e = jnp.exp2(s_ref[p] - m)
        l = l + jnp.sum(e, axis=-1, keepdims=True)
        acc = acc + jnp.dot(e.astype(BF16), v_ref[chunk_rows(p), :], preferred_element_type=F32)
        return l, acc

    l, acc = chunk_loop(n_past + 1, pass2, (jnp.zeros((qt, 1), F32), jnp.zeros((qt, dh), F32)))
    o_ref[...] = (acc / l).astype(o_ref.dtype)


def moba(proj, slopes, *, batch, seq):
    blk = MOBA_BLOCK
    nb = seq // blk
    qt = MOBA_Q_TILE_BLOCKS * blk
    nt = seq // qt
    assert nb % MOBA_CHUNK_BLOCKS == 0 and nb <= MOBA_MAX_BLOCKS and seq % MOBA_PREP_ROWS == 0
    assert MOBA_CHUNK_BLOCKS % MOBA_Q_TILE_BLOCKS == 0
    dh = MOBA_HEAD_DIM
    cq, ck, cv = COL_MOBA_Q // dh, COL_MOBA_K // dh, COL_MOBA_V // dh
    return pl.pallas_call(
        functools.partial(_moba_kernel, nb=nb),
        grid_spec=pltpu.PrefetchScalarGridSpec(
            num_scalar_prefetch=1,
            grid=(batch, MOBA_HEADS, nt),
            in_specs=[pl.BlockSpec((seq, dh), lambda b, h, i, s: (b, cq + h)),
                      pl.BlockSpec((seq, dh), lambda b, h, i, s: (b, ck + h)),
                      pl.BlockSpec((seq, dh), lambda b, h, i, s: (b, cv + h)),
                      pl.BlockSpec((seq, LANES), lambda b, h, i, s: (0, 0))],
            out_specs=pl.BlockSpec((qt, dh), lambda b, h, i, s: (b * nt + i, h)),
            scratch_shapes=[pltpu.VMEM((seq, dh + LANES), BF16), pltpu.VMEM((seq, dh + LANES), BF16),
                            pltpu.VMEM((LANES, dh), BF16), pltpu.VMEM((LANES, dh), BF16),
                            pltpu.VMEM((nb // MOBA_CHUNK_BLOCKS, qt, MOBA_CHUNK_BLOCKS * blk), F32)],
        ),
        out_shape=jax.ShapeDtypeStruct((batch * seq, BRANCH_WIDTH), BF16),
        compiler_params=_params("parallel", "parallel", "arbitrary"),
        name="moba",
    )(slopes, proj, proj, proj, _moba_key_features(seq))


SWA_STEP_BLOCKS = 2


def _swa_kernel(slopes_ref, sinks_ref, q_ref, *refs):
    nsub = SWA_STEP_BLOCKS
    k_refs, v_refs, o_ref = refs[:nsub + 1], refs[nsub + 1:2 * nsub + 2], refs[2 * nsub + 2]
    w = SWA_WINDOW
    dh = SWA_HEAD_DIM
    n = pl.program_id(1)
    scale = dh ** -0.5
    low2 = lax.broadcasted_iota(jnp.int32, (2 * w, LANES), 1) < dh
    t_idx = lax.broadcasted_iota(jnp.int32, (w, 4 * w), 0)
    col = lax.broadcasted_iota(jnp.int32, (w, 4 * w), 1)
    s_idx = col % w
    is_prev = (col % (2 * w)) < w
    second = col >= 2 * w
    dist = (t_idx - s_idx).astype(F32) + jnp.where(is_prev, float(w), 0.0)
    in_window = (is_prev & (s_idx > t_idx)) | (jnp.logical_not(is_prev) & (s_idx <= t_idx))
    in_window_first = in_window & (jnp.logical_not(is_prev) | (n > 0))

    def stacked(prev_ref, cur_ref):
        x = jnp.concatenate([prev_ref[...], cur_ref[...]], axis=0).astype(F32)
        xr = pltpu.roll(x, dh, axis=1)
        z = jnp.zeros_like(x)
        head0 = jnp.concatenate([jnp.where(low2, x, z), jnp.where(low2, z, xr)], axis=0)
        head1 = jnp.concatenate([jnp.where(low2, xr, z), jnp.where(low2, z, x)], axis=0)
        return head0.astype(BF16), head1.astype(BF16)

    pairs = SWA_Q_HEADS // 2
    group = SWA_Q_HEADS // SWA_KV_HEADS
    for sub in range(nsub):
        rows = slice(sub * w, (sub + 1) * w)
        ok = in_window_first if sub == 0 else in_window
        k_st = stacked(k_refs[sub], k_refs[sub + 1])
        v_st = stacked(v_refs[sub], v_refs[sub + 1])
        for pr in range(pairs):
            q = q_ref[rows, pr * LANES:(pr + 1) * LANES]
            kh = (2 * pr) // group
            slope = jnp.where(second, slopes_ref[2 * pr + 1], slopes_ref[2 * pr])
            s = _dot_nt(q, k_st[kh]) * scale
            s = jnp.where(ok, s - slope * dist, MASKED)
            probs = []
            for half in range(2):
                sink = sinks_ref[2 * pr + half]
                sh = s[:, half * 2 * w:(half + 1) * 2 * w]
                m = jnp.maximum(jnp.max(sh, axis=-1, keepdims=True), sink)
                e = jnp.exp(sh - m)
                denom = jnp.sum(e, axis=-1, keepdims=True) + jnp.exp(sink - m)
                probs.append((e * (1.0 / denom)).astype(BF16))
            out = jnp.dot(jnp.concatenate(probs, axis=1), v_st[kh], preferred_element_type=F32)
            o_ref[rows, pr * LANES:(pr + 1) * LANES] = out.astype(o_ref.dtype)


def swa(proj, slopes, sinks, *, batch, seq):
    w = SWA_WINDOW
    nsub = SWA_STEP_BLOCKS
    nblk = seq // w
    nstep = nblk // nsub
    assert nblk % nsub == 0
    bw = BRANCH_WIDTH
    cq = COL_SWA_Q // bw
    ck = COL_SWA_K // SWA_KV_WIDTH
    cv = COL_SWA_V // SWA_KV_WIDTH

    def kv_spec(col_blk, j):
        return pl.BlockSpec((w, SWA_KV_WIDTH),
                            lambda b, n, *_: (b * nblk + jnp.maximum(n * nsub + j - 1, 0), col_blk))

    return pl.pallas_call(
        _swa_kernel,
        grid_spec=pltpu.PrefetchScalarGridSpec(
            num_scalar_prefetch=2,
            grid=(batch, nstep),
            in_specs=([pl.BlockSpec((nsub * w, bw), lambda b, n, *_: (b * nstep + n, cq))]
                      + [kv_spec(ck, j) for j in range(nsub + 1)]
                      + [kv_spec(cv, j) for j in range(nsub + 1)]),
            out_specs=pl.BlockSpec((nsub * w, bw), lambda b, n, *_: (b * nstep + n, 0)),
        ),
        out_shape=jax.ShapeDtypeStruct((batch * seq, bw), BF16),
        compiler_params=_params("parallel", "arbitrary"),
        name="swa",
    )(slopes, sinks, *([proj] * (2 * nsub + 3)))


def _merge_kernel(*refs, tiles):
    n = len(tiles)
    h_ref, r_ref, wg_ref0, wg_ref1, wg_ref2, bg_ref, ya_ref, yb_ref, yc_ref, wb_ref = refs[:10]
    o_ref = refs[10 + n]
    h = h_ref[...]
    r = r_ref[...]
    merged = None
    for br, (wg_ref, y_ref) in enumerate(((wg_ref0, ya_ref), (wg_ref1, yb_ref), (wg_ref2, yc_ref))):
        logits = r * jnp.dot(h, wg_ref[...], preferred_element_type=F32) + bg_ref[br]
        branch = jnp.dot(y_ref[...], wb_ref[br], preferred_element_type=F32)
        term = jax.nn.sigmoid(logits) * branch
        merged = term if merged is None else merged + term
    o_ref[...] = merged.astype(o_ref.dtype)
    _side_cast_run(tiles, refs[10:10 + n], refs[11 + n:])


def merge(h, r, w_gate, b_gate, y_a, y_b, y_c, w_branch, *, casts=(), tm=1024, tn=256):
    t, d = h.shape
    bw = BRANCH_WIDTH
    ni, nj = t // tm, d // tn
    c_in, c_out, c_shapes = _side_cast_plan(casts, ni * nj, lambda i, j: i * nj + j)
    gate_spec = lambda n: pl.BlockSpec((d, tn), lambda i, j: (0, n * nj + j))
    y_spec = pl.BlockSpec((tm, bw), lambda i, j: (i, 0))
    return pl.pallas_call(
        functools.partial(_merge_kernel, tiles=tuple(c.tile for c in casts)),
        grid=(ni, nj),
        in_specs=[pl.BlockSpec((tm, d), lambda i, j: (i, 0)),
                  pl.BlockSpec((tm, 1), lambda i, j: (i, 0)),
                  gate_spec(0), gate_spec(1), gate_spec(2),
                  pl.BlockSpec((N_BRANCHES, 1, tn), lambda i, j: (0, 0, j)),
                  y_spec, y_spec, y_spec,
                  pl.BlockSpec((N_BRANCHES, bw, tn), lambda i, j: (0, 0, j))] + c_in,
        out_specs=[pl.BlockSpec((tm, tn), lambda i, j: (i, j))] + c_out,
        out_shape=[jax.ShapeDtypeStruct((t, d), BF16)] + c_shapes,
        compiler_params=_params("arbitrary", "arbitrary"),
        name="merge",
    )(h, r, w_gate, w_gate, w_gate, b_gate.reshape(N_BRANCHES, 1, d), y_a, y_b, y_c, w_branch,
      *[c.src for c in casts])


def _alibi_slopes():
    i = jnp.arange(1, N_ALIBI_HEADS + 1, dtype=F32)
    s = jnp.exp2(-8.0 * i / N_ALIBI_HEADS)
    return s[:SWA_Q_HEADS], s[SWA_Q_HEADS:]


def kernel(x, ffn1_pre_g, ffn1_w_up, ffn1_w_down, ffn1_post_g, mix_pre_g, w_in, gmlp_ln_g, gmlp_ln_b,
           gmlp_w_s, gmlp_b_s, swa_sinks, w_gate, b_gate, w_branch, w_out, mix_post_g, ffn2_pre_g,
           ffn2_w_up, ffn2_w_down, ffn2_post_g):
    batch, seq, d = x.shape
    depth = ffn1_pre_g.shape[0]
    swa_slopes, moba_slopes = _alibi_slopes()
    xf = x.reshape(batch * seq, d)
    h, r = prenorm(xf, ffn1_pre_g[0])
    tiled = lambda w, i: _SideCast(w, i, MM_TILE)
    plain = lambda w, i: _SideCast(w, i, None)
    w_branch_rows = w_branch.reshape(depth, -1, d)
    w_up1 = ffn1_w_up[0].astype(BF16)
    for i in range(depth):
        act, w_down1, w_in_b, w_gate_b, w_branch_b, w_out_b = ffn_up(
            h, r, w_up1, casts=(tiled(ffn1_w_down, i), plain(w_in, i), plain(w_gate, i),
                                plain(w_branch_rows, i), tiled(w_out, i)))
        xf, h, r = mm_norm_res(act, w_down1, xf, ffn1_post_g[i], mix_pre_g[i], res_scale=0.5)

        proj = matmul(h, r, w_in_b)
        y_a = gmlp(proj, gmlp_ln_g[i], gmlp_ln_b[i], gmlp_w_s[i], gmlp_b_s[i])
        y_b = moba(proj, moba_slopes, batch=batch, seq=seq)
        y_c = swa(proj, swa_slopes, swa_sinks[i], batch=batch, seq=seq)
        merged, w_up2 = merge(h, r, w_gate_b, b_gate[i], y_a, y_b, y_c, w_branch_b.reshape(w_branch.shape[1:]),
                              casts=(plain(ffn2_w_up, i),))
        xf, h, r = mm_norm_res(merged, w_out_b, xf, mix_post_g[i], ffn2_pre_g[i], res_scale=1.0)

        last = i + 1 == depth
        casts = (tiled(ffn2_w_down, i),) + (() if last else (plain(ffn1_w_up, i + 1),))
        act, w_down2, *nxt = ffn_up(h, r, w_up2, casts=casts)
        xf, h, r = mm_norm_res(act, w_down2, xf, ffn2_post_g[i], None if last else ffn1_pre_g[i + 1],
                               res_scale=0.5)
        if not last:
            w_up1 = nxt[0]
    return xf.reshape(batch, seq, d)
```

```python
import functools
import math
from typing import NamedTuple, Optional, Tuple

import jax
import jax.numpy as jnp
from jax import lax
from jax.experimental import pallas as pl
from jax.experimental.pallas import tpu as pltpu

D_MODEL = 4096
DEPTH = 2
BRANCH_WIDTH = D_MODEL // 4
N_BRANCHES = 3
GMLP_CHUNK = 128
GMLP_GROUP_DIM = 128
GMLP_GROUPS = BRANCH_WIDTH // GMLP_GROUP_DIM
MOBA_HEAD_DIM = 128
MOBA_HEADS = BRANCH_WIDTH // MOBA_HEAD_DIM
MOBA_BLOCK = 256
MOBA_TOPK = 3
SWA_HEAD_DIM = 64
SWA_Q_HEADS = BRANCH_WIDTH // SWA_HEAD_DIM
SWA_KV_HEADS = SWA_Q_HEADS // 8
SWA_WINDOW = 128
SWA_KV_WIDTH = SWA_KV_HEADS * SWA_HEAD_DIM
IN_COLS = 6 * BRANCH_WIDTH + 2 * SWA_KV_WIDTH
D_FF = 2 * D_MODEL
N_ALIBI_HEADS = SWA_Q_HEADS + MOBA_HEADS
RMS_EPS = 1e-6
LN_EPS = 1e-5

LANES = 128
BF16_SUBLANES = 16
VMEM_LIMIT = 60 * 1024 * 1024
MM_TILE = (4096, 512)
A_LEAD_STEPS = 2
MASKED = -1e30

F32 = jnp.float32
BF16 = jnp.bfloat16

COL_GMLP = 0
COL_MOBA_Q = 2 * BRANCH_WIDTH
COL_MOBA_K = 3 * BRANCH_WIDTH
COL_MOBA_V = 4 * BRANCH_WIDTH
COL_SWA_Q = 5 * BRANCH_WIDTH
COL_SWA_K = 6 * BRANCH_WIDTH
COL_SWA_V = 6 * BRANCH_WIDTH + SWA_KV_WIDTH


def _params(*semantics):
    return pltpu.CompilerParams(dimension_semantics=semantics, vmem_limit_bytes=VMEM_LIMIT)


def _rms_scale(x):
    return lax.rsqrt(jnp.mean(x * x, axis=-1, keepdims=True) + RMS_EPS)


def _dot_nt(a, b):
    return lax.dot_general(a, b, (((1,), (1,)), ((), ())), preferred_element_type=F32)


def _prenorm_kernel(x_ref, g_ref, h_ref, r_ref):
    x = x_ref[...]
    h_ref[...] = (x * g_ref[...]).astype(h_ref.dtype)
    r_ref[...] = _rms_scale(x)


def prenorm(x, g, *, tm=256):
    t, d = x.shape
    return pl.pallas_call(
        _prenorm_kernel,
        grid=(t // tm,),
        in_specs=[pl.BlockSpec((tm, d), lambda i: (i, 0)),
                  pl.BlockSpec((1, d), lambda i: (0, 0))],
        out_specs=[pl.BlockSpec((tm, d), lambda i: (i, 0)),
                   pl.BlockSpec((tm, 1), lambda i: (i, 0))],
        out_shape=[jax.ShapeDtypeStruct((t, d), BF16), jax.ShapeDtypeStruct((t, 1), F32)],
        compiler_params=_params("parallel"),
        name="prenorm",
    )(x, g.reshape(1, d))


class _SideCast(NamedTuple):
    src: jax.Array
    layer: int
    tile: Optional[Tuple[int, int]]


def _side_cast_plan(casts, steps, step_of):
    in_specs, out_specs, out_shapes = [], [], []
    for c in casts:
        _, nrow, ncol = c.src.shape
        nblk = steps
        while nrow % nblk or (nrow // nblk) % BF16_SUBLANES:
            assert nblk % 2 == 0, (nrow, steps)
            nblk //= 2
        rows = nrow // nblk
        blk_of = lambda *ids, hold=steps // nblk: step_of(*ids) // hold
        in_specs.append(pl.BlockSpec((None, rows, ncol), lambda *ids, f=blk_of, layer=c.layer: (layer, f(*ids), 0)))
        if c.tile is None:
            out_specs.append(pl.BlockSpec((rows, ncol), lambda *ids, f=blk_of: (f(*ids), 0)))
            out_shapes.append(jax.ShapeDtypeStruct((nrow, ncol), BF16))
        else:
            tk, tn = c.tile
            per = tk // rows
            assert tk % rows == 0 and nrow % tk == 0 and ncol % tn == 0
            out_specs.append(pl.BlockSpec((None, ncol // tn, rows, tn),
                                          lambda *ids, f=blk_of, per=per: (f(*ids) // per, 0, f(*ids) % per, 0)))
            out_shapes.append(jax.ShapeDtypeStruct((nrow // tk, ncol // tn, tk, tn), BF16))
    return in_specs, out_specs, out_shapes


def _side_cast_run(tiles, src_refs, dst_refs):
    for tile, src, dst in zip(tiles, src_refs, dst_refs, strict=True):
        if tile is None:
            dst[...] = src[...].astype(BF16)
        else:
            tn = tile[1]
            for j in range(dst.shape[0]):
                dst[j] = src[:, j * tn:(j + 1) * tn].astype(BF16)


def _ffn_up_kernel(*refs, tiles):
    n = len(tiles)
    h_ref, r_ref, wg_ref, wu_ref = refs[:4]
    o_ref = refs[4 + n]
    h = h_ref[...]
    r = r_ref[...]
    g = r * jnp.dot(h, wg_ref[...], preferred_element_type=F32)
    u = r * jnp.dot(h, wu_ref[...], preferred_element_type=F32)
    o_ref[...] = (g * jax.nn.sigmoid(g) * u).astype(o_ref.dtype)
    _side_cast_run(tiles, refs[4:4 + n], refs[5 + n:])


def ffn_up(h, r, w_up, *, casts=(), tm=1024, tn=512):
    t, d = h.shape
    f = w_up.shape[1] // 2
    ni, nj = t // tm, f // tn
    c_in, c_out, c_shapes = _side_cast_plan(casts, ni * nj, lambda i, j: i * nj + j)
    return pl.pallas_call(
        functools.partial(_ffn_up_kernel, tiles=tuple(c.tile for c in casts)),
        grid=(ni, nj),
        in_specs=[pl.BlockSpec((tm, d), lambda i, j: (i, 0)),
                  pl.BlockSpec((tm, 1), lambda i, j: (i, 0)),
                  pl.BlockSpec((d, tn), lambda i, j: (0, j)),
                  pl.BlockSpec((d, tn), lambda i, j: (0, j + nj))] + c_in,
        out_specs=[pl.BlockSpec((tm, tn), lambda i, j: (i, j))] + c_out,
        out_shape=[jax.ShapeDtypeStruct((t, f), BF16)] + c_shapes,
        compiler_params=_params("arbitrary", "arbitrary"),
        name="ffn_up",
    )(h, r, w_up, w_up, *[c.src for c in casts])


def _mm_norm_res_kernel(*refs, res_scale, nkk, nj, ni, rows, emit_h):
    if emit_h:
        a_hbm, w_ref, x_ref, pg_ref, ng_ref, xo_ref, ho_ref, ro_ref, y_ref, a_buf, sy_ref, ry_ref, sx_ref, a_sem = refs
    else:
        a_hbm, w_ref, x_ref, pg_ref, xo_ref, y_ref, a_buf, sy_ref, ry_ref, a_sem = refs
    b = pl.program_id(0)
    s = pl.program_id(1)
    n_mm = nkk * nj
    tm, d = y_ref.shape
    tk, tn = w_ref.shape
    kk = s // nj
    q = b * nkk + kk
    a_ref = a_buf.at[q % 2]

    def a_copy(blk, part, slot):
        src = a_hbm.at[pl.ds(pl.multiple_of(blk * tm, tm), tm), pl.ds(pl.multiple_of(part * tk, tk), tk)]
        return pltpu.make_async_copy(src, a_buf.at[slot], a_sem.at[slot])

    @pl.when((b == 0) & (s == 0))
    def _():
        a_copy(0, 0, 0).start()

    @pl.when((b < ni) & (s % nj == 0))
    def _():
        a_copy(b, kk, q % 2).wait()

    last_tile = (b == ni - 1) & (kk == nkk - 1)

    @pl.when((b < ni) & (s % nj == A_LEAD_STEPS) & jnp.logical_not(last_tile))
    def _():
        wrap = kk == nkk - 1
        a_copy(jnp.where(wrap, b + 1, b), jnp.where(wrap, 0, kk + 1), (q + 1) % 2).start()

    def tile_cols(e):
        return pl.ds(pl.multiple_of(e * tn, tn), tn)

    def for_row_chunks(body):
        def step(c, carry):
            body(pl.ds(pl.multiple_of(c * rows, rows), rows))
            return carry
        lax.fori_loop(0, tm // rows, step, 0)

    @pl.when((b > 0) & (s == 0))
    def _():
        ry_ref[...] = lax.rsqrt(sy_ref[...] * (1.0 / d) + RMS_EPS)
        if emit_h:
            sx_ref[...] = jnp.zeros_like(sx_ref)

    def residual(r, cols):
        xn = x_ref[r, :] + res_scale * (y_ref[r, cols] * ry_ref[r, :] * pg_ref[...])
        xo_ref[r, :] = xn
        if emit_h:
            ho_ref[r, :] = (xn * ng_ref[...]).astype(BF16)
            sx_ref[r, :] += jnp.sum(xn * xn, axis=-1, keepdims=True)

    first_k = s < nj
    last_k = s >= n_mm - nj

    def matmul_step(first, last):
        cols = tile_cols(s % nj)
        y = jnp.dot(a_ref[...], w_ref[...], preferred_element_type=F32)
        if not first:
            y = y_ref[:, cols] + y
        y_ref[:, cols] = y
        if last:
            sq = jnp.sum(y * y, axis=-1, keepdims=True)
            sy_ref[...] = jnp.where(s == n_mm - nj, sq, sy_ref[...] + sq)

    @pl.when((b > 0) & (b < ni) & first_k)
    def _():
        for c in range(tm // rows):
            residual(pl.ds(c * rows, rows), tile_cols(s))
        matmul_step(True, nkk == 1)

    @pl.when((b == 0) & first_k)
    def _():
        matmul_step(True, nkk == 1)

    @pl.when((b == ni) & first_k)
    def _():
        for_row_chunks(lambda r: residual(r, tile_cols(s)))

    if emit_h:
        @pl.when((b > 0) & (s == nj - 1))
        def _():
            ro_ref[...] = lax.rsqrt(sx_ref[...] * (1.0 / d) + RMS_EPS)

    if nkk > 2:
        @pl.when((b < ni) & jnp.logical_not(first_k) & jnp.logical_not(last_k))
        def _():
            matmul_step(False, False)

    if nkk > 1:
        @pl.when((b < ni) & last_k)
        def _():
            matmul_step(False, True)


def mm_norm_res(a, w_tiled, x, post_g, next_g=None, *, res_scale, tm=1024, rows=256):
    t, kdim = a.shape
    nkk, nj, tkk, tn = w_tiled.shape
    d = nj * tn
    assert kdim == nkk * tkk
    ni = t // tm
    n_mm = nkk * nj
    emit_h = next_g is not None
    mm_step = lambda b, s: jnp.where(b < ni, s, n_mm - 1)
    ep_row = lambda b: jnp.maximum(b - 1, 0)
    ep_tile = lambda b, s: jnp.where(b > 0, jnp.minimum(s, nj - 1), 0)
    tile_spec = pl.BlockSpec((tm, tn), lambda b, s: (ep_row(b), ep_tile(b, s)))
    gain_spec = pl.BlockSpec((1, tn), lambda b, s: (0, ep_tile(b, s)))
    operands = [a, w_tiled, x, post_g.reshape(1, d)]
    in_specs = [pl.BlockSpec(memory_space=pl.ANY),
                pl.BlockSpec((None, None, tkk, tn), lambda b, s: (mm_step(b, s) // nj, mm_step(b, s) % nj, 0, 0)),
                tile_spec, gain_spec]
    out_shape = [jax.ShapeDtypeStruct((t, d), F32)]
    out_specs = [tile_spec]
    row_stat = pltpu.VMEM((tm, 1), F32)
    scratch = [pltpu.VMEM((tm, d), F32), pltpu.VMEM((2, tm, tkk), BF16), row_stat, row_stat]
    if emit_h:
        operands.append(next_g.reshape(1, d))
        in_specs.append(gain_spec)
        out_shape += [jax.ShapeDtypeStruct((t, d), BF16), jax.ShapeDtypeStruct((t, 1), F32)]
        out_specs += [tile_spec, pl.BlockSpec((tm, 1), lambda b, s: (ep_row(b), 0))]
        scratch.append(row_stat)
    scratch.append(pltpu.SemaphoreType.DMA((2,)))
    assert nj > A_LEAD_STEPS
    res = pl.pallas_call(
        functools.partial(_mm_norm_res_kernel, res_scale=res_scale, nkk=nkk, nj=nj, ni=ni, rows=rows,
                          emit_h=emit_h),
        grid=(ni + 1, n_mm),
        in_specs=in_specs,
        out_specs=out_specs,
        out_shape=out_shape,
        scratch_shapes=scratch,
        compiler_params=_params("arbitrary", "arbitrary"),
        name="mm_norm_res",
    )(*operands)
    return (res[0], res[1], res[2]) if emit_h else (res[0], None, None)


def _matmul_kernel(a_ref, r_ref, w_ref, o_ref):
    acc = jnp.dot(a_ref[...], w_ref[...], preferred_element_type=F32)
    o_ref[...] = (r_ref[...] * acc).astype(o_ref.dtype)


def matmul(a, r, w, *, tm=1024, tn=1280):
    t, kdim = a.shape
    n = w.shape[1]
    return pl.pallas_call(
        _matmul_kernel,
        grid=(t // tm, n // tn),
        in_specs=[pl.BlockSpec((tm, kdim), lambda i, j: (i, 0)),
                  pl.BlockSpec((tm, 1), lambda i, j: (i, 0)),
                  pl.BlockSpec((kdim, tn), lambda i, j: (0, j))],
        out_specs=pl.BlockSpec((tm, tn), lambda i, j: (i, j)),
        out_shape=jax.ShapeDtypeStruct((t, n), BF16),
        compiler_params=_params("parallel", "arbitrary"),
        name="in_proj",
    )(a, r, w)


def _gelu_exact(x):
    return 0.5 * x * (1.0 + lax.erf(x * (1.0 / math.sqrt(2.0))))


def _gmlp_kernel(z_ref, lng_ref, lnb_ref, ws_ref, bs_ref, o_ref, *, chunks):
    bw = BRANCH_WIDTH
    row = lax.broadcasted_iota(jnp.int32, (GMLP_CHUNK, GMLP_CHUNK), 0)
    col = lax.broadcasted_iota(jnp.int32, (GMLP_CHUNK, GMLP_CHUNK), 1)
    causal = row >= col
    for c in range(chunks):
        r = slice(c * GMLP_CHUNK, (c + 1) * GMLP_CHUNK)
        u = _gelu_exact(z_ref[r, :bw].astype(F32))
        v = _gelu_exact(z_ref[r, bw:].astype(F32))
        mu = jnp.mean(v, axis=-1, keepdims=True)
        vc = v - mu
        var = jnp.mean(vc * vc, axis=-1, keepdims=True)
        vn = (vc * lax.rsqrt(var + LN_EPS) * lng_ref[...] + lnb_ref[...]).astype(BF16)
        for g in range(GMLP_GROUPS):
            gs = slice(g * GMLP_GROUP_DIM, (g + 1) * GMLP_GROUP_DIM)
            w = jnp.where(causal, ws_ref[g], 0.0).astype(BF16)
            mixed = jnp.dot(w, vn[:, gs], preferred_element_type=F32) + bs_ref[:, g:g + 1]
            o_ref[r, gs] = (u[:, gs] * mixed).astype(o_ref.dtype)


def gmlp(proj, ln_g, ln_b, w_s, b_s, *, chunks=4):
    t = proj.shape[0]
    bw = BRANCH_WIDTH
    tm = chunks * GMLP_CHUNK
    return pl.pallas_call(
        functools.partial(_gmlp_kernel, chunks=chunks),
        grid=(t // tm,),
        in_specs=[pl.BlockSpec((tm, 2 * bw), lambda i: (i, 0)),
                  pl.BlockSpec((1, bw), lambda i: (0, 0)),
                  pl.BlockSpec((1, bw), lambda i: (0, 0)),
                  pl.BlockSpec((GMLP_GROUPS, GMLP_CHUNK, GMLP_CHUNK), lambda i: (0, 0, 0)),
                  pl.BlockSpec((GMLP_CHUNK, GMLP_GROUPS), lambda i: (0, 0))],
        out_specs=pl.BlockSpec((tm, bw), lambda i: (i, 0)),
        out_shape=jax.ShapeDtypeStruct((t, bw), BF16),
        compiler_params=_params("parallel"),
        name="gmlp",
    )(proj, ln_g.reshape(1, bw), ln_b.reshape(1, bw), w_s, b_s.T)


MOBA_MAX_BLOCKS = 32
MOBA_CHUNK_BLOCKS = 4
MOBA_Q_TILE_BLOCKS = 2
MOBA_PREP_ROWS = 512
LOG2_E = math.log2(math.e)
FEAT_SEL, FEAT_ONE, FEAT_BLK, FEAT_OFF, FEAT_END = 0, 32, 35, 38, 41


def _moba_key_features(seq):
    pos = jnp.arange(seq, dtype=jnp.int32)[:, None]
    kb, off = pos // MOBA_BLOCK, pos % MOBA_BLOCK
    lane = jnp.arange(LANES, dtype=jnp.int32)[None, :]
    feat = jnp.where(lane < FEAT_ONE, (lane == kb).astype(F32),
                     jnp.where(lane < FEAT_BLK, 1.0,
                               jnp.where(lane < FEAT_OFF, (kb * MOBA_BLOCK).astype(F32),
                                         jnp.where(lane < FEAT_END, off.astype(F32), 0.0))))
    return feat.astype(BF16)


def _split3(x):
    hi = x.astype(BF16).astype(F32)
    mid = (x - hi).astype(BF16).astype(F32)
    return hi, mid, x - hi - mid


def _moba_prepare(slope, q_ref, k_ref, kfeat_ref, kaug_ref, qaug_ref, kmh_ref, kml_ref, *, nb):
    blk = MOBA_BLOCK
    dh = MOBA_HEAD_DIM
    seq = q_ref.shape[0]
    prep = MOBA_PREP_ROWS
    nbp = MOBA_MAX_BLOCKS
    scale = dh ** -0.5 * LOG2_E
    slope = slope * LOG2_E

    kmh_ref[...] = jnp.zeros_like(kmh_ref)
    kml_ref[...] = jnp.zeros_like(kml_ref)
    for n in range(nb):
        rows = slice(n * blk, (n + 1) * blk)
        kn = k_ref[rows, :]
        kaug_ref[rows, :dh] = kn
        kaug_ref[rows, dh:] = kfeat_ref[rows, :]
        km = jnp.mean(kn.astype(F32), axis=0, keepdims=True)
        hi = km.astype(BF16)
        kmh_ref[n:n + 1, :] = hi
        kml_ref[n:n + 1, :] = (km - hi.astype(F32)).astype(BF16)

    blk_id = lax.broadcasted_iota(jnp.int32, (nbp, prep), 0)
    q_off = lax.broadcasted_iota(jnp.int32, (nbp, prep), 1)
    lane = lax.broadcasted_iota(jnp.int32, (prep, LANES), 1)
    row = lax.broadcasted_iota(jnp.int32, (prep, LANES), 0)
    part = jnp.where(lane < FEAT_BLK, lane - FEAT_ONE, jnp.where(lane < FEAT_OFF, lane - FEAT_BLK, lane - FEAT_OFF))
    pick3 = lambda parts: jnp.where(part == 0, parts[0], jnp.where(part == 1, parts[1], parts[2]))
    slope_feat = pick3(_split3(jnp.full((prep, LANES), slope, F32)))

    def rows_chunk(c, carry):
        row0 = pl.multiple_of(c * prep, prep)
        rows = pl.ds(row0, prep)
        q = q_ref[rows, :]
        gate = _dot_nt(kmh_ref[:nbp, :], q) + _dot_nt(kml_ref[:nbp, :], q)
        q_blk = (row0 + q_off) // blk
        gate = jnp.where(blk_id < q_blk, gate, -jnp.inf)
        sel_bias = jnp.where(blk_id == q_blk, 0.0, MASKED)
        for _ in range(MOBA_TOPK):
            best = jnp.max(gate, axis=0, keepdims=True)
            first = jnp.min(jnp.where(gate == best, blk_id, nbp), axis=0, keepdims=True)
            pick = (blk_id == first) & (best > -jnp.inf)
            sel_bias = jnp.where(pick, 0.0, sel_bias)
            gate = jnp.where(pick, -jnp.inf, gate)
        sel_rows = jnp.concatenate([sel_bias, jnp.zeros((LANES - nbp, prep), F32)], axis=0).T

        t_pos = (row0 + row).astype(F32)
        feat = jnp.where(lane < FEAT_ONE, sel_rows,
                         jnp.where(lane < FEAT_BLK, pick3(_split3(-slope * t_pos)),
                                   jnp.where(lane < FEAT_END, slope_feat, 0.0)))
        qaug_ref[rows, :dh] = (q.astype(F32) * scale).astype(BF16)
        qaug_ref[rows, dh:] = feat.astype(BF16)
        return carry

    lax.fori_loop(0, seq // prep, rows_chunk, 0)


def _moba_kernel(slopes_ref, q_ref, k_ref, v_ref, kfeat_ref, o_ref, kaug_ref, qaug_ref, kmh_ref, kml_ref, s_ref,
                 *, nb):
    blk = MOBA_BLOCK
    dh = MOBA_HEAD_DIM
    qt = MOBA_Q_TILE_BLOCKS * blk
    cb = MOBA_CHUNK_BLOCKS
    span = cb * blk
    h = pl.program_id(1)
    ti = pl.program_id(2)
    first_blk = ti * MOBA_Q_TILE_BLOCKS

    @pl.when(ti == 0)
    def _():
        _moba_prepare(slopes_ref[h], q_ref, k_ref, kfeat_ref, kaug_ref, qaug_ref, kmh_ref, kml_ref, nb=nb)

    q_aug = qaug_ref[pl.ds(pl.multiple_of(ti * qt, qt), qt), :]

    def chunk_rows(p):
        return pl.ds(pl.multiple_of(p * span, span), span)

    def scores(p):
        return _dot_nt(q_aug, kaug_ref[chunk_rows(p), :])

    def lanes_max(m, s):
        for c in range(0, s.shape[1], LANES):
            m = jnp.maximum(m, s[:, c:c + LANES])
        return m

    def pass1(p, mvec):
        s = scores(p)
        s_ref[p] = s
        return lanes_max(mvec, s)

    def chunk_loop(count, body, init):
        carry = lax.fori_loop(0, count // 2, lambda i, c: body(2 * i + 1, body(2 * i, c)), init)
        return lax.fori_loop(count - count % 2, count, body, carry)

    n_past = first_blk // cb
    mvec = chunk_loop(n_past, pass1, jnp.full((qt, LANES), -jnp.inf, F32))
    r_idx = lax.broadcasted_iota(jnp.int32, (blk, blk), 0)
    c_idx = lax.broadcasted_iota(jnp.int32, (blk, blk), 1)
    causal_bias = jnp.where(r_idx >= c_idx, 0.0, MASKED)
    own = first_blk % cb
    bias = jnp.concatenate(
        [jnp.concatenate([causal_bias * (own + r == c).astype(F32) for c in range(cb)], axis=1)
         for r in range(MOBA_Q_TILE_BLOCKS)], axis=0)
    s = scores(n_past) + bias
    s_ref[n_past] = s
    m = jnp.max(lanes_max(mvec, s), axis=-1, keepdims=True)

    def pass2(p, carry):
        l, acc = carry
        e = jnp.exp2(s_ref[p] - m)
        l = l + jnp.sum(e, axis=-1, keepdims=True)
        acc = acc + jnp.dot(e.astype(BF16), v_ref[chunk_rows(p), :], preferred_element_type=F32)
        return l, acc

    l, acc = chunk_loop(n_past + 1, pass2, (jnp.zeros((qt, 1), F32), jnp.zeros((qt, dh), F32)))
    o_ref[...] = (acc / l).astype(o_ref.dtype)


def moba(proj, slopes, *, batch, seq):
    blk = MOBA_BLOCK
    nb = seq // blk
    qt = MOBA_Q_TILE_BLOCKS * blk
    nt = seq // qt
    assert nb % MOBA_CHUNK_BLOCKS == 0 and nb <= MOBA_MAX_BLOCKS and seq % MOBA_PREP_ROWS == 0
    assert MOBA_CHUNK_BLOCKS % MOBA_Q_TILE_BLOCKS == 0
    dh = MOBA_HEAD_DIM
    cq, ck, cv = COL_MOBA_Q // dh, COL_MOBA_K // dh, COL_MOBA_V // dh
    return pl.pallas_call(
        functools.partial(_moba_kernel, nb=nb),
        grid_spec=pltpu.PrefetchScalarGridSpec(
            num_scalar_prefetch=1,
            grid=(batch, MOBA_HEADS, nt),
            in_specs=[pl.BlockSpec((seq, dh), lambda b, h, i, s: (b, cq + h)),
                      pl.BlockSpec((seq, dh), lambda b, h, i, s: (b, ck + h)),
                      pl.BlockSpec((seq, dh), lambda b, h, i, s: (b, cv + h)),
                      pl.BlockSpec((seq, LANES), lambda b, h, i, s: (0, 0))],
            out_specs=pl.BlockSpec((qt, dh), lambda b, h, i, s: (b * nt + i, h)),
            scratch_shapes=[pltpu.VMEM((seq, dh + LANES), BF16), pltpu.VMEM((seq, dh + LANES), BF16),
                            pltpu.VMEM((LANES, dh), BF16), pltpu.VMEM((LANES, dh), BF16),
                            pltpu.VMEM((nb // MOBA_CHUNK_BLOCKS, qt, MOBA_CHUNK_BLOCKS * blk), F32)],
        ),
        out_shape=jax.ShapeDtypeStruct((batch * seq, BRANCH_WIDTH), BF16),
        compiler_params=_params("parallel", "parallel", "arbitrary"),
        name="moba",
    )(slopes, proj, proj, proj, _moba_key_features(seq))


SWA_STEP_BLOCKS = 4


def _swa_kernel(slopes_ref, sinks_ref, q_ref, *refs):
    nsub = SWA_STEP_BLOCKS
    k_refs, v_refs, o_ref = refs[:nsub + 1], refs[nsub + 1:2 * nsub + 2], refs[2 * nsub + 2]
    w = SWA_WINDOW
    dh = SWA_HEAD_DIM
    n = pl.program_id(1)
    scale = dh ** -0.5
    low2 = lax.broadcasted_iota(jnp.int32, (2 * w, LANES), 1) < dh
    t_idx = lax.broadcasted_iota(jnp.int32, (w, 4 * w), 0)
    col = lax.broadcasted_iota(jnp.int32, (w, 4 * w), 1)
    s_idx = col % w
    is_prev = (col % (2 * w)) < w
    second = col >= 2 * w
    dist = (t_idx - s_idx).astype(F32) + jnp.where(is_prev, float(w), 0.0)
    in_window = (is_prev & (s_idx > t_idx)) | (jnp.logical_not(is_prev) & (s_idx <= t_idx))
    in_window_first = in_window & (jnp.logical_not(is_prev) | (n > 0))

    def stacked(prev_ref, cur_ref):
        x = jnp.concatenate([prev_ref[...], cur_ref[...]], axis=0).astype(F32)
        xr = pltpu.roll(x, dh, axis=1)
        z = jnp.zeros_like(x)
        head0 = jnp.concatenate([jnp.where(low2, x, z), jnp.where(low2, z, xr)], axis=0)
        head1 = jnp.concatenate([jnp.where(low2, xr, z), jnp.where(low2, z, x)], axis=0)
        return head0.astype(BF16), head1.astype(BF16)

    pairs = SWA_Q_HEADS // 2
    group = SWA_Q_HEADS // SWA_KV_HEADS
    for sub in range(nsub):
        rows = slice(sub * w, (sub + 1) * w)
        ok = in_window_first if sub == 0 else in_window
        k_st = stacked(k_refs[sub], k_refs[sub + 1])
        v_st = stacked(v_refs[sub], v_refs[sub + 1])
        for pr in range(pairs):
            q = q_ref[rows, pr * LANES:(pr + 1) * LANES]
            kh = (2 * pr) // group
            slope = jnp.where(second, slopes_ref[2 * pr + 1], slopes_ref[2 * pr])
            s = _dot_nt(q, k_st[kh]) * scale
            s = jnp.where(ok, s - slope * dist, MASKED)
            probs = []
            for half in range(2):
                sink = sinks_ref[2 * pr + half]
                sh = s[:, half * 2 * w:(half + 1) * 2 * w]
                m = jnp.maximum(jnp.max(sh, axis=-1, keepdims=True), sink)
                e = jnp.exp(sh - m)
                denom = jnp.sum(e, axis=-1, keepdims=True) + jnp.exp(sink - m)
                probs.append((e * (1.0 / denom)).astype(BF16))
            out = jnp.dot(jnp.concatenate(probs, axis=1), v_st[kh], preferred_element_type=F32)
            o_ref[rows, pr * LANES:(pr + 1) * LANES] = out.astype(o_ref.dtype)


def swa(proj, slopes, sinks, *, batch, seq):
    w = SWA_WINDOW
    nsub = SWA_STEP_BLOCKS
    nblk = seq // w
    nstep = nblk // nsub
    assert nblk % nsub == 0
    bw = BRANCH_WIDTH
    cq = COL_SWA_Q // bw
    ck = COL_SWA_K // SWA_KV_WIDTH
    cv = COL_SWA_V // SWA_KV_WIDTH

    def kv_spec(col_blk, j):
        return pl.BlockSpec((w, SWA_KV_WIDTH),
                            lambda b, n, *_: (b * nblk + jnp.maximum(n * nsub + j - 1, 0), col_blk))

    return pl.pallas_call(
        _swa_kernel,
        grid_spec=pltpu.PrefetchScalarGridSpec(
            num_scalar_prefetch=2,
            grid=(batch, nstep),
            in_specs=([pl.BlockSpec((nsub * w, bw), lambda b, n, *_: (b * nstep + n, cq))]
                      + [kv_spec(ck, j) for j in range(nsub + 1)]
                      + [kv_spec(cv, j) for j in range(nsub + 1)]),
            out_specs=pl.BlockSpec((nsub * w, bw), lambda b, n, *_: (b * nstep + n, 0)),
        ),
        out_shape=jax.ShapeDtypeStruct((batch * seq, bw), BF16),
        compiler_params=_params("parallel", "arbitrary"),
        name="swa",
    )(slopes, sinks, *([proj] * (2 * nsub + 3)))


def _local_mixers_kernel(slopes_ref, sinks_ref, z_ref, lng_ref, lnb_ref, ws_ref, bs_ref, q_ref, *refs):
    ya_ref, yc_ref = refs[-2:]
    _gmlp_kernel(z_ref, lng_ref, lnb_ref, ws_ref, bs_ref, ya_ref, chunks=z_ref.shape[0] // GMLP_CHUNK)
    _swa_kernel(slopes_ref, sinks_ref, q_ref, *refs[:-2], yc_ref)


def local_mixers(proj, ln_g, ln_b, w_s, b_s, slopes, sinks, *, batch, seq):
    w = SWA_WINDOW
    nsub = SWA_STEP_BLOCKS
    rows = nsub * w
    nblk = seq // w
    nstep = nblk // nsub
    assert nblk % nsub == 0 and rows % GMLP_CHUNK == 0
    bw = BRANCH_WIDTH
    cq = COL_SWA_Q // bw
    ck = COL_SWA_K // SWA_KV_WIDTH
    cv = COL_SWA_V // SWA_KV_WIDTH
    const = lambda *shape: pl.BlockSpec(shape, lambda b, n, *_: (0,) * len(shape))
    row_block = lambda col_blk, width: pl.BlockSpec((rows, width), lambda b, n, *_: (b * nstep + n, col_blk))

    def kv_spec(col_blk, j):
        return pl.BlockSpec((w, SWA_KV_WIDTH),
                            lambda b, n, *_: (b * nblk + jnp.maximum(n * nsub + j - 1, 0), col_blk))

    out = jax.ShapeDtypeStruct((batch * seq, bw), BF16)
    return pl.pallas_call(
        _local_mixers_kernel,
        grid_spec=pltpu.PrefetchScalarGridSpec(
            num_scalar_prefetch=2,
            grid=(batch, nstep),
            in_specs=([row_block(0, 2 * bw), const(1, bw), const(1, bw),
                       const(GMLP_GROUPS, GMLP_CHUNK, GMLP_CHUNK), const(GMLP_CHUNK, GMLP_GROUPS),
                       row_block(cq, bw)]
                      + [kv_spec(ck, j) for j in range(nsub + 1)]
                      + [kv_spec(cv, j) for j in range(nsub + 1)]),
            out_specs=[row_block(0, bw), row_block(0, bw)],
        ),
        out_shape=[out, out],
        compiler_params=_params("parallel", "arbitrary"),
        name="local_mixers",
    )(slopes, sinks, proj, ln_g.reshape(1, bw), ln_b.reshape(1, bw), w_s, b_s.T, *([proj] * (2 * nsub + 3)))


def _merge_kernel(*refs, tiles):
    n = len(tiles)
    h_ref, r_ref, wg_ref0, wg_ref1, wg_ref2, bg_ref, ya_ref, yb_ref, yc_ref, wb_ref = refs[:10]
    o_ref = refs[10 + n]
    h = h_ref[...]
    r = r_ref[...]
    merged = None
    for br, (wg_ref, y_ref) in enumerate(((wg_ref0, ya_ref), (wg_ref1, yb_ref), (wg_ref2, yc_ref))):
        logits = r * jnp.dot(h, wg_ref[...], preferred_element_type=F32) + bg_ref[br]
        branch = jnp.dot(y_ref[...], wb_ref[br], preferred_element_type=F32)
        term = jax.nn.sigmoid(logits) * branch
        merged = term if merged is None else merged + term
    o_ref[...] = merged.astype(o_ref.dtype)
    _side_cast_run(tiles, refs[10:10 + n], refs[11 + n:])


def merge(h, r, w_gate, b_gate, y_a, y_b, y_c, w_branch, *, casts=(), tm=1024, tn=256):
    t, d = h.shape
    bw = BRANCH_WIDTH
    ni, nj = t // tm, d // tn
    c_in, c_out, c_shapes = _side_cast_plan(casts, ni * nj, lambda i, j: i * nj + j)
    gate_spec = lambda n: pl.BlockSpec((d, tn), lambda i, j: (0, n * nj + j))
    y_spec = pl.BlockSpec((tm, bw), lambda i, j: (i, 0))
    return pl.pallas_call(
        functools.partial(_merge_kernel, tiles=tuple(c.tile for c in casts)),
        grid=(ni, nj),
        in_specs=[pl.BlockSpec((tm, d), lambda i, j: (i, 0)),
                  pl.BlockSpec((tm, 1), lambda i, j: (i, 0)),
                  gate_spec(0), gate_spec(1), gate_spec(2),
                  pl.BlockSpec((N_BRANCHES, 1, tn), lambda i, j: (0, 0, j)),
                  y_spec, y_spec, y_spec,
                  pl.BlockSpec((N_BRANCHES, bw, tn), lambda i, j: (0, 0, j))] + c_in,
        out_specs=[pl.BlockSpec((tm, tn), lambda i, j: (i, j))] + c_out,
        out_shape=[jax.ShapeDtypeStruct((t, d), BF16)] + c_shapes,
        compiler_params=_params("arbitrary", "arbitrary"),
        name="merge",
    )(h, r, w_gate, w_gate, w_gate, b_gate.reshape(N_BRANCHES, 1, d), y_a, y_b, y_c, w_branch,
      *[c.src for c in casts])


def _alibi_slopes():
    i = jnp.arange(1, N_ALIBI_HEADS + 1, dtype=F32)
    s = jnp.exp2(-8.0 * i / N_ALIBI_HEADS)
    return s[:SWA_Q_HEADS], s[SWA_Q_HEADS:]


def kernel(x, ffn1_pre_g, ffn1_w_up, ffn1_w_down, ffn1_post_g, mix_pre_g, w_in, gmlp_ln_g, gmlp_ln_b,
           gmlp_w_s, gmlp_b_s, swa_sinks, w_gate, b_gate, w_branch, w_out, mix_post_g, ffn2_pre_g,
           ffn2_w_up, ffn2_w_down, ffn2_post_g):
    batch, seq, d = x.shape
    depth = ffn1_pre_g.shape[0]
    swa_slopes, moba_slopes = _alibi_slopes()
    xf = x.reshape(batch * seq, d)
    h, r = prenorm(xf, ffn1_pre_g[0])
    tiled = lambda w, i: _SideCast(w, i, MM_TILE)
    plain = lambda w, i: _SideCast(w, i, None)
    w_branch_rows = w_branch.reshape(depth, -1, d)
    w_up1 = ffn1_w_up[0].astype(BF16)
    for i in range(depth):
        act, w_down1, w_in_b, w_gate_b, w_branch_b, w_out_b = ffn_up(
            h, r, w_up1, casts=(tiled(ffn1_w_down, i), plain(w_in, i), plain(w_gate, i),
                                plain(w_branch_rows, i), tiled(w_out, i)))
        xf, h, r = mm_norm_res(act, w_down1, xf, ffn1_post_g[i], mix_pre_g[i], res_scale=0.5)

        proj = matmul(h, r, w_in_b)
        y_a, y_c = local_mixers(proj, gmlp_ln_g[i], gmlp_ln_b[i], gmlp_w_s[i], gmlp_b_s[i], swa_slopes, swa_sinks[i],
                                batch=batch, seq=seq)
        y_b = moba(proj, moba_slopes, batch=batch, seq=seq)
        merged, w_up2 = merge(h, r, w_gate_b, b_gate[i], y_a, y_b, y_c, w_branch_b.reshape(w_branch.shape[1:]),
                              casts=(plain(ffn2_w_up, i),))
        xf, h, r = mm_norm_res(merged, w_out_b, xf, mix_post_g[i], ffn2_pre_g[i], res_scale=1.0)

        last = i + 1 == depth
        casts = (tiled(ffn2_w_down, i),) + (() if last else (plain(ffn1_w_up, i + 1),))
        act, w_down2, *nxt = ffn_up(h, r, w_up2, casts=casts)
        xf, h, r = mm_norm_res(act, w_down2, xf, ffn2_post_g[i], None if last else ffn1_pre_g[i + 1],
                               res_scale=0.5)
        if not last:
            w_up1 = nxt[0]
    return xf.reshape(batch, seq, d)
```

```python
import functools
import math
from typing import NamedTuple, Optional, Tuple

import jax
import jax.numpy as jnp
from jax import lax
from jax.experimental import pallas as pl
from jax.experimental.pallas import tpu as pltpu

D_MODEL = 4096
DEPTH = 2
BRANCH_WIDTH = D_MODEL // 4
N_BRANCHES = 3
GMLP_CHUNK = 128
GMLP_GROUP_DIM = 128
GMLP_GROUPS = BRANCH_WIDTH // GMLP_GROUP_DIM
MOBA_HEAD_DIM = 128
MOBA_HEADS = BRANCH_WIDTH // MOBA_HEAD_DIM
MOBA_BLOCK = 256
MOBA_TOPK = 3
SWA_HEAD_DIM = 64
SWA_Q_HEADS = BRANCH_WIDTH // SWA_HEAD_DIM
SWA_KV_HEADS = SWA_Q_HEADS // 8
SWA_WINDOW = 128
SWA_KV_WIDTH = SWA_KV_HEADS * SWA_HEAD_DIM
IN_COLS = 6 * BRANCH_WIDTH + 2 * SWA_KV_WIDTH
D_FF = 2 * D_MODEL
N_ALIBI_HEADS = SWA_Q_HEADS + MOBA_HEADS
RMS_EPS = 1e-6
LN_EPS = 1e-5

LANES = 128
BF16_SUBLANES = 16
VMEM_LIMIT = 60 * 1024 * 1024
MM_TILE = (4096, 512)
A_LEAD_STEPS = 2
MASKED = -1e30

F32 = jnp.float32
BF16 = jnp.bfloat16

COL_GMLP = 0
COL_MOBA_Q = 2 * BRANCH_WIDTH
COL_MOBA_K = 3 * BRANCH_WIDTH
COL_MOBA_V = 4 * BRANCH_WIDTH
COL_SWA_Q = 5 * BRANCH_WIDTH
COL_SWA_K = 6 * BRANCH_WIDTH
COL_SWA_V = 6 * BRANCH_WIDTH + SWA_KV_WIDTH


def _params(*semantics):
    return pltpu.CompilerParams(dimension_semantics=semantics, vmem_limit_bytes=VMEM_LIMIT)


def _rms_scale(x):
    return lax.rsqrt(jnp.mean(x * x, axis=-1, keepdims=True) + RMS_EPS)


def _dot_nt(a, b):
    return lax.dot_general(a, b, (((1,), (1,)), ((), ())), preferred_element_type=F32)


def _prenorm_kernel(*refs, tiles):
    n = len(tiles)
    x_ref, g_ref = refs[:2]
    h_ref, r_ref = refs[2 + n:4 + n]
    x = x_ref[...]
    h_ref[...] = (x * g_ref[...]).astype(h_ref.dtype)
    r_ref[...] = _rms_scale(x)
    _side_cast_run(tiles, refs[2:2 + n], refs[4 + n:])


def prenorm(x, g, *, casts=(), tm=256):
    t, d = x.shape
    c_in, c_out, c_shapes = _side_cast_plan(casts, t // tm, lambda i: i)
    return pl.pallas_call(
        functools.partial(_prenorm_kernel, tiles=tuple(c.tile for c in casts)),
        grid=(t // tm,),
        in_specs=[pl.BlockSpec((tm, d), lambda i: (i, 0)),
                  pl.BlockSpec((1, d), lambda i: (0, 0))] + c_in,
        out_specs=[pl.BlockSpec((tm, d), lambda i: (i, 0)),
                   pl.BlockSpec((tm, 1), lambda i: (i, 0))] + c_out,
        out_shape=[jax.ShapeDtypeStruct((t, d), BF16), jax.ShapeDtypeStruct((t, 1), F32)] + c_shapes,
        compiler_params=_params("arbitrary"),
        name="prenorm",
    )(x, g.reshape(1, d), *[c.src for c in casts])


class _SideCast(NamedTuple):
    src: jax.Array
    layer: int
    tile: Optional[Tuple[int, int]]


def _side_cast_plan(casts, steps, step_of):
    in_specs, out_specs, out_shapes = [], [], []
    for c in casts:
        _, nrow, ncol = c.src.shape
        nblk = steps
        while nrow % nblk or (nrow // nblk) % BF16_SUBLANES:
            assert nblk % 2 == 0, (nrow, steps)
            nblk //= 2
        rows = nrow // nblk
        blk_of = lambda *ids, hold=steps // nblk: step_of(*ids) // hold
        in_specs.append(pl.BlockSpec((None, rows, ncol), lambda *ids, f=blk_of, layer=c.layer: (layer, f(*ids), 0)))
        if c.tile is None:
            out_specs.append(pl.BlockSpec((rows, ncol), lambda *ids, f=blk_of: (f(*ids), 0)))
            out_shapes.append(jax.ShapeDtypeStruct((nrow, ncol), BF16))
        else:
            tk, tn = c.tile
            per = tk // rows
            assert tk % rows == 0 and nrow % tk == 0 and ncol % tn == 0
            out_specs.append(pl.BlockSpec((None, ncol // tn, rows, tn),
                                          lambda *ids, f=blk_of, per=per: (f(*ids) // per, 0, f(*ids) % per, 0)))
            out_shapes.append(jax.ShapeDtypeStruct((nrow // tk, ncol // tn, tk, tn), BF16))
    return in_specs, out_specs, out_shapes


def _side_cast_run(tiles, src_refs, dst_refs):
    for tile, src, dst in zip(tiles, src_refs, dst_refs, strict=True):
        if tile is None:
            dst[...] = src[...].astype(BF16)
        else:
            tn = tile[1]
            for j in range(dst.shape[0]):
                dst[j] = src[:, j * tn:(j + 1) * tn].astype(BF16)


def _ffn_up_kernel(*refs, tiles):
    n = len(tiles)
    h_ref, r_ref, wg_ref, wu_ref = refs[:4]
    o_ref = refs[4 + n]
    h = h_ref[...]
    r = r_ref[...]
    g = r * jnp.dot(h, wg_ref[...], preferred_element_type=F32)
    u = r * jnp.dot(h, wu_ref[...], preferred_element_type=F32)
    o_ref[...] = (g * jax.nn.sigmoid(g) * u).astype(o_ref.dtype)
    _side_cast_run(tiles, refs[4:4 + n], refs[5 + n:])


def ffn_up(h, r, w_up, *, casts=(), tm=1024, tn=512):
    t, d = h.shape
    f = w_up.shape[1] // 2
    ni, nj = t // tm, f // tn
    c_in, c_out, c_shapes = _side_cast_plan(casts, ni * nj, lambda i, j: i * nj + j)
    return pl.pallas_call(
        functools.partial(_ffn_up_kernel, tiles=tuple(c.tile for c in casts)),
        grid=(ni, nj),
        in_specs=[pl.BlockSpec((tm, d), lambda i, j: (i, 0)),
                  pl.BlockSpec((tm, 1), lambda i, j: (i, 0)),
                  pl.BlockSpec((d, tn), lambda i, j: (0, j)),
                  pl.BlockSpec((d, tn), lambda i, j: (0, j + nj))] + c_in,
        out_specs=[pl.BlockSpec((tm, tn), lambda i, j: (i, j))] + c_out,
        out_shape=[jax.ShapeDtypeStruct((t, f), BF16)] + c_shapes,
        compiler_params=_params("arbitrary", "arbitrary"),
        name="ffn_up",
    )(h, r, w_up, w_up, *[c.src for c in casts])


def _mm_norm_res_kernel(*refs, res_scale, nkk, nj, ni, rows, emit_h):
    if emit_h:
        a_hbm, w_ref, x_ref, pg_ref, ng_ref, xo_ref, ho_ref, ro_ref, y_ref, a_buf, sy_ref, ry_ref, sx_ref, a_sem = refs
    else:
        a_hbm, w_ref, x_ref, pg_ref, xo_ref, y_ref, a_buf, sy_ref, ry_ref, a_sem = refs
    b = pl.program_id(0)
    s = pl.program_id(1)
    n_mm = nkk * nj
    tm, d = y_ref.shape
    tk, tn = w_ref.shape
    kk = s // nj
    q = b * nkk + kk
    a_ref = a_buf.at[q % 2]

    def a_copy(blk, part, slot):
        src = a_hbm.at[pl.ds(pl.multiple_of(blk * tm, tm), tm), pl.ds(pl.multiple_of(part * tk, tk), tk)]
        return pltpu.make_async_copy(src, a_buf.at[slot], a_sem.at[slot])

    @pl.when((b == 0) & (s == 0))
    def _():
        a_copy(0, 0, 0).start()

    @pl.when((b < ni) & (s % nj == 0))
    def _():
        a_copy(b, kk, q % 2).wait()

    last_tile = (b == ni - 1) & (kk == nkk - 1)

    @pl.when((b < ni) & (s % nj == A_LEAD_STEPS) & jnp.logical_not(last_tile))
    def _():
        wrap = kk == nkk - 1
        a_copy(jnp.where(wrap, b + 1, b), jnp.where(wrap, 0, kk + 1), (q + 1) % 2).start()

    def tile_cols(e):
        return pl.ds(pl.multiple_of(e * tn, tn), tn)

    def for_row_chunks(body):
        def step(c, carry):
            body(pl.ds(pl.multiple_of(c * rows, rows), rows))
            return carry
        lax.fori_loop(0, tm // rows, step, 0)

    @pl.when((b > 0) & (s == 0))
    def _():
        ry_ref[...] = lax.rsqrt(sy_ref[...] * (1.0 / d) + RMS_EPS)
        if emit_h:
            sx_ref[...] = jnp.zeros_like(sx_ref)

    def residual(r, cols):
        xn = x_ref[r, :] + res_scale * (y_ref[r, cols] * ry_ref[r, :] * pg_ref[...])
        xo_ref[r, :] = xn
        if emit_h:
            ho_ref[r, :] = (xn * ng_ref[...]).astype(BF16)
            sx_ref[r, :] += jnp.sum(xn * xn, axis=-1, keepdims=True)

    first_k = s < nj
    last_k = s >= n_mm - nj

    def matmul_step(first, last):
        cols = tile_cols(s % nj)
        y = jnp.dot(a_ref[...], w_ref[...], preferred_element_type=F32)
        if not first:
            y = y_ref[:, cols] + y
        y_ref[:, cols] = y
        if last:
            sq = jnp.sum(y * y, axis=-1, keepdims=True)
            sy_ref[...] = jnp.where(s == n_mm - nj, sq, sy_ref[...] + sq)

    @pl.when((b > 0) & (b < ni) & first_k)
    def _():
        for c in range(tm // rows):
            residual(pl.ds(c * rows, rows), tile_cols(s))
        matmul_step(True, nkk == 1)

    @pl.when((b == 0) & first_k)
    def _():
        matmul_step(True, nkk == 1)

    @pl.when((b == ni) & first_k)
    def _():
        for_row_chunks(lambda r: residual(r, tile_cols(s)))

    if emit_h:
        @pl.when((b > 0) & (s == nj - 1))
        def _():
            ro_ref[...] = lax.rsqrt(sx_ref[...] * (1.0 / d) + RMS_EPS)

    if nkk > 2:
        @pl.when((b < ni) & jnp.logical_not(first_k) & jnp.logical_not(last_k))
        def _():
            matmul_step(False, False)

    if nkk > 1:
        @pl.when((b < ni) & last_k)
        def _():
            matmul_step(False, True)


def mm_norm_res(a, w_tiled, x, post_g, next_g=None, *, res_scale, tm=1024, rows=256):
    t, kdim = a.shape
    nkk, nj, tkk, tn = w_tiled.shape
    d = nj * tn
    assert kdim == nkk * tkk
    ni = t // tm
    n_mm = nkk * nj
    emit_h = next_g is not None
    mm_step = lambda b, s: jnp.where(b < ni, s, n_mm - 1)
    ep_row = lambda b: jnp.maximum(b - 1, 0)
    ep_tile = lambda b, s: jnp.where(b > 0, jnp.minimum(s, nj - 1), 0)
    tile_spec = pl.BlockSpec((tm, tn), lambda b, s: (ep_row(b), ep_tile(b, s)))
    gain_spec = pl.BlockSpec((1, tn), lambda b, s: (0, ep_tile(b, s)))
    operands = [a, w_tiled, x, post_g.reshape(1, d)]
    in_specs = [pl.BlockSpec(memory_space=pl.ANY),
                pl.BlockSpec((None, None, tkk, tn), lambda b, s: (mm_step(b, s) // nj, mm_step(b, s) % nj, 0, 0)),
                tile_spec, gain_spec]
    out_shape = [jax.ShapeDtypeStruct((t, d), F32)]
    out_specs = [tile_spec]
    row_stat = pltpu.VMEM((tm, 1), F32)
    scratch = [pltpu.VMEM((tm, d), F32), pltpu.VMEM((2, tm, tkk), BF16), row_stat, row_stat]
    if emit_h:
        operands.append(next_g.reshape(1, d))
        in_specs.append(gain_spec)
        out_shape += [jax.ShapeDtypeStruct((t, d), BF16), jax.ShapeDtypeStruct((t, 1), F32)]
        out_specs += [tile_spec, pl.BlockSpec((tm, 1), lambda b, s: (ep_row(b), 0))]
        scratch.append(row_stat)
    scratch.append(pltpu.SemaphoreType.DMA((2,)))
    assert nj > A_LEAD_STEPS
    res = pl.pallas_call(
        functools.partial(_mm_norm_res_kernel, res_scale=res_scale, nkk=nkk, nj=nj, ni=ni, rows=rows,
                          emit_h=emit_h),
        grid=(ni + 1, n_mm),
        in_specs=in_specs,
        out_specs=out_specs,
        out_shape=out_shape,
        scratch_shapes=scratch,
        compiler_params=_params("arbitrary", "arbitrary"),
        name="mm_norm_res",
    )(*operands)
    return (res[0], res[1], res[2]) if emit_h else (res[0], None, None)


def _matmul_kernel(a_ref, r_ref, w_ref, o_ref):
    acc = jnp.dot(a_ref[...], w_ref[...], preferred_element_type=F32)
    o_ref[...] = (r_ref[...] * acc).astype(o_ref.dtype)


def matmul(a, r, w, *, tm=1024, tn=1280):
    t, kdim = a.shape
    n = w.shape[1]
    return pl.pallas_call(
        _matmul_kernel,
        grid=(t // tm, n // tn),
        in_specs=[pl.BlockSpec((tm, kdim), lambda i, j: (i, 0)),
                  pl.BlockSpec((tm, 1), lambda i, j: (i, 0)),
                  pl.BlockSpec((kdim, tn), lambda i, j: (0, j))],
        out_specs=pl.BlockSpec((tm, tn), lambda i, j: (i, j)),
        out_shape=jax.ShapeDtypeStruct((t, n), BF16),
        compiler_params=_params("parallel", "arbitrary"),
        name="in_proj",
    )(a, r, w)


def _gelu_exact(x):
    return 0.5 * x * (1.0 + lax.erf(x * (1.0 / math.sqrt(2.0))))


def _gmlp_kernel(z_ref, lng_ref, lnb_ref, ws_ref, bs_ref, o_ref, *, chunks):
    bw = BRANCH_WIDTH
    row = lax.broadcasted_iota(jnp.int32, (GMLP_CHUNK, GMLP_CHUNK), 0)
    col = lax.broadcasted_iota(jnp.int32, (GMLP_CHUNK, GMLP_CHUNK), 1)
    causal = row >= col
    for c in range(chunks):
        r = slice(c * GMLP_CHUNK, (c + 1) * GMLP_CHUNK)
        u = _gelu_exact(z_ref[r, :bw].astype(F32))
        v = _gelu_exact(z_ref[r, bw:].astype(F32))
        mu = jnp.mean(v, axis=-1, keepdims=True)
        vc = v - mu
        var = jnp.mean(vc * vc, axis=-1, keepdims=True)
        vn = (vc * lax.rsqrt(var + LN_EPS) * lng_ref[...] + lnb_ref[...]).astype(BF16)
        for g in range(GMLP_GROUPS):
            gs = slice(g * GMLP_GROUP_DIM, (g + 1) * GMLP_GROUP_DIM)
            w = jnp.where(causal, ws_ref[g], 0.0).astype(BF16)
            mixed = jnp.dot(w, vn[:, gs], preferred_element_type=F32) + bs_ref[:, g:g + 1]
            o_ref[r, gs] = (u[:, gs] * mixed).astype(o_ref.dtype)


def gmlp(proj, ln_g, ln_b, w_s, b_s, *, chunks=4):
    t = proj.shape[0]
    bw = BRANCH_WIDTH
    tm = chunks * GMLP_CHUNK
    return pl.pallas_call(
        functools.partial(_gmlp_kernel, chunks=chunks),
        grid=(t // tm,),
        in_specs=[pl.BlockSpec((tm, 2 * bw), lambda i: (i, 0)),
                  pl.BlockSpec((1, bw), lambda i: (0, 0)),
                  pl.BlockSpec((1, bw), lambda i: (0, 0)),
                  pl.BlockSpec((GMLP_GROUPS, GMLP_CHUNK, GMLP_CHUNK), lambda i: (0, 0, 0)),
                  pl.BlockSpec((GMLP_CHUNK, GMLP_GROUPS), lambda i: (0, 0))],
        out_specs=pl.BlockSpec((tm, bw), lambda i: (i, 0)),
        out_shape=jax.ShapeDtypeStruct((t, bw), BF16),
        compiler_params=_params("parallel"),
        name="gmlp",
    )(proj, ln_g.reshape(1, bw), ln_b.reshape(1, bw), w_s, b_s.T)


MOBA_MAX_BLOCKS = 32
MOBA_CHUNK_BLOCKS = 4
MOBA_Q_TILE_BLOCKS = 2
MOBA_PREP_ROWS = 512
LOG2_E = math.log2(math.e)
FEAT_SEL, FEAT_ONE, FEAT_BLK, FEAT_OFF, FEAT_END = 0, 32, 35, 38, 41


def _moba_key_features(seq):
    pos = jnp.arange(seq, dtype=jnp.int32)[:, None]
    kb, off = pos // MOBA_BLOCK, pos % MOBA_BLOCK
    lane = jnp.arange(LANES, dtype=jnp.int32)[None, :]
    feat = jnp.where(lane < FEAT_ONE, (lane == kb).astype(F32),
                     jnp.where(lane < FEAT_BLK, 1.0,
                               jnp.where(lane < FEAT_OFF, (kb * MOBA_BLOCK).astype(F32),
                                         jnp.where(lane < FEAT_END, off.astype(F32), 0.0))))
    return feat.astype(BF16)


def _split3(x):
    hi = x.astype(BF16).astype(F32)
    mid = (x - hi).astype(BF16).astype(F32)
    return hi, mid, x - hi - mid


def _moba_prepare(slope, q_ref, k_ref, kfeat_ref, kaug_ref, qaug_ref, kmh_ref, kml_ref, *, nb):
    blk = MOBA_BLOCK
    dh = MOBA_HEAD_DIM
    seq = q_ref.shape[0]
    prep = MOBA_PREP_ROWS
    nbp = MOBA_MAX_BLOCKS
    scale = dh ** -0.5 * LOG2_E
    slope = slope * LOG2_E

    kmh_ref[...] = jnp.zeros_like(kmh_ref)
    kml_ref[...] = jnp.zeros_like(kml_ref)
    for n in range(nb):
        rows = slice(n * blk, (n + 1) * blk)
        kn = k_ref[rows, :]
        kaug_ref[rows, :dh] = kn
        kaug_ref[rows, dh:] = kfeat_ref[rows, :]
        km = jnp.mean(kn.astype(F32), axis=0, keepdims=True)
        hi = km.astype(BF16)
        kmh_ref[n:n + 1, :] = hi
        kml_ref[n:n + 1, :] = (km - hi.astype(F32)).astype(BF16)

    blk_id = lax.broadcasted_iota(jnp.int32, (nbp, prep), 0)
    q_off = lax.broadcasted_iota(jnp.int32, (nbp, prep), 1)
    lane = lax.broadcasted_iota(jnp.int32, (prep, LANES), 1)
    row = lax.broadcasted_iota(jnp.int32, (prep, LANES), 0)
    part = jnp.where(lane < FEAT_BLK, lane - FEAT_ONE, jnp.where(lane < FEAT_OFF, lane - FEAT_BLK, lane - FEAT_OFF))
    pick3 = lambda parts: jnp.where(part == 0, parts[0], jnp.where(part == 1, parts[1], parts[2]))
    slope_feat = pick3(_split3(jnp.full((prep, LANES), slope, F32)))

    def rows_chunk(c, carry):
        row0 = pl.multiple_of(c * prep, prep)
        rows = pl.ds(row0, prep)
        q = q_ref[rows, :]
        gate = _dot_nt(kmh_ref[:nbp, :], q) + _dot_nt(kml_ref[:nbp, :], q)
        q_blk = (row0 + q_off) // blk
        gate = jnp.where(blk_id < q_blk, gate, -jnp.inf)
        sel_bias = jnp.where(blk_id == q_blk, 0.0, MASKED)
        for _ in range(MOBA_TOPK):
            best = jnp.max(gate, axis=0, keepdims=True)
            first = jnp.min(jnp.where(gate == best, blk_id, nbp), axis=0, keepdims=True)
            pick = (blk_id == first) & (best > -jnp.inf)
            sel_bias = jnp.where(pick, 0.0, sel_bias)
            gate = jnp.where(pick, -jnp.inf, gate)
        sel_rows = jnp.concatenate([sel_bias, jnp.zeros((LANES - nbp, prep), F32)], axis=0).T

        t_pos = (row0 + row).astype(F32)
        feat = jnp.where(lane < FEAT_ONE, sel_rows,
                         jnp.where(lane < FEAT_BLK, pick3(_split3(-slope * t_pos)),
                                   jnp.where(lane < FEAT_END, slope_feat, 0.0)))
        qaug_ref[rows, :dh] = (q.astype(F32) * scale).astype(BF16)
        qaug_ref[rows, dh:] = feat.astype(BF16)
        return carry

    lax.fori_loop(0, seq // prep, rows_chunk, 0)


def _moba_kernel(slopes_ref, q_ref, k_ref, v_ref, kfeat_ref, o_ref, kaug_ref, qaug_ref, kmh_ref, kml_ref, s_ref,
                 *, nb):
    blk = MOBA_BLOCK
    dh = MOBA_HEAD_DIM
    qt = MOBA_Q_TILE_BLOCKS * blk
    cb = MOBA_CHUNK_BLOCKS
    span = cb * blk
    h = pl.program_id(1)
    ti = pl.program_id(2)
    first_blk = ti * MOBA_Q_TILE_BLOCKS

    @pl.when(ti == 0)
    def _():
        _moba_prepare(slopes_ref[h], q_ref, k_ref, kfeat_ref, kaug_ref, qaug_ref, kmh_ref, kml_ref, nb=nb)

    q_aug = qaug_ref[pl.ds(pl.multiple_of(ti * qt, qt), qt), :]

    def chunk_rows(p):
        return pl.ds(pl.multiple_of(p * span, span), span)

    def scores(p):
        return _dot_nt(q_aug, kaug_ref[chunk_rows(p), :])

    def lanes_max(m, s):
        for c in range(0, s.shape[1], LANES):
            m = jnp.maximum(m, s[:, c:c + LANES])
        return m

    def pass1(p, mvec):
        s = scores(p)
        s_ref[p] = s
        return lanes_max(mvec, s)

    def chunk_loop(count, body, init):
        carry = lax.fori_loop(0, count // 2, lambda i, c: body(2 * i + 1, body(2 * i, c)), init)
        return lax.fori_loop(count - count % 2, count, body, carry)

    n_past = first_blk // cb
    mvec = chunk_loop(n_past, pass1, jnp.full((qt, LANES), -jnp.inf, F32))
    r_idx = lax.broadcasted_iota(jnp.int32, (blk, blk), 0)
    c_idx = lax.broadcasted_iota(jnp.int32, (blk, blk), 1)
    causal_bias = jnp.where(r_idx >= c_idx, 0.0, MASKED)
    own = first_blk % cb
    bias = jnp.concatenate(
        [jnp.concatenate([causal_bias * (own + r == c).astype(F32) for c in range(cb)], axis=1)
         for r in range(MOBA_Q_TILE_BLOCKS)], axis=0)
    s = scores(n_past) + bias
    s_ref[n_past] = s
    m = jnp.max(lanes_max(mvec, s), axis=-1, keepdims=True)

    def pass2(p, carry):
        l, acc = carry
        e = jnp.exp2(s_ref[p] - m)
        l = l + jnp.sum(e, axis=-1, keepdims=True)
        acc = acc + jnp.dot(e.astype(BF16), v_ref[chunk_rows(p), :], preferred_element_type=F32)
        return l, acc

    l, acc = chunk_loop(n_past + 1, pass2, (jnp.zeros((qt, 1), F32), jnp.zeros((qt, dh), F32)))
    o_ref[...] = (acc / l).astype(o_ref.dtype)


def moba(proj, slopes, *, batch, seq):
    blk = MOBA_BLOCK
    nb = seq // blk
    qt = MOBA_Q_TILE_BLOCKS * blk
    nt = seq // qt
    assert nb % MOBA_CHUNK_BLOCKS == 0 and nb <= MOBA_MAX_BLOCKS and seq % MOBA_PREP_ROWS == 0
    assert MOBA_CHUNK_BLOCKS % MOBA_Q_TILE_BLOCKS == 0
    dh = MOBA_HEAD_DIM
    cq, ck, cv = COL_MOBA_Q // dh, COL_MOBA_K // dh, COL_MOBA_V // dh
    return pl.pallas_call(
        functools.partial(_moba_kernel, nb=nb),
        grid_spec=pltpu.PrefetchScalarGridSpec(
            num_scalar_prefetch=1,
            grid=(batch, MOBA_HEADS, nt),
            in_specs=[pl.BlockSpec((seq, dh), lambda b, h, i, s: (b, cq + h)),
                      pl.BlockSpec((seq, dh), lambda b, h, i, s: (b, ck + h)),
                      pl.BlockSpec((seq, dh), lambda b, h, i, s: (b, cv + h)),
                      pl.BlockSpec((seq, LANES), lambda b, h, i, s: (0, 0))],
            out_specs=pl.BlockSpec((qt, dh), lambda b, h, i, s: (b * nt + i, h)),
            scratch_shapes=[pltpu.VMEM((seq, dh + LANES), BF16), pltpu.VMEM((seq, dh + LANES), BF16),
                            pltpu.VMEM((LANES, dh), BF16), pltpu.VMEM((LANES, dh), BF16),
                            pltpu.VMEM((nb // MOBA_CHUNK_BLOCKS, qt, MOBA_CHUNK_BLOCKS * blk), F32)],
        ),
        out_shape=jax.ShapeDtypeStruct((batch * seq, BRANCH_WIDTH), BF16),
        compiler_params=_params("parallel", "parallel", "arbitrary"),
        name="moba",
    )(slopes, proj, proj, proj, _moba_key_features(seq))


SWA_STEP_BLOCKS = 4


def _swa_kernel(slopes_ref, sinks_ref, q_ref, *refs):
    nsub = SWA_STEP_BLOCKS
    k_refs, v_refs, o_ref = refs[:nsub + 1], refs[nsub + 1:2 * nsub + 2], refs[2 * nsub + 2]
    w = SWA_WINDOW
    dh = SWA_HEAD_DIM
    n = pl.program_id(1)
    scale = dh ** -0.5
    low2 = lax.broadcasted_iota(jnp.int32, (2 * w, LANES), 1) < dh
    t_idx = lax.broadcasted_iota(jnp.int32, (w, 4 * w), 0)
    col = lax.broadcasted_iota(jnp.int32, (w, 4 * w), 1)
    s_idx = col % w
    is_prev = (col % (2 * w)) < w
    second = col >= 2 * w
    dist = (t_idx - s_idx).astype(F32) + jnp.where(is_prev, float(w), 0.0)
    in_window = (is_prev & (s_idx > t_idx)) | (jnp.logical_not(is_prev) & (s_idx <= t_idx))
    in_window_first = in_window & (jnp.logical_not(is_prev) | (n > 0))

    def stacked(prev_ref, cur_ref):
        x = jnp.concatenate([prev_ref[...], cur_ref[...]], axis=0).astype(F32)
        xr = pltpu.roll(x, dh, axis=1)
        z = jnp.zeros_like(x)
        head0 = jnp.concatenate([jnp.where(low2, x, z), jnp.where(low2, z, xr)], axis=0)
        head1 = jnp.concatenate([jnp.where(low2, xr, z), jnp.where(low2, z, x)], axis=0)
        return head0.astype(BF16), head1.astype(BF16)

    pairs = SWA_Q_HEADS // 2
    group = SWA_Q_HEADS // SWA_KV_HEADS
    for sub in range(nsub):
        rows = slice(sub * w, (sub + 1) * w)
        ok = in_window_first if sub == 0 else in_window
        k_st = stacked(k_refs[sub], k_refs[sub + 1])
        v_st = stacked(v_refs[sub], v_refs[sub + 1])
        for pr in range(pairs):
            q = q_ref[rows, pr * LANES:(pr + 1) * LANES]
            kh = (2 * pr) // group
            slope = jnp.where(second, slopes_ref[2 * pr + 1], slopes_ref[2 * pr])
            s = _dot_nt(q, k_st[kh]) * scale
            s = jnp.where(ok, s - slope * dist, MASKED)
            probs = []
            for half in range(2):
                sink = sinks_ref[2 * pr + half]
                sh = s[:, half * 2 * w:(half + 1) * 2 * w]
                m = jnp.maximum(jnp.max(sh, axis=-1, keepdims=True), sink)
                e = jnp.exp(sh - m)
                denom = jnp.sum(e, axis=-1, keepdims=True) + jnp.exp(sink - m)
                probs.append((e * (1.0 / denom)).astype(BF16))
            out = jnp.dot(jnp.concatenate(probs, axis=1), v_st[kh], preferred_element_type=F32)
            o_ref[rows, pr * LANES:(pr + 1) * LANES] = out.astype(o_ref.dtype)


def swa(proj, slopes, sinks, *, batch, seq):
    w = SWA_WINDOW
    nsub = SWA_STEP_BLOCKS
    nblk = seq // w
    nstep = nblk // nsub
    assert nblk % nsub == 0
    bw = BRANCH_WIDTH
    cq = COL_SWA_Q // bw
    ck = COL_SWA_K // SWA_KV_WIDTH
    cv = COL_SWA_V // SWA_KV_WIDTH

    def kv_spec(col_blk, j):
        return pl.BlockSpec((w, SWA_KV_WIDTH),
                            lambda b, n, *_: (b * nblk + jnp.maximum(n * nsub + j - 1, 0), col_blk))

    return pl.pallas_call(
        _swa_kernel,
        grid_spec=pltpu.PrefetchScalarGridSpec(
            num_scalar_prefetch=2,
            grid=(batch, nstep),
            in_specs=([pl.BlockSpec((nsub * w, bw), lambda b, n, *_: (b * nstep + n, cq))]
                      + [kv_spec(ck, j) for j in range(nsub + 1)]
                      + [kv_spec(cv, j) for j in range(nsub + 1)]),
            out_specs=pl.BlockSpec((nsub * w, bw), lambda b, n, *_: (b * nstep + n, 0)),
        ),
        out_shape=jax.ShapeDtypeStruct((batch * seq, bw), BF16),
        compiler_params=_params("parallel", "arbitrary"),
        name="swa",
    )(slopes, sinks, *([proj] * (2 * nsub + 3)))


def _local_mixers_kernel(slopes_ref, sinks_ref, z_ref, lng_ref, lnb_ref, ws_ref, bs_ref, q_ref, *refs):
    ya_ref, yc_ref = refs[-2:]
    _gmlp_kernel(z_ref, lng_ref, lnb_ref, ws_ref, bs_ref, ya_ref, chunks=z_ref.shape[0] // GMLP_CHUNK)
    _swa_kernel(slopes_ref, sinks_ref, q_ref, *refs[:-2], yc_ref)


def local_mixers(proj, ln_g, ln_b, w_s, b_s, slopes, sinks, *, batch, seq):
    w = SWA_WINDOW
    nsub = SWA_STEP_BLOCKS
    rows = nsub * w
    nblk = seq // w
    nstep = nblk // nsub
    assert nblk % nsub == 0 and rows % GMLP_CHUNK == 0
    bw = BRANCH_WIDTH
    cq = COL_SWA_Q // bw
    ck = COL_SWA_K // SWA_KV_WIDTH
    cv = COL_SWA_V // SWA_KV_WIDTH
    const = lambda *shape: pl.BlockSpec(shape, lambda b, n, *_: (0,) * len(shape))
    row_block = lambda col_blk, width: pl.BlockSpec((rows, width), lambda b, n, *_: (b * nstep + n, col_blk))

    def kv_spec(col_blk, j):
        return pl.BlockSpec((w, SWA_KV_WIDTH),
                            lambda b, n, *_: (b * nblk + jnp.maximum(n * nsub + j - 1, 0), col_blk))

    out = jax.ShapeDtypeStruct((batch * seq, bw), BF16)
    return pl.pallas_call(
        _local_mixers_kernel,
        grid_spec=pltpu.PrefetchScalarGridSpec(
            num_scalar_prefetch=2,
            grid=(batch, nstep),
            in_specs=([row_block(0, 2 * bw), const(1, bw), const(1, bw),
                       const(GMLP_GROUPS, GMLP_CHUNK, GMLP_CHUNK), const(GMLP_CHUNK, GMLP_GROUPS),
                       row_block(cq, bw)]
                      + [kv_spec(ck, j) for j in range(nsub + 1)]
                      + [kv_spec(cv, j) for j in range(nsub + 1)]),
            out_specs=[row_block(0, bw), row_block(0, bw)],
        ),
        out_shape=[out, out],
        compiler_params=_params("parallel", "arbitrary"),
        name="local_mixers",
    )(slopes, sinks, proj, ln_g.reshape(1, bw), ln_b.reshape(1, bw), w_s, b_s.T, *([proj] * (2 * nsub + 3)))


def _merge_kernel(*refs, tiles):
    n = len(tiles)
    h_ref, r_ref, wg_ref0, wg_ref1, wg_ref2, bg_ref, ya_ref, yb_ref, yc_ref, wb_ref = refs[:10]
    o_ref = refs[10 + n]
    h = h_ref[...]
    r = r_ref[...]
    merged = None
    for br, (wg_ref, y_ref) in enumerate(((wg_ref0, ya_ref), (wg_ref1, yb_ref), (wg_ref2, yc_ref))):
        logits = r * jnp.dot(h, wg_ref[...], preferred_element_type=F32) + bg_ref[br]
        branch = jnp.dot(y_ref[...], wb_ref[br], preferred_element_type=F32)
        term = jax.nn.sigmoid(logits) * branch
        merged = term if merged is None else merged + term
    o_ref[...] = merged.astype(o_ref.dtype)
    _side_cast_run(tiles, refs[10:10 + n], refs[11 + n:])


def merge(h, r, w_gate, b_gate, y_a, y_b, y_c, w_branch, *, casts=(), tm=1024, tn=256):
    t, d = h.shape
    bw = BRANCH_WIDTH
    ni, nj = t // tm, d // tn
    c_in, c_out, c_shapes = _side_cast_plan(casts, ni * nj, lambda i, j: i * nj + j)
    gate_spec = lambda n: pl.BlockSpec((d, tn), lambda i, j: (0, n * nj + j))
    y_spec = pl.BlockSpec((tm, bw), lambda i, j: (i, 0))
    return pl.pallas_call(
        functools.partial(_merge_kernel, tiles=tuple(c.tile for c in casts)),
        grid=(ni, nj),
        in_specs=[pl.BlockSpec((tm, d), lambda i, j: (i, 0)),
                  pl.BlockSpec((tm, 1), lambda i, j: (i, 0)),
                  gate_spec(0), gate_spec(1), gate_spec(2),
                  pl.BlockSpec((N_BRANCHES, 1, tn), lambda i, j: (0, 0, j)),
                  y_spec, y_spec, y_spec,
                  pl.BlockSpec((N_BRANCHES, bw, tn), lambda i, j: (0, 0, j))] + c_in,
        out_specs=[pl.BlockSpec((tm, tn), lambda i, j: (i, j))] + c_out,
        out_shape=[jax.ShapeDtypeStruct((t, d), BF16)] + c_shapes,
        compiler_params=_params("arbitrary", "arbitrary"),
        name="merge",
    )(h, r, w_gate, w_gate, w_gate, b_gate.reshape(N_BRANCHES, 1, d), y_a, y_b, y_c, w_branch,
      *[c.src for c in casts])


def _alibi_slopes():
    i = jnp.arange(1, N_ALIBI_HEADS + 1, dtype=F32)
    s = jnp.exp2(-8.0 * i / N_ALIBI_HEADS)
    return s[:SWA_Q_HEADS], s[SWA_Q_HEADS:]


def kernel(x, ffn1_pre_g, ffn1_w_up, ffn1_w_down, ffn1_post_g, mix_pre_g, w_in, gmlp_ln_g, gmlp_ln_b,
           gmlp_w_s, gmlp_b_s, swa_sinks, w_gate, b_gate, w_branch, w_out, mix_post_g, ffn2_pre_g,
           ffn2_w_up, ffn2_w_down, ffn2_post_g):
    batch, seq, d = x.shape
    depth = ffn1_pre_g.shape[0]
    swa_slopes, moba_slopes = _alibi_slopes()
    xf = x.reshape(batch * seq, d)
    tiled = lambda w, i: _SideCast(w, i, MM_TILE)
    plain = lambda w, i: _SideCast(w, i, None)
    w_branch_rows = w_branch.reshape(depth, -1, d)
    h, r, w_up1 = prenorm(xf, ffn1_pre_g[0], casts=(plain(ffn1_w_up, 0),))
    for i in range(depth):
        act, w_down1, w_in_b, w_gate_b, w_branch_b, w_out_b = ffn_up(
            h, r, w_up1, casts=(tiled(ffn1_w_down, i), plain(w_in, i), plain(w_gate, i),
                                plain(w_branch_rows, i), tiled(w_out, i)))
        xf, h, r = mm_norm_res(act, w_down1, xf, ffn1_post_g[i], mix_pre_g[i], res_scale=0.5)

        proj = matmul(h, r, w_in_b)
        y_a, y_c = local_mixers(proj, gmlp_ln_g[i], gmlp_ln_b[i], gmlp_w_s[i], gmlp_b_s[i], swa_slopes, swa_sinks[i],
                                batch=batch, seq=seq)
        y_b = moba(proj, moba_slopes, batch=batch, seq=seq)
        merged, w_up2 = merge(h, r, w_gate_b, b_gate[i], y_a, y_b, y_c, w_branch_b.reshape(w_branch.shape[1:]),
                              casts=(plain(ffn2_w_up, i),))
        xf, h, r = mm_norm_res(merged, w_out_b, xf, mix_post_g[i], ffn2_pre_g[i], res_scale=1.0)

        last = i + 1 == depth
        casts = (tiled(ffn2_w_down, i),) + (() if last else (plain(ffn1_w_up, i + 1),))
        act, w_down2, *nxt = ffn_up(h, r, w_up2, casts=casts)
        xf, h, r = mm_norm_res(act, w_down2, xf, ffn2_post_g[i], None if last else ffn1_pre_g[i + 1],
                               res_scale=0.5)
        if not last:
            w_up1 = nxt[0]
    return xf.reshape(batch, seq, d)
```
